```python
import math
import jax, jax.numpy as jnp
from jax import lax
import numpy as np

D_MODEL = 4096
BATCH = 4
SEQ = 2048
DEPTH = 1

CHUNK = 64
D_MIX = D_MODEL
NORM_EPS = 1e-6
D_ATTN = D_MIX // 2
N_HEADS = 16
HEAD_DIM = D_ATTN // N_HEADS
KV_LATENT = 512
N_IDX_HEADS = 16
IDX_DIM = 64
TOPK_MAX = 256
Q_BLOCK = 128
D_POOL = D_MIX - D_ATTN
POOL_WINDOWS = (2, 4, 8, 16)
N_POOL_GROUPS = 4
POOL_GROUP = D_POOL // N_POOL_GROUPS
PROJ_SIZES = (D_ATTN, KV_LATENT, N_IDX_HEADS * IDX_DIM, IDX_DIM, N_IDX_HEADS, D_ATTN, D_POOL, D_POOL)
D_IN = sum(PROJ_SIZES)

kernel_name = "hybrid_dsa_pool_parallel_block"


def _rmsnorm(x, g):
    xf = x.astype(jnp.float32)
    y = xf * lax.rsqrt(jnp.mean(xf * xf, axis=-1, keepdims=True) + NORM_EPS)
    return (y * g.astype(jnp.float32)).astype(x.dtype)


def _layernorm(x, g, b):
    xf = x.astype(jnp.float32)
    mu = jnp.mean(xf, axis=-1, keepdims=True)
    var = jnp.mean(jnp.square(xf - mu), axis=-1, keepdims=True)
    y = (xf - mu) * lax.rsqrt(var + NORM_EPS)
    return (y * g.astype(jnp.float32) + b.astype(jnp.float32)).astype(x.dtype)


def _split_cols(z):
    offs = []
    acc = 0
    for s in PROJ_SIZES[:-1]:
        acc += s
        offs.append(acc)
    return jnp.split(z, offs, axis=-1)


def _dsa_attention(q, c, qi, k_idx, w_idx, k_top):
    B, S = q.shape[0], q.shape[1]
    nb = S // Q_BLOCK

    def blocks(a):
        a = a.reshape((B, nb, Q_BLOCK) + a.shape[2:])
        return jnp.moveaxis(a, 1, 0)

    key_chunk = jnp.arange(S) // CHUNK
    idx_scale = IDX_DIM ** -0.5
    attn_scale = HEAD_DIM ** -0.5

    def one_block(args):
        qb, qib, wb, t0 = args
        q_chunk = (t0 + jnp.arange(Q_BLOCK)) // CHUNK
        admissible = key_chunk[None, :] <= q_chunk[:, None]
        dots = jnp.einsum('bqhd,bsd->bqhs', qib.astype(jnp.float32), k_idx.astype(jnp.float32)) * idx_scale
        iscore = jnp.einsum('bqh,bqhs->bqs', wb.astype(jnp.float32), jax.nn.relu(dots))
        iscore = jnp.where(admissible[None], iscore, -jnp.inf)
        _, sel = lax.top_k(iscore, k_top)
        valid = (sel // CHUNK) <= q_chunk[None, :, None]
        c_sel = jax.vmap(lambda cb, ib: cb[ib])(c, sel)
        q_lat = jnp.einsum('bqhd,hcd->bqhc', qb, w_uk_g)
        scores = jnp.einsum('bqhc,bqkc->bqhk', q_lat, c_sel).astype(jnp.float32) * attn_scale
        scores = jnp.where(valid[:, :, None, :], scores, -jnp.inf)
        p = jax.nn.softmax(scores, axis=-1).astype(c.dtype)
        o_lat = jnp.einsum('bqhk,bqkc->bqhc', p, c_sel)
        return jnp.einsum('bqhc,hcd->bqhd', o_lat, w_uv_g)

    w_uk_g, w_uv_g = _dsa_attention.w_uk, _dsa_attention.w_uv
    out = lax.map(one_block, (blocks(q), blocks(qi), blocks(w_idx), jnp.arange(nb) * Q_BLOCK))
    return jnp.moveaxis(out, 0, 1).reshape(q.shape)


def _sparse_attention(q, c, qi, k_idx, w_idx, w_uk, w_uv, k_top):
    B, S = q.shape[0], q.shape[1]
    nb = S // Q_BLOCK

    def blocks(a):
        a = a.reshape((B, nb, Q_BLOCK) + a.shape[2:])
        return jnp.moveaxis(a, 1, 0)

    key_chunk = jnp.arange(S) // CHUNK
    idx_scale = IDX_DIM ** -0.5
    attn_scale = HEAD_DIM ** -0.5
    kf = k_idx.astype(jnp.float32)

    def one_block(args):
        qb, qib, wb, t0 = args
        q_chunk = (t0 + jnp.arange(Q_BLOCK)) // CHUNK
        admissible = key_chunk[None, :] <= q_chunk[:, None]
        dots = jnp.einsum('bqhd,bsd->bqhs', qib.astype(jnp.float32), kf) * idx_scale
        iscore = jnp.einsum('bqh,bqhs->bqs', wb.astype(jnp.float32), jax.nn.relu(dots))
        iscore = jnp.where(admissible[None], iscore, -jnp.inf)
        _, sel = lax.top_k(iscore, k_top)
        valid = (sel // CHUNK) <= q_chunk[None, :, None]
        c_sel = jax.vmap(lambda cb, ib: cb[ib])(c, sel)
        q_lat = jnp.einsum('bqhd,hcd->bqhc', qb, w_uk)
        scores = jnp.einsum('bqhc,bqkc->bqhk', q_lat, c_sel).astype(jnp.float32) * attn_scale
        scores = jnp.where(valid[:, :, None, :], scores, -jnp.inf)
        p = jax.nn.softmax(scores, axis=-1).astype(c_sel.dtype)
        o_lat = jnp.einsum('bqhk,bqkc->bqhc', p, c_sel)
        return jnp.einsum('bqhc,hcd->bqhd', o_lat, w_uv)

    out = lax.map(one_block, (blocks(q), blocks(qi), blocks(w_idx), jnp.arange(nb) * Q_BLOCK))
    return jnp.moveaxis(out, 0, 1).reshape(q.shape)


def _multiscale_pool(u, w_pool, pool_scale):
    B, S = u.shape[0], u.shape[1]
    p = u.reshape(B, S, N_POOL_GROUPS, POOL_GROUP).astype(jnp.float32)
    cs = jnp.concatenate([jnp.zeros((B, 1, N_POOL_GROUPS, POOL_GROUP), jnp.float32), jnp.cumsum(p, axis=1)], axis=1)
    pos = jnp.arange(S)
    win = jnp.array(POOL_WINDOWS, dtype=jnp.int32)
    start = jnp.maximum(pos[:, None] + 1 - win[None, :], 0)
    count = (pos[:, None] + 1 - start).astype(jnp.float32)
    gidx = jnp.broadcast_to(jnp.arange(N_POOL_GROUPS)[None, :], (S, N_POOL_GROUPS))
    cs_start = cs[:, start, gidx]
    mean = (cs[:, 1:] - cs_start) / count[None, :, :, None]
    mixed = (mean - p).astype(u.dtype)
    y = jnp.einsum('bsgc,gcd->bsgd', mixed, w_pool) * pool_scale.reshape(N_POOL_GROUPS, POOL_GROUP)
    return y.reshape(B, S, D_POOL)


def setup_inputs(seed: int = 0) -> dict:
    key = jax.random.key(seed)
    ks = jax.random.split(key, 14)
    f = jnp.float32
    x = jax.random.normal(ks[0], (BATCH, SEQ, D_MODEL), f)
    pre_norm = 1.0 + 0.02 * jax.random.normal(ks[1], (DEPTH, D_MODEL), f)
    w_in = jax.random.normal(ks[2], (DEPTH, D_MODEL, D_IN), f) * D_MODEL ** -0.5
    kv_norm = 1.0 + 0.02 * jax.random.normal(ks[3], (DEPTH, KV_LATENT), f)
    w_uk = jax.random.normal(ks[4], (DEPTH, N_HEADS, KV_LATENT, HEAD_DIM), f) * KV_LATENT ** -0.5
    w_uv = jax.random.normal(ks[5], (DEPTH, N_HEADS, KV_LATENT, HEAD_DIM), f) * KV_LATENT ** -0.5
    idx_k_norm_g = 1.0 + 0.02 * jax.random.normal(ks[6], (DEPTH, IDX_DIM), f)
    idx_k_norm_b = 0.02 * jax.random.normal(ks[7], (DEPTH, IDX_DIM), f)
    w_pool = jax.random.normal(ks[8], (DEPTH, N_POOL_GROUPS, POOL_GROUP, POOL_GROUP), f) * POOL_GROUP ** -0.5
    pool_scale = 1.0 + 0.02 * jax.random.normal(ks[9], (DEPTH, D_POOL), f)
    w_out = jax.random.normal(ks[10], (DEPTH, D_MIX, D_MODEL), f) * D_MIX ** -0.5
    post_norm = 1.0 + 0.02 * jax.random.normal(ks[11], (DEPTH, D_MODEL), f)
    return {"x": x, "pre_norm": pre_norm, "w_in": w_in, "kv_norm": kv_norm, "w_uk": w_uk, "w_uv": w_uv,
            "idx_k_norm_g": idx_k_norm_g, "idx_k_norm_b": idx_k_norm_b, "w_pool": w_pool,
            "pool_scale": pool_scale, "w_out": w_out, "post_norm": post_norm}


def reference(x, pre_norm, w_in, kv_norm, w_uk, w_uv, idx_k_norm_g, idx_k_norm_b, w_pool, pool_scale, w_out, post_norm):
    B, S, _ = x.shape
    k_top = min(TOPK_MAX, S // 4)
    for l in range(DEPTH):
        h = _rmsnorm(x, pre_norm[l])
        z = jnp.einsum('bsd,de->bse', h, w_in[l])
        q, c, qi, kix, wix, gate_a, pool_in, gate_b = _split_cols(z)
        q = q.reshape(B, S, N_HEADS, HEAD_DIM)
        c = _rmsnorm(c, kv_norm[l])
        qi = qi.reshape(B, S, N_IDX_HEADS, IDX_DIM)
        kix = _layernorm(kix, idx_k_norm_g[l], idx_k_norm_b[l])
        wix = wix * (N_IDX_HEADS ** -0.5)
        ya = _sparse_attention(q, c, qi, kix, wix, w_uk[l], w_uv[l], k_top).reshape(B, S, D_ATTN)
        ya = ya * jax.nn.silu(gate_a)
        yb = _multiscale_pool(pool_in, w_pool[l], pool_scale[l]) * jax.nn.silu(gate_b)
        y = jnp.einsum('bse,ed->bsd', jnp.concatenate([ya, yb], axis=-1), w_out[l])
        x = x + _rmsnorm(y, post_norm[l])
    return x
```

```python
import functools
import math

import jax
import jax.numpy as jnp
from jax import lax
from jax.experimental import pallas as pl
from jax.experimental.pallas import tpu as pltpu

F32 = jnp.float32
BF16 = jnp.bfloat16

NORM_EPS = 1e-6
CHUNK = 64
N_HEADS = 16
HEAD_DIM = 128
KV_LATENT = 512
N_IDX_HEADS = 16
IDX_DIM = 64
TOPK_MAX = 256
POOL_WINDOWS = (2, 4, 8, 16)
POOL_GROUP = 512

LANES = 128
Q_TILE = 128
KEY_BLOCK = 256
HEADS_PER_GROUP = 4
VMEM_LIMIT = 56 * 1024 * 1024


def _cparams(sem):
    return pltpu.CompilerParams(dimension_semantics=sem, vmem_limit_bytes=VMEM_LIMIT)


def _prenorm_kernel(x_ref, g_ref, o_ref):
    x = x_ref[...]
    ms = jnp.mean(x * x, axis=-1, keepdims=True)
    o_ref[...] = (x * lax.rsqrt(ms + NORM_EPS) * g_ref[...]).astype(o_ref.dtype)


def _prenorm(x2, g, tm=256):
    T, D = x2.shape
    return pl.pallas_call(
        _prenorm_kernel,
        grid=(T // tm,),
        in_specs=[pl.BlockSpec((tm, D), lambda i: (i, 0)),
                  pl.BlockSpec((1, D), lambda i: (0, 0))],
        out_specs=pl.BlockSpec((tm, D), lambda i: (i, 0)),
        out_shape=jax.ShapeDtypeStruct((T, D), BF16),
        compiler_params=_cparams(("parallel",)),
        name="prenorm",
    )(x2, g.reshape(1, D))


def _matmul_kernel(h_ref, w_ref, o_ref):
    o_ref[...] = jnp.dot(h_ref[...], w_ref[...], preferred_element_type=F32).astype(o_ref.dtype)


def _matmul(h, w, out_dtype, tm, tn, name):
    T, K = h.shape
    N = w.shape[1]
    tm = min(tm, T)
    return pl.pallas_call(
        _matmul_kernel,
        grid=(T // tm, N // tn),
        in_specs=[pl.BlockSpec((tm, K), lambda i, j: (i, 0)),
                  pl.BlockSpec((K, tn), lambda i, j: (0, j))],
        out_specs=pl.BlockSpec((tm, tn), lambda i, j: (i, j)),
        out_shape=jax.ShapeDtypeStruct((T, N), out_dtype),
        compiler_params=_cparams(("parallel", "arbitrary")),
        name=name,
    )(h, w)


def _latent_kernel(h_ref, w_ref, kvg_ref, lng_ref, lnb_ref, c_ref, ct_ref, kix_ref, wixt_ref):
    z = jnp.dot(h_ref[...], w_ref[...], preferred_element_type=F32)
    c = z[:, :KV_LATENT]
    cn = c * lax.rsqrt(jnp.mean(c * c, axis=-1, keepdims=True) + NORM_EPS) * kvg_ref[...]
    c_ref[...] = cn.astype(c_ref.dtype)
    for u in range(ct_ref.shape[0]):
        ct_ref[u] = cn[u * KEY_BLOCK:(u + 1) * KEY_BLOCK, :].T.astype(ct_ref.dtype)
    kw = z[:, KV_LATENT:]
    lane = lax.broadcasted_iota(jnp.int32, kw.shape, 1)
    is_k = lane < IDX_DIM
    mu = jnp.sum(jnp.where(is_k, kw, 0.0), axis=-1, keepdims=True) * (1.0 / IDX_DIM)
    d = jnp.where(is_k, kw - mu, 0.0)
    var = jnp.sum(d * d, axis=-1, keepdims=True) * (1.0 / IDX_DIM)
    kn = d * lax.rsqrt(var + NORM_EPS) * lng_ref[...] + lnb_ref[...]
    kix_ref[...] = jnp.concatenate([kn, pltpu.roll(kn, IDX_DIM, 1)], axis=1).astype(kix_ref.dtype)
    wixt_ref[...] = kw.T[IDX_DIM:IDX_DIM + N_IDX_HEADS, :] * (N_IDX_HEADS ** -0.5)


def _latent_proj(h, w_ckw, kv_norm, ln_g, ln_b, B, S, tm=512):
    T, K = h.shape
    nS = S // tm
    pad = jnp.zeros((LANES - IDX_DIM,), F32)
    lng = jnp.concatenate([ln_g, pad]).reshape(1, LANES)
    lnb = jnp.concatenate([ln_b, pad]).reshape(1, LANES)
    return pl.pallas_call(
        _latent_kernel,
        grid=(B, nS),
        in_specs=[pl.BlockSpec((tm, K), lambda b, i: (b * nS + i, 0)),
                  pl.BlockSpec(w_ckw.shape, lambda b, i: (0, 0)),
                  pl.BlockSpec((1, KV_LATENT), lambda b, i: (0, 0)),
                  pl.BlockSpec((1, LANES), lambda b, i: (0, 0)),
                  pl.BlockSpec((1, LANES), lambda b, i: (0, 0))],
        out_specs=[pl.BlockSpec((None, tm, KV_LATENT), lambda b, i: (b, i, 0)),
                   pl.BlockSpec((None, tm // KEY_BLOCK, KV_LATENT, KEY_BLOCK), lambda b, i: (b, i, 0, 0)),
                   pl.BlockSpec((None, tm, 2 * LANES), lambda b, i: (b, i, 0)),
                   pl.BlockSpec((None, N_IDX_HEADS, tm), lambda b, i: (b, 0, i))],
        out_shape=[jax.ShapeDtypeStruct((B, S, KV_LATENT), BF16),
                   jax.ShapeDtypeStruct((B, S // KEY_BLOCK, KV_LATENT, KEY_BLOCK), BF16),
                   jax.ShapeDtypeStruct((B, S, 2 * LANES), BF16),
                   jax.ShapeDtypeStruct((B, N_IDX_HEADS, S), F32)],
        compiler_params=_cparams(("parallel", "arbitrary")),
        name="latent_proj",
    )(h, w_ckw, kv_norm.reshape(1, KV_LATENT), lng, lnb)


def _sortable_to_f32(key):
    bits = key ^ ((key >> 31) & jnp.int32(0x7FFFFFFF))
    return pltpu.bitcast(bits, F32)


def _fold8(a):
    return jnp.sum(a.reshape(a.shape[0] // 8, 8, a.shape[1]), axis=0)


def _attn_kernel(q_ref, qi_ref, wixt_ref, kix_ref, c_ref, ct_ref, wuk_ref, wuvt_ref, ga_ref,
                 o_ref, isc_ref, bias_ref, s_ref, qlat_ref, acc_ref, *, k_top):
    i = pl.program_id(1)
    t0 = i * Q_TILE
    nkb = (t0 + Q_TILE + KEY_BLOCK - 1) // KEY_BLOCK
    neg_inf = jnp.float32(-jnp.inf)

    t_idx = t0 + lax.broadcasted_iota(jnp.int32, (KEY_BLOCK, Q_TILE), 1)
    q_chunk = t_idx // CHUNK

    def admissible(kb):
        s_idx = kb * KEY_BLOCK + lax.broadcasted_iota(jnp.int32, (KEY_BLOCK, Q_TILE), 0)
        return (s_idx // CHUNK) <= q_chunk

    wix = wixt_ref[...] * (IDX_DIM ** -0.5)

    def idx_body(kb, carry):
        r0 = pl.multiple_of(kb * KEY_BLOCK, KEY_BLOCK)
        kx = kix_ref[pl.ds(r0, KEY_BLOCK), :]
        kx_even, kx_odd = kx[:, :LANES], kx[:, LANES:]
        acc = jnp.zeros((KEY_BLOCK, Q_TILE), F32)
        for g in range(N_IDX_HEADS // 4):
            rhs = jnp.concatenate([qi_ref[:, (2 * g) * LANES:(2 * g + 1) * LANES],
                                   qi_ref[:, (2 * g + 1) * LANES:(2 * g + 2) * LANES]], axis=0)
            for par, kxp in ((0, kx_even), (1, kx_odd)):
                d = lax.dot_general(kxp, rhs, (((1,), (1,)), ((), ())), preferred_element_type=F32)
                for half in range(2):
                    h = 4 * g + 2 * half + par
                    acc = acc + wix[h:h + 1, :] * jnp.maximum(d[:, half * Q_TILE:(half + 1) * Q_TILE], 0.0)
        isc_ref[pl.ds(r0, KEY_BLOCK), :] = jnp.where(admissible(kb), acc, neg_inf)
        return carry

    lax.fori_loop(0, nkb, idx_body, 0)

    def count(pred):
        def body(kb, cnt):
            r0 = pl.multiple_of(kb * KEY_BLOCK, KEY_BLOCK)
            v = isc_ref[pl.ds(r0, KEY_BLOCK), :]
            return cnt + _fold8(jnp.where(pred(v), 1, 0).astype(jnp.int32))
        c8 = lax.fori_loop(0, nkb, body, jnp.zeros((8, Q_TILE), jnp.int32))
        return jnp.sum(c8, axis=0, keepdims=True)

    int_min = jnp.int32(-2 ** 31)

    def search_body(step, key):
        bit = 31 - step
        inc = jnp.where(bit == 31, int_min, jnp.left_shift(jnp.int32(1), jnp.minimum(bit, 30)))
        cand = key + inc
        thr = _sortable_to_f32(cand)
        n_ge = count(lambda v: v >= thr)
        return jnp.where(n_ge >= k_top, cand, key)

    key = lax.fori_loop(0, 32, search_body, jnp.full((1, Q_TILE), int_min, jnp.int32))
    thr = _sortable_to_f32(key)
    n_gt = count(lambda v: v > thr)
    need = (k_top - n_gt).astype(F32)
    n_adm = (t_idx[0:1, :] // CHUNK + 1) * CHUNK
    thr = jnp.where(n_adm <= k_top, neg_inf, thr)

    tri = (lax.broadcasted_iota(jnp.int32, (KEY_BLOCK, KEY_BLOCK), 1)
           <= lax.broadcasted_iota(jnp.int32, (KEY_BLOCK, KEY_BLOCK), 0)).astype(BF16)

    def sel_body(kb, tie_carry):
        r0 = pl.multiple_of(kb * KEY_BLOCK, KEY_BLOCK)
        v = isc_ref[pl.ds(r0, KEY_BLOCK), :]
        tie = (v == thr).astype(BF16)
        rank = jnp.dot(tri, tie, preferred_element_type=F32) + tie_carry
        sel = ((v > thr) | ((v == thr) & (rank <= need))) & admissible(kb)
        bias_ref[pl.ds(r0, KEY_BLOCK), :] = jnp.where(sel, 0.0, neg_inf)
        return rank[KEY_BLOCK - 1:KEY_BLOCK, :]

    lax.fori_loop(0, nkb, sel_body, jnp.zeros((1, Q_TILE), F32))

    for h in range(N_HEADS):
        qh = q_ref[:, h * HEAD_DIM:(h + 1) * HEAD_DIM]
        ql = lax.dot_general(wuk_ref[h], qh, (((1,), (1,)), ((), ())), preferred_element_type=F32)
        qlat_ref[:, h * Q_TILE:(h + 1) * Q_TILE] = ql.astype(qlat_ref.dtype)

    attn_scale = HEAD_DIM ** -0.5
    gw = HEADS_PER_GROUP * Q_TILE
    for g in range(N_HEADS // HEADS_PER_GROUP):
        qlat_g = qlat_ref.at[:, g * gw:(g + 1) * gw]

        def score_body(kb, m8):
            r0 = pl.multiple_of(kb * KEY_BLOCK, KEY_BLOCK)
            s = jnp.dot(c_ref[pl.ds(r0, KEY_BLOCK), :], qlat_g[...], preferred_element_type=F32)
            b = bias_ref[pl.ds(r0, KEY_BLOCK), :]
            s = s * attn_scale + jnp.concatenate([b] * HEADS_PER_GROUP, axis=1)
            s_ref[pl.ds(r0, KEY_BLOCK), :] = s
            return jnp.maximum(m8, jnp.max(s.reshape(KEY_BLOCK // 8, 8, gw), axis=0))

        m8 = lax.fori_loop(0, nkb, score_body, jnp.full((8, gw), neg_inf, F32))
        m = jnp.max(m8, axis=0, keepdims=True)

        acc_ref[...] = jnp.zeros_like(acc_ref)

        def pv_body(kb, l8):
            r0 = pl.multiple_of(kb * KEY_BLOCK, KEY_BLOCK)
            p = jnp.exp(s_ref[pl.ds(r0, KEY_BLOCK), :] - m)
            acc_ref[...] += jnp.dot(ct_ref[kb], p.astype(BF16), preferred_element_type=F32)
            return l8 + _fold8(p)

        l8 = lax.fori_loop(0, nkb, pv_body, jnp.zeros((8, gw), F32))
        inv_l = 1.0 / jnp.sum(l8, axis=0, keepdims=True)
        o_lat = (acc_ref[...] * inv_l).astype(BF16)
        for hh in range(HEADS_PER_GROUP):
            h = g * HEADS_PER_GROUP + hh
            out_t = jnp.dot(wuvt_ref[h], o_lat[:, hh * Q_TILE:(hh + 1) * Q_TILE],
                            preferred_element_type=F32)
            gate = ga_ref[:, h * HEAD_DIM:(h + 1) * HEAD_DIM]
            o_ref[:, h * HEAD_DIM:(h + 1) * HEAD_DIM] = (
                out_t.T * (gate * jax.nn.sigmoid(gate))).astype(o_ref.dtype)


def _sparse_attention(qqi, wixt, kix, c, ct, w_uk, w_uvt, gates, k_top):
    B, S, _ = qqi.shape
    DA = N_HEADS * HEAD_DIM
    DI = N_IDX_HEADS * IDX_DIM
    nq = S // Q_TILE
    gw = HEADS_PER_GROUP * Q_TILE
    return pl.pallas_call(
        functools.partial(_attn_kernel, k_top=k_top),
        grid=(B, nq),
        in_specs=[pl.BlockSpec((None, Q_TILE, DA), lambda b, i: (b, i, 0)),
                  pl.BlockSpec((None, Q_TILE, DI), lambda b, i: (b, i, DA // DI)),
                  pl.BlockSpec((None, N_IDX_HEADS, Q_TILE), lambda b, i: (b, 0, i)),
                  pl.BlockSpec((None, S, 2 * LANES), lambda b, i: (b, 0, 0)),
                  pl.BlockSpec((None, S, KV_LATENT), lambda b, i: (b, 0, 0)),
                  pl.BlockSpec((None,) + ct.shape[1:], lambda b, i: (b, 0, 0, 0)),
                  pl.BlockSpec(w_uk.shape, lambda b, i: (0, 0, 0)),
                  pl.BlockSpec(w_uvt.shape, lambda b, i: (0, 0, 0)),
                  pl.BlockSpec((Q_TILE, DA), lambda b, i: (b * nq + i, 0))],
        out_specs=pl.BlockSpec((Q_TILE, DA), lambda b, i: (b * nq + i, 0)),
        out_shape=jax.ShapeDtypeStruct((B * S, DA), BF16),
        scratch_shapes=[pltpu.VMEM((S, Q_TILE), F32),
                        pltpu.VMEM((S, Q_TILE), F32),
                        pltpu.VMEM((S, gw), F32),
                        pltpu.VMEM((KV_LATENT, N_HEADS * Q_TILE), BF16),
                        pltpu.VMEM((KV_LATENT, gw), F32)],
        compiler_params=_cparams(("parallel", "arbitrary")),
        name="sparse_attention",
    )(qqi, qqi, wixt, kix, c, ct, w_uk, w_uvt, gates)


POOL_HALO = 16


def _pool_kernel(pin_ref, halo_ref, gate_ref, wp_ref, ps_ref, o_ref):
    i = pl.program_id(1)
    ts = pin_ref.shape[0]
    t = i * ts + lax.broadcasted_iota(jnp.int32, (ts, POOL_GROUP), 0)
    for g, win in enumerate(POOL_WINDOWS):
        cols = slice(g * POOL_GROUP, (g + 1) * POOL_GROUP)
        cur = pin_ref[:, cols]
        halo = jnp.where(i == 0, 0.0, halo_ref[:, cols])
        ssum = jnp.concatenate([halo, cur], axis=0)
        span = 1
        while span < win:
            ssum = ssum + pltpu.roll(ssum, span, 0)
            span *= 2
        count = jnp.minimum(t + 1, win).astype(F32)
        mixed = ssum[POOL_HALO:, :] / count - cur
        y = jnp.dot(mixed.astype(BF16), wp_ref[g], preferred_element_type=F32)
        gate = gate_ref[:, cols]
        o_ref[:, cols] = (y * ps_ref[:, cols] * (gate * jax.nn.sigmoid(gate))).astype(o_ref.dtype)


def _pool_mixer(gates, w_pool, pool_scale, B, S, ts=256):
    T = gates.shape[0]
    D = gates.shape[1] // 3
    nS = S // ts
    hb = ts // POOL_HALO
    return pl.pallas_call(
        _pool_kernel,
        grid=(B, nS),
        in_specs=[pl.BlockSpec((ts, D), lambda b, i: (b * nS + i, 1)),
                  pl.BlockSpec((POOL_HALO, D), lambda b, i: (jnp.maximum((b * nS + i) * hb - 1, 0), 1)),
                  pl.BlockSpec((ts, D), lambda b, i: (b * nS + i, 2)),
                  pl.BlockSpec(w_pool.shape, lambda b, i: (0, 0, 0)),
                  pl.BlockSpec((1, D), lambda b, i: (0, 0))],
        out_specs=pl.BlockSpec((ts, D), lambda b, i: (b * nS + i, 0)),
        out_shape=jax.ShapeDtypeStruct((T, D), BF16),
        compiler_params=_cparams(("parallel", "arbitrary")),
        name="pool_mixer",
    )(gates, gates, gates, w_pool, pool_scale.reshape(1, D))


def _outproj_kernel(ya_ref, yb_ref, wa_ref, wb_ref, x_ref, g_ref, o_ref):
    j = pl.program_id(1)
    tn = wa_ref.shape[1]
    n_col = o_ref.shape[1] // tn
    y = (jnp.dot(ya_ref[...], wa_ref[...], preferred_element_type=F32)
         + jnp.dot(yb_ref[...], wb_ref[...], preferred_element_type=F32))
    for jj in range(n_col):
        @pl.when(j == jj)
        def _():
            o_ref[:, jj * tn:(jj + 1) * tn] = y

    @pl.when(j == n_col - 1)
    def _():
        ss = jnp.zeros((o_ref.shape[0], 1), F32)
        for jj in range(n_col):
            yj = o_ref[:, jj * tn:(jj + 1) * tn]
            ss = ss + jnp.sum(yj * yj, axis=-1, keepdims=True)
        rs = lax.rsqrt(ss * (1.0 / (n_col * tn)) + NORM_EPS)
        for jj in range(n_col):
            cols = slice(jj * tn, (jj + 1) * tn)
            o_ref[:, cols] = x_ref[:, cols] + o_ref[:, cols] * rs * g_ref[:, cols]


def _out_proj(ya, yb, wa, wb, x2, post_g, tm=512, tn=512):
    T, D = x2.shape
    DA = ya.shape[1]
    return pl.pallas_call(
        _outproj_kernel,
        grid=(T // tm, D // tn),
        in_specs=[pl.BlockSpec((tm, DA), lambda i, j: (i, 0)),
                  pl.BlockSpec((tm, yb.shape[1]), lambda i, j: (i, 0)),
                  pl.BlockSpec((DA, tn), lambda i, j: (0, j)),
                  pl.BlockSpec((yb.shape[1], tn), lambda i, j: (0, j)),
                  pl.BlockSpec((tm, D), lambda i, j: (i, 0)),
                  pl.BlockSpec((1, D), lambda i, j: (0, 0))],
        out_specs=pl.BlockSpec((tm, D), lambda i, j: (i, 0)),
        out_shape=jax.ShapeDtypeStruct((T, D), F32),
        compiler_params=_cparams(("parallel", "arbitrary")),
        name="out_proj",
    )(ya, yb, wa, wb, x2, post_g.reshape(1, D))


def _layer(x2, B, S, pre_g, w_in, kv_g, w_uk, w_uv, ln_g, ln_b, w_pool, pool_scale, w_out, post_g, k_top):
    d_attn = N_HEADS * HEAD_DIM
    n_qi = N_IDX_HEADS * IDX_DIM
    o_c = d_attn
    o_qi = o_c + KV_LATENT
    o_k = o_qi + n_qi
    o_w = o_k + IDX_DIM
    o_g = o_w + N_IDX_HEADS
    w_qqi = jnp.concatenate([w_in[:, :o_c], w_in[:, o_qi:o_k]], axis=1).astype(BF16)
    w_ckw = jnp.concatenate([w_in[:, o_c:o_qi], w_in[:, o_k:o_g],
                             jnp.zeros((w_in.shape[0], LANES - IDX_DIM - N_IDX_HEADS), F32)], axis=1).astype(BF16)
    w_gates = w_in[:, o_g:].astype(BF16)
    w_uk_b = w_uk.astype(BF16)
    w_uvt_b = jnp.swapaxes(w_uv, 1, 2).astype(BF16)
    w_pool_b = w_pool.astype(BF16)
    wa = w_out[:d_attn].astype(BF16)
    wb = w_out[d_attn:].astype(BF16)

    h = _prenorm(x2, pre_g)
    qqi = _matmul(h, w_qqi, BF16, 1024, 512, "proj_q_qi")
    gates = _matmul(h, w_gates, F32, 1024, 512, "proj_gates")
    c, ct, kix, wixt = _latent_proj(h, w_ckw, kv_g, ln_g, ln_b, B, S)
    ya = _sparse_attention(qqi.reshape(B, S, d_attn + n_qi), wixt, kix, c, ct, w_uk_b, w_uvt_b, gates, k_top)
    yb = _pool_mixer(gates, w_pool_b, pool_scale, B, S)
    return _out_proj(ya, yb, wa, wb, x2, post_g)


def kernel(x, pre_norm, w_in, kv_norm, w_uk, w_uv, idx_k_norm_g, idx_k_norm_b, w_pool, pool_scale, w_out, post_norm):
    B, S, D = x.shape
    k_top = min(TOPK_MAX, S // 4)
    x2 = x.reshape(B * S, D)
    for l in range(pre_norm.shape[0]):
        x2 = _layer(x2, B, S, pre_norm[l], w_in[l], kv_norm[l], w_uk[l], w_uv[l], idx_k_norm_g[l],
                    idx_k_norm_b[l], w_pool[l], pool_scale[l], w_out[l], post_norm[l], k_top)
    return x2.reshape(B, S, D)
```

```python
import functools
import math

import jax
import jax.numpy as jnp
from jax import lax
from jax.experimental import pallas as pl
from jax.experimental.pallas import tpu as pltpu

F32 = jnp.float32
BF16 = jnp.bfloat16

NORM_EPS = 1e-6
CHUNK = 64
N_HEADS = 16
HEAD_DIM = 128
KV_LATENT = 512
N_IDX_HEADS = 16
IDX_DIM = 64
TOPK_MAX = 256
POOL_WINDOWS = (2, 4, 8, 16)
POOL_GROUP = 512

LANES = 128
Q_TILE = 128
KEY_BLOCK = 256
HEADS_PER_GROUP = 4
M_FLOOR = -1e30
VMEM_LIMIT = 56 * 1024 * 1024


def _cparams(sem):
    return pltpu.CompilerParams(dimension_semantics=sem, vmem_limit_bytes=VMEM_LIMIT)


def _prenorm_kernel(x_ref, g_ref, o_ref):
    x = x_ref[...]
    ms = jnp.mean(x * x, axis=-1, keepdims=True)
    o_ref[...] = (x * lax.rsqrt(ms + NORM_EPS) * g_ref[...]).astype(o_ref.dtype)


def _prenorm(x2, g, tm=256):
    T, D = x2.shape
    return pl.pallas_call(
        _prenorm_kernel,
        grid=(T // tm,),
        in_specs=[pl.BlockSpec((tm, D), lambda i: (i, 0)),
                  pl.BlockSpec((1, D), lambda i: (0, 0))],
        out_specs=pl.BlockSpec((tm, D), lambda i: (i, 0)),
        out_shape=jax.ShapeDtypeStruct((T, D), BF16),
        compiler_params=_cparams(("parallel",)),
        name="prenorm",
    )(x2, g.reshape(1, D))


def _matmul_kernel(h_ref, w_ref, o_ref):
    o_ref[...] = jnp.dot(h_ref[...], w_ref[...], preferred_element_type=F32).astype(o_ref.dtype)


def _matmul(h, w, out_dtype, tm, tn, name):
    T, K = h.shape
    N = w.shape[1]
    tm = min(tm, T)
    return pl.pallas_call(
        _matmul_kernel,
        grid=(T // tm, N // tn),
        in_specs=[pl.BlockSpec((tm, K), lambda i, j: (i, 0)),
                  pl.BlockSpec((K, tn), lambda i, j: (0, j))],
        out_specs=pl.BlockSpec((tm, tn), lambda i, j: (i, j)),
        out_shape=jax.ShapeDtypeStruct((T, N), out_dtype),
        compiler_params=_cparams(("parallel", "arbitrary")),
        name=name,
    )(h, w)


def _latent_kernel(h_ref, w_ref, kvg_ref, lng_ref, lnb_ref, c_ref, ct_ref, kix_ref, wixt_ref):
    z = jnp.dot(h_ref[...], w_ref[...], preferred_element_type=F32)
    c = z[:, :KV_LATENT]
    cn = c * lax.rsqrt(jnp.mean(c * c, axis=-1, keepdims=True) + NORM_EPS) * kvg_ref[...]
    c_ref[...] = cn.astype(c_ref.dtype)
    for u in range(ct_ref.shape[0]):
        ct_ref[u] = cn[u * KEY_BLOCK:(u + 1) * KEY_BLOCK, :].T.astype(ct_ref.dtype)
    kw = z[:, KV_LATENT:]
    lane = lax.broadcasted_iota(jnp.int32, kw.shape, 1)
    is_k = lane < IDX_DIM
    mu = jnp.sum(jnp.where(is_k, kw, 0.0), axis=-1, keepdims=True) * (1.0 / IDX_DIM)
    d = jnp.where(is_k, kw - mu, 0.0)
    var = jnp.sum(d * d, axis=-1, keepdims=True) * (1.0 / IDX_DIM)
    kn = d * lax.rsqrt(var + NORM_EPS) * lng_ref[...] + lnb_ref[...]
    kix_ref[...] = jnp.concatenate([kn, pltpu.roll(kn, IDX_DIM, 1)], axis=1).astype(kix_ref.dtype)
    wixt_ref[...] = kw.T[IDX_DIM:IDX_DIM + N_IDX_HEADS, :] * (N_IDX_HEADS ** -0.5)


def _latent_proj(h, w_ckw, kv_norm, ln_g, ln_b, B, S, tm=512):
    T, K = h.shape
    nS = S // tm
    pad = jnp.zeros((LANES - IDX_DIM,), F32)
    lng = jnp.concatenate([ln_g, pad]).reshape(1, LANES)
    lnb = jnp.concatenate([ln_b, pad]).reshape(1, LANES)
    return pl.pallas_call(
        _latent_kernel,
        grid=(B, nS),
        in_specs=[pl.BlockSpec((tm, K), lambda b, i: (b * nS + i, 0)),
                  pl.BlockSpec(w_ckw.shape, lambda b, i: (0, 0)),
                  pl.BlockSpec((1, KV_LATENT), lambda b, i: (0, 0)),
                  pl.BlockSpec((1, LANES), lambda b, i: (0, 0)),
                  pl.BlockSpec((1, LANES), lambda b, i: (0, 0))],
        out_specs=[pl.BlockSpec((None, tm, KV_LATENT), lambda b, i: (b, i, 0)),
                   pl.BlockSpec((None, tm // KEY_BLOCK, KV_LATENT, KEY_BLOCK), lambda b, i: (b, i, 0, 0)),
                   pl.BlockSpec((None, tm, 2 * LANES), lambda b, i: (b, i, 0)),
                   pl.BlockSpec((None, N_IDX_HEADS, tm), lambda b, i: (b, 0, i))],
        out_shape=[jax.ShapeDtypeStruct((B, S, KV_LATENT), BF16),
                   jax.ShapeDtypeStruct((B, S // KEY_BLOCK, KV_LATENT, KEY_BLOCK), BF16),
                   jax.ShapeDtypeStruct((B, S, 2 * LANES), BF16),
                   jax.ShapeDtypeStruct((B, N_IDX_HEADS, S), F32)],
        compiler_params=_cparams(("parallel", "arbitrary")),
        name="latent_proj",
    )(h, w_ckw, kv_norm.reshape(1, KV_LATENT), lng, lnb)


def _sortable_to_f32(key):
    bits = key ^ ((key >> 31) & jnp.int32(0x7FFFFFFF))
    return pltpu.bitcast(bits, F32)


def _fold8(a):
    return jnp.sum(a.reshape(a.shape[0] // 8, 8, a.shape[1]), axis=0)


def _attn_kernel(q_ref, qi_ref, wixt_ref, kix_ref, c_ref, ct_ref, wuk_ref, wuvt_ref, ga_ref,
                 o_ref, isc_ref, bias_ref, qlat_ref, acc_ref, m_ref, l_ref, r_ref, *, k_top):
    i = pl.program_id(1)
    t0 = i * Q_TILE
    nkb = (t0 + Q_TILE + KEY_BLOCK - 1) // KEY_BLOCK
    neg_inf = jnp.float32(-jnp.inf)

    t_idx = t0 + lax.broadcasted_iota(jnp.int32, (KEY_BLOCK, Q_TILE), 1)
    q_chunk = t_idx // CHUNK

    def admissible(kb):
        s_idx = kb * KEY_BLOCK + lax.broadcasted_iota(jnp.int32, (KEY_BLOCK, Q_TILE), 0)
        return (s_idx // CHUNK) <= q_chunk

    wix = wixt_ref[...] * (IDX_DIM ** -0.5)

    def idx_body(kb, carry):
        r0 = pl.multiple_of(kb * KEY_BLOCK, KEY_BLOCK)
        kx = kix_ref[pl.ds(r0, KEY_BLOCK), :]
        kx_even, kx_odd = kx[:, :LANES], kx[:, LANES:]
        acc = jnp.zeros((KEY_BLOCK, Q_TILE), F32)
        for g in range(N_IDX_HEADS // 4):
            rhs = jnp.concatenate([qi_ref[:, (2 * g) * LANES:(2 * g + 1) * LANES],
                                   qi_ref[:, (2 * g + 1) * LANES:(2 * g + 2) * LANES]], axis=0)
            for par, kxp in ((0, kx_even), (1, kx_odd)):
                d = lax.dot_general(kxp, rhs, (((1,), (1,)), ((), ())), preferred_element_type=F32)
                for half in range(2):
                    h = 4 * g + 2 * half + par
                    acc = acc + wix[h:h + 1, :] * jnp.maximum(d[:, half * Q_TILE:(half + 1) * Q_TILE], 0.0)
        isc_ref[pl.ds(r0, KEY_BLOCK), :] = jnp.where(admissible(kb), acc, neg_inf)
        return carry

    lax.fori_loop(0, nkb, idx_body, 0)

    def count(pred):
        def body(kb, cnt):
            r0 = pl.multiple_of(kb * KEY_BLOCK, KEY_BLOCK)
            v = isc_ref[pl.ds(r0, KEY_BLOCK), :]
            return cnt + _fold8(jnp.where(pred(v), 1, 0).astype(jnp.int32))
        c8 = lax.fori_loop(0, nkb, body, jnp.zeros((8, Q_TILE), jnp.int32))
        return jnp.sum(c8, axis=0, keepdims=True)

    int_min = jnp.int32(-2 ** 31)

    def search_body(step, key):
        bit = 31 - step
        inc = jnp.where(bit == 31, int_min, jnp.left_shift(jnp.int32(1), jnp.minimum(bit, 30)))
        cand = key + inc
        thr = _sortable_to_f32(cand)
        n_ge = count(lambda v: v >= thr)
        return jnp.where(n_ge >= k_top, cand, key)

    key = lax.fori_loop(0, 32, search_body, jnp.full((1, Q_TILE), int_min, jnp.int32))
    thr = _sortable_to_f32(key)
    n_gt = count(lambda v: v > thr)
    need = (k_top - n_gt).astype(F32)
    n_adm = (t_idx[0:1, :] // CHUNK + 1) * CHUNK
    thr = jnp.where(n_adm <= k_top, neg_inf, thr)

    tri = (lax.broadcasted_iota(jnp.int32, (KEY_BLOCK, KEY_BLOCK), 1)
           <= lax.broadcasted_iota(jnp.int32, (KEY_BLOCK, KEY_BLOCK), 0)).astype(BF16)

    def sel_body(kb, tie_carry):
        r0 = pl.multiple_of(kb * KEY_BLOCK, KEY_BLOCK)
        v = isc_ref[pl.ds(r0, KEY_BLOCK), :]
        tie = (v == thr).astype(BF16)
        rank = jnp.dot(tri, tie, preferred_element_type=F32) + tie_carry
        sel = ((v > thr) | ((v == thr) & (rank <= need))) & admissible(kb)
        bias_ref[pl.ds(r0, KEY_BLOCK), :] = jnp.where(sel, 0.0, neg_inf)
        return rank[KEY_BLOCK - 1:KEY_BLOCK, :]

    lax.fori_loop(0, nkb, sel_body, jnp.zeros((1, Q_TILE), F32))

    for h in range(N_HEADS):
        qh = q_ref[:, h * HEAD_DIM:(h + 1) * HEAD_DIM]
        ql = lax.dot_general(wuk_ref[h], qh, (((1,), (1,)), ((), ())), preferred_element_type=F32)
        qlat_ref[:, h * Q_TILE:(h + 1) * Q_TILE] = ql.astype(qlat_ref.dtype)

    attn_scale = HEAD_DIM ** -0.5
    gw = HEADS_PER_GROUP * Q_TILE
    n_groups = N_HEADS // HEADS_PER_GROUP
    acc_ref[...] = jnp.zeros_like(acc_ref)
    m_ref[...] = jnp.full_like(m_ref, M_FLOOR)
    l_ref[...] = jnp.zeros_like(l_ref)

    def raw_scores(kb, g):
        r0 = pl.multiple_of(kb * KEY_BLOCK, KEY_BLOCK)
        return jnp.dot(c_ref[pl.ds(r0, KEY_BLOCK), :], qlat_ref[:, g * gw:(g + 1) * gw],
                       preferred_element_type=F32)

    r_ref[...] = raw_scores(0, 0)

    def attn_body(kb, carry):
        r0 = pl.multiple_of(kb * KEY_BLOCK, KEY_BLOCK)
        ct_blk = ct_ref[kb]
        b = bias_ref[pl.ds(r0, KEY_BLOCK), :]
        b4 = jnp.concatenate([b] * HEADS_PER_GROUP, axis=1)
        kb_next = jnp.minimum(kb + 1, nkb - 1)

        def pv_update(g, alpha, p):
            acc_ref[g] = alpha * acc_ref[g] + jnp.dot(ct_blk, p, preferred_element_type=F32)

        r = r_ref[...]
        pending = None
        for g in range(n_groups):
            r_next = raw_scores(kb, g + 1) if g + 1 < n_groups else raw_scores(kb_next, 0)
            if pending is not None:
                pv_update(*pending)
            s = r * attn_scale + b4
            m_old = m_ref[g]
            m_blk = jnp.max(jnp.max(s.reshape(KEY_BLOCK // 8, 8, gw), axis=0), axis=0, keepdims=True)
            m_new = jnp.maximum(m_old, m_blk)
            alpha = jnp.exp(m_old - m_new)
            p = jnp.exp(s - m_new)
            m_ref[g] = m_new
            l_ref[g] = alpha * l_ref[g] + _fold8(p)
            pending = (g, alpha, p.astype(BF16))
            r = r_next
        r_ref[...] = r
        pv_update(*pending)
        return carry

    lax.fori_loop(0, nkb, attn_body, 0)

    for g in range(n_groups):
        inv_l = 1.0 / jnp.sum(l_ref[g], axis=0, keepdims=True)
        o_lat = (acc_ref[g] * inv_l).astype(BF16)
        for hh in range(HEADS_PER_GROUP):
            h = g * HEADS_PER_GROUP + hh
            out_t = jnp.dot(wuvt_ref[h], o_lat[:, hh * Q_TILE:(hh + 1) * Q_TILE],
                            preferred_element_type=F32)
            gate = ga_ref[:, h * HEAD_DIM:(h + 1) * HEAD_DIM]
            o_ref[:, h * HEAD_DIM:(h + 1) * HEAD_DIM] = (
                out_t.T * (gate * jax.nn.sigmoid(gate))).astype(o_ref.dtype)


def _sparse_attention(qqi, wixt, kix, c, ct, w_uk, w_uvt, gates, k_top):
    B, S, _ = qqi.shape
    DA = N_HEADS * HEAD_DIM
    DI = N_IDX_HEADS * IDX_DIM
    nq = S // Q_TILE
    gw = HEADS_PER_GROUP * Q_TILE
    return pl.pallas_call(
        functools.partial(_attn_kernel, k_top=k_top),
        grid=(B, nq),
        in_specs=[pl.BlockSpec((None, Q_TILE, DA), lambda b, i: (b, i, 0)),
                  pl.BlockSpec((None, Q_TILE, DI), lambda b, i: (b, i, DA // DI)),
                  pl.BlockSpec((None, N_IDX_HEADS, Q_TILE), lambda b, i: (b, 0, i)),
                  pl.BlockSpec((None, S, 2 * LANES), lambda b, i: (b, 0, 0)),
                  pl.BlockSpec((None, S, KV_LATENT), lambda b, i: (b, 0, 0)),
                  pl.BlockSpec((None,) + ct.shape[1:], lambda b, i: (b, 0, 0, 0)),
                  pl.BlockSpec(w_uk.shape, lambda b, i: (0, 0, 0)),
                  pl.BlockSpec(w_uvt.shape, lambda b, i: (0, 0, 0)),
                  pl.BlockSpec((Q_TILE, DA), lambda b, i: (b * nq + i, 0))],
        out_specs=pl.BlockSpec((Q_TILE, DA), lambda b, i: (b * nq + i, 0)),
        out_shape=jax.ShapeDtypeStruct((B * S, DA), BF16),
        scratch_shapes=[pltpu.VMEM((S, Q_TILE), F32),
                        pltpu.VMEM((S, Q_TILE), F32),
                        pltpu.VMEM((KV_LATENT, N_HEADS * Q_TILE), BF16),
                        pltpu.VMEM((N_HEADS // HEADS_PER_GROUP, KV_LATENT, gw), F32),
                        pltpu.VMEM((N_HEADS // HEADS_PER_GROUP, 1, gw), F32),
                        pltpu.VMEM((N_HEADS // HEADS_PER_GROUP, 8, gw), F32),
                        pltpu.VMEM((KEY_BLOCK, gw), F32)],
        compiler_params=_cparams(("parallel", "arbitrary")),
        name="sparse_attention",
    )(qqi, qqi, wixt, kix, c, ct, w_uk, w_uvt, gates)


POOL_HALO = 16


def _pool_kernel(pin_ref, halo_ref, gate_ref, wp_ref, ps_ref, o_ref):
    i = pl.program_id(1)
    ts = pin_ref.shape[0]
    t = i * ts + lax.broadcasted_iota(jnp.int32, (ts, POOL_GROUP), 0)
    for g, win in enumerate(POOL_WINDOWS):
        cols = slice(g * POOL_GROUP, (g + 1) * POOL_GROUP)
        cur = pin_ref[:, cols]
        halo = jnp.where(i == 0, 0.0, halo_ref[:, cols])
        ssum = jnp.concatenate([halo, cur], axis=0)
        span = 1
        while span < win:
            ssum = ssum + pltpu.roll(ssum, span, 0)
            span *= 2
        count = jnp.minimum(t + 1, win).astype(F32)
        mixed = ssum[POOL_HALO:, :] / count - cur
        y = jnp.dot(mixed.astype(BF16), wp_ref[g], preferred_element_type=F32)
        gate = gate_ref[:, cols]
        o_ref[:, cols] = (y * ps_ref[:, cols] * (gate * jax.nn.sigmoid(gate))).astype(o_ref.dtype)


def _pool_mixer(gates, w_pool, pool_scale, B, S, ts=256):
    T = gates.shape[0]
    D = gates.shape[1] // 3
    nS = S // ts
    hb = ts // POOL_HALO
    return pl.pallas_call(
        _pool_kernel,
        grid=(B, nS),
        in_specs=[pl.BlockSpec((ts, D), lambda b, i: (b * nS + i, 1)),
                  pl.BlockSpec((POOL_HALO, D), lambda b, i: (jnp.maximum((b * nS + i) * hb - 1, 0), 1)),
                  pl.BlockSpec((ts, D), lambda b, i: (b * nS + i, 2)),
                  pl.BlockSpec(w_pool.shape, lambda b, i: (0, 0, 0)),
                  pl.BlockSpec((1, D), lambda b, i: (0, 0))],
        out_specs=pl.BlockSpec((ts, D), lambda b, i: (b * nS + i, 0)),
        out_shape=jax.ShapeDtypeStruct((T, D), BF16),
        compiler_params=_cparams(("parallel", "arbitrary")),
        name="pool_mixer",
    )(gates, gates, gates, w_pool, pool_scale.reshape(1, D))


def _outproj_kernel(ya_ref, yb_ref, wa_ref, wb_ref, x_ref, g_ref, o_ref):
    j = pl.program_id(1)
    tn = wa_ref.shape[1]
    n_col = o_ref.shape[1] // tn
    y = (jnp.dot(ya_ref[...], wa_ref[...], preferred_element_type=F32)
         + jnp.dot(yb_ref[...], wb_ref[...], preferred_element_type=F32))
    for jj in range(n_col):
        @pl.when(j == jj)
        def _():
            o_ref[:, jj * tn:(jj + 1) * tn] = y

    @pl.when(j == n_col - 1)
    def _():
        ss = jnp.zeros((o_ref.shape[0], 1), F32)
        for jj in range(n_col):
            yj = o_ref[:, jj * tn:(jj + 1) * tn]
            ss = ss + jnp.sum(yj * yj, axis=-1, keepdims=True)
        rs = lax.rsqrt(ss * (1.0 / (n_col * tn)) + NORM_EPS)
        for jj in range(n_col):
            cols = slice(jj * tn, (jj + 1) * tn)
            o_ref[:, cols] = x_ref[:, cols] + o_ref[:, cols] * rs * g_ref[:, cols]


def _out_proj(ya, yb, wa, wb, x2, post_g, tm=512, tn=512):
    T, D = x2.shape
    DA = ya.shape[1]
    return pl.pallas_call(
        _outproj_kernel,
        grid=(T // tm, D // tn),
        in_specs=[pl.BlockSpec((tm, DA), lambda i, j: (i, 0)),
                  pl.BlockSpec((tm, yb.shape[1]), lambda i, j: (i, 0)),
                  pl.BlockSpec((DA, tn), lambda i, j: (0, j)),
                  pl.BlockSpec((yb.shape[1], tn), lambda i, j: (0, j)),
                  pl.BlockSpec((tm, D), lambda i, j: (i, 0)),
                  pl.BlockSpec((1, D), lambda i, j: (0, 0))],
        out_specs=pl.BlockSpec((tm, D), lambda i, j: (i, 0)),
        out_shape=jax.ShapeDtypeStruct((T, D), F32),
        compiler_params=_cparams(("parallel", "arbitrary")),
        name="out_proj",
    )(ya, yb, wa, wb, x2, post_g.reshape(1, D))


def _layer(x2, B, S, pre_g, w_in, kv_g, w_uk, w_uv, ln_g, ln_b, w_pool, pool_scale, w_out, post_g, k_top):
    d_attn = N_HEADS * HEAD_DIM
    n_qi = N_IDX_HEADS * IDX_DIM
    o_c = d_attn
    o_qi = o_c + KV_LATENT
    o_k = o_qi + n_qi
    o_w = o_k + IDX_DIM
    o_g = o_w + N_IDX_HEADS
    w_qqi = jnp.concatenate([w_in[:, :o_c], w_in[:, o_qi:o_k]], axis=1).astype(BF16)
    w_ckw = jnp.concatenate([w_in[:, o_c:o_qi], w_in[:, o_k:o_g],
                             jnp.zeros((w_in.shape[0], LANES - IDX_DIM - N_IDX_HEADS), F32)], axis=1).astype(BF16)
    w_gates = w_in[:, o_g:].astype(BF16)
    w_uk_b = w_uk.astype(BF16)
    w_uvt_b = jnp.swapaxes(w_uv, 1, 2).astype(BF16)
    w_pool_b = w_pool.astype(BF16)
    wa = w_out[:d_attn].astype(BF16)
    wb = w_out[d_attn:].astype(BF16)

    h = _prenorm(x2, pre_g)
    qqi = _matmul(h, w_qqi, BF16, 1024, 512, "proj_q_qi")
    gates = _matmul(h, w_gates, F32, 1024, 512, "proj_gates")
    c, ct, kix, wixt = _latent_proj(h, w_ckw, kv_g, ln_g, ln_b, B, S)
    ya = _sparse_attention(qqi.reshape(B, S, d_attn + n_qi), wixt, kix, c, ct, w_uk_b, w_uvt_b, gates, k_top)
    yb = _pool_mixer(gates, w_pool_b, pool_scale, B, S)
    return _out_proj(ya, yb, wa, wb, x2, post_g)


def kernel(x, pre_norm, w_in, kv_norm, w_uk, w_uv, idx_k_norm_g, idx_k_norm_b, w_pool, pool_scale, w_out, post_norm):
    B, S, D = x.shape
    k_top = min(TOPK_MAX, S // 4)
    x2 = x.reshape(B * S, D)
    for l in range(pre_norm.shape[0]):
        x2 = _layer(x2, B, S, pre_norm[l], w_in[l], kv_norm[l], w_uk[l], w_uv[l], idx_k_norm_g[l],
                    idx_k_norm_b[l], w_pool[l], pool_scale[l], w_out[l], post_norm[l], k_top)
    return x2.reshape(B, S, D)
```

```python
import functools
import math

import jax
import jax.numpy as jnp
from jax import lax
from jax.experimental import pallas as pl
from jax.experimental.pallas import tpu as pltpu

F32 = jnp.float32
BF16 = jnp.bfloat16

NORM_EPS = 1e-6
CHUNK = 64
N_HEADS = 16
HEAD_DIM = 128
KV_LATENT = 512
N_IDX_HEADS = 16
IDX_DIM = 64
TOPK_MAX = 256
POOL_WINDOWS = (2, 4, 8, 16)
POOL_GROUP = 512

LANES = 128
Q_TILE = 256
KEY_BLOCK = 256
HEADS_PER_GROUP = 2
M_FLOOR = -1e30
VMEM_LIMIT = 56 * 1024 * 1024


def _cparams(sem):
    return pltpu.CompilerParams(dimension_semantics=sem, vmem_limit_bytes=VMEM_LIMIT)


def _prenorm_kernel(x_ref, g_ref, o_ref):
    x = x_ref[...]
    ms = jnp.mean(x * x, axis=-1, keepdims=True)
    o_ref[...] = (x * lax.rsqrt(ms + NORM_EPS) * g_ref[...]).astype(o_ref.dtype)


def _prenorm(x2, g, tm=256):
    T, D = x2.shape
    return pl.pallas_call(
        _prenorm_kernel,
        grid=(T // tm,),
        in_specs=[pl.BlockSpec((tm, D), lambda i: (i, 0)),
                  pl.BlockSpec((1, D), lambda i: (0, 0))],
        out_specs=pl.BlockSpec((tm, D), lambda i: (i, 0)),
        out_shape=jax.ShapeDtypeStruct((T, D), BF16),
        compiler_params=_cparams(("parallel",)),
        name="prenorm",
    )(x2, g.reshape(1, D))


def _matmul_kernel(h_ref, w_ref, o_ref):
    o_ref[...] = jnp.dot(h_ref[...], w_ref[...], preferred_element_type=F32).astype(o_ref.dtype)


def _matmul(h, w, out_dtype, tm, tn, name):
    T, K = h.shape
    N = w.shape[1]
    tm = min(tm, T)
    return pl.pallas_call(
        _matmul_kernel,
        grid=(T // tm, N // tn),
        in_specs=[pl.BlockSpec((tm, K), lambda i, j: (i, 0)),
                  pl.BlockSpec((K, tn), lambda i, j: (0, j))],
        out_specs=pl.BlockSpec((tm, tn), lambda i, j: (i, j)),
        out_shape=jax.ShapeDtypeStruct((T, N), out_dtype),
        compiler_params=_cparams(("parallel", "arbitrary")),
        name=name,
    )(h, w)


def _latent_kernel(h_ref, w_ref, kvg_ref, lng_ref, lnb_ref, c_ref, ct_ref, kix_ref, wixt_ref):
    z = jnp.dot(h_ref[...], w_ref[...], preferred_element_type=F32)
    c = z[:, :KV_LATENT]
    cn = c * lax.rsqrt(jnp.mean(c * c, axis=-1, keepdims=True) + NORM_EPS) * kvg_ref[...]
    c_ref[...] = cn.astype(c_ref.dtype)
    for u in range(ct_ref.shape[0]):
        ct_ref[u] = cn[u * KEY_BLOCK:(u + 1) * KEY_BLOCK, :].T.astype(ct_ref.dtype)
    kw = z[:, KV_LATENT:]
    lane = lax.broadcasted_iota(jnp.int32, kw.shape, 1)
    is_k = lane < IDX_DIM
    mu = jnp.sum(jnp.where(is_k, kw, 0.0), axis=-1, keepdims=True) * (1.0 / IDX_DIM)
    d = jnp.where(is_k, kw - mu, 0.0)
    var = jnp.sum(d * d, axis=-1, keepdims=True) * (1.0 / IDX_DIM)
    kn = d * lax.rsqrt(var + NORM_EPS) * lng_ref[...] + lnb_ref[...]
    kix_ref[...] = jnp.concatenate([kn, pltpu.roll(kn, IDX_DIM, 1)], axis=1).astype(kix_ref.dtype)
    wixt_ref[...] = kw.T[IDX_DIM:IDX_DIM + N_IDX_HEADS, :] * (N_IDX_HEADS ** -0.5)


def _latent_proj(h, w_ckw, kv_norm, ln_g, ln_b, B, S, tm=512):
    T, K = h.shape
    nS = S // tm
    pad = jnp.zeros((LANES - IDX_DIM,), F32)
    lng = jnp.concatenate([ln_g, pad]).reshape(1, LANES)
    lnb = jnp.concatenate([ln_b, pad]).reshape(1, LANES)
    return pl.pallas_call(
        _latent_kernel,
        grid=(B, nS),
        in_specs=[pl.BlockSpec((tm, K), lambda b, i: (b * nS + i, 0)),
                  pl.BlockSpec(w_ckw.shape, lambda b, i: (0, 0)),
                  pl.BlockSpec((1, KV_LATENT), lambda b, i: (0, 0)),
                  pl.BlockSpec((1, LANES), lambda b, i: (0, 0)),
                  pl.BlockSpec((1, LANES), lambda b, i: (0, 0))],
        out_specs=[pl.BlockSpec((None, tm, KV_LATENT), lambda b, i: (b, i, 0)),
                   pl.BlockSpec((None, tm // KEY_BLOCK, KV_LATENT, KEY_BLOCK), lambda b, i: (b, i, 0, 0)),
                   pl.BlockSpec((None, tm, 2 * LANES), lambda b, i: (b, i, 0)),
                   pl.BlockSpec((None, N_IDX_HEADS, tm), lambda b, i: (b, 0, i))],
        out_shape=[jax.ShapeDtypeStruct((B, S, KV_LATENT), BF16),
                   jax.ShapeDtypeStruct((B, S // KEY_BLOCK, KV_LATENT, KEY_BLOCK), BF16),
                   jax.ShapeDtypeStruct((B, S, 2 * LANES), BF16),
                   jax.ShapeDtypeStruct((B, N_IDX_HEADS, S), F32)],
        compiler_params=_cparams(("parallel", "arbitrary")),
        name="latent_proj",
    )(h, w_ckw, kv_norm.reshape(1, KV_LATENT), lng, lnb)


def _sortable_to_f32(key):
    bits = key ^ ((key >> 31) & jnp.int32(0x7FFFFFFF))
    return pltpu.bitcast(bits, F32)


def _fold8(a):
    return jnp.sum(a.reshape(a.shape[0] // 8, 8, a.shape[1]), axis=0)


def _attn_kernel(q_ref, qi_ref, wixt_ref, kix_ref, c_ref, ct_ref, wuk_ref, wuvt_ref, ga_ref,
                 o_ref, isc_ref, bias_ref, qlat_ref, acc_ref, m_ref, l_ref, r_ref, p_ref, *, k_top):
    i = pl.program_id(1)
    t0 = i * Q_TILE
    nkb = (t0 + Q_TILE + KEY_BLOCK - 1) // KEY_BLOCK
    neg_inf = jnp.float32(-jnp.inf)

    t_idx = t0 + lax.broadcasted_iota(jnp.int32, (KEY_BLOCK, Q_TILE), 1)
    q_chunk = t_idx // CHUNK

    def admissible(kb):
        s_idx = kb * KEY_BLOCK + lax.broadcasted_iota(jnp.int32, (KEY_BLOCK, Q_TILE), 0)
        return (s_idx // CHUNK) <= q_chunk

    wix = wixt_ref[...] * (IDX_DIM ** -0.5)

    def idx_body(kb, carry):
        r0 = pl.multiple_of(kb * KEY_BLOCK, KEY_BLOCK)
        kx = kix_ref[pl.ds(r0, KEY_BLOCK), :]
        kx_even, kx_odd = kx[:, :LANES], kx[:, LANES:]
        acc = jnp.zeros((KEY_BLOCK, Q_TILE), F32)
        for pair in range(N_IDX_HEADS // 2):
            rhs = qi_ref[:, pair * LANES:(pair + 1) * LANES]
            for par, kxp in ((0, kx_even), (1, kx_odd)):
                h = 2 * pair + par
                d = lax.dot_general(kxp, rhs, (((1,), (1,)), ((), ())), preferred_element_type=F32)
                acc = acc + wix[h:h + 1, :] * jnp.maximum(d, 0.0)
        isc_ref[pl.ds(r0, KEY_BLOCK), :] = jnp.where(admissible(kb), acc, neg_inf)
        return carry

    lax.fori_loop(0, nkb, idx_body, 0)

    def count(pred):
        def body(kb, cnt):
            r0 = pl.multiple_of(kb * KEY_BLOCK, KEY_BLOCK)
            v = isc_ref[pl.ds(r0, KEY_BLOCK), :]
            return cnt + _fold8(jnp.where(pred(v), 1, 0).astype(jnp.int32))
        c8 = lax.fori_loop(0, nkb, body, jnp.zeros((8, Q_TILE), jnp.int32))
        return jnp.sum(c8, axis=0, keepdims=True)

    int_min = jnp.int32(-2 ** 31)

    def search_body(step, key):
        bit = 31 - step
        inc = jnp.where(bit == 31, int_min, jnp.left_shift(jnp.int32(1), jnp.minimum(bit, 30)))
        cand = key + inc
        thr = _sortable_to_f32(cand)
        n_ge = count(lambda v: v >= thr)
        return jnp.where(n_ge >= k_top, cand, key)

    key = lax.fori_loop(0, 32, search_body, jnp.full((1, Q_TILE), int_min, jnp.int32))
    thr = _sortable_to_f32(key)
    n_gt = count(lambda v: v > thr)
    need = (k_top - n_gt).astype(F32)
    n_adm = (t_idx[0:1, :] // CHUNK + 1) * CHUNK
    thr = jnp.where(n_adm <= k_top, neg_inf, thr)

    tri = (lax.broadcasted_iota(jnp.int32, (KEY_BLOCK, KEY_BLOCK), 1)
           <= lax.broadcasted_iota(jnp.int32, (KEY_BLOCK, KEY_BLOCK), 0)).astype(BF16)

    def sel_body(kb, tie_carry):
        r0 = pl.multiple_of(kb * KEY_BLOCK, KEY_BLOCK)
        v = isc_ref[pl.ds(r0, KEY_BLOCK), :]
        tie = (v == thr).astype(BF16)
        rank = jnp.dot(tri, tie, preferred_element_type=F32) + tie_carry
        sel = ((v > thr) | ((v == thr) & (rank <= need))) & admissible(kb)
        bias_ref[pl.ds(r0, KEY_BLOCK), :] = jnp.where(sel, 0.0, neg_inf)
        return rank[KEY_BLOCK - 1:KEY_BLOCK, :]

    lax.fori_loop(0, nkb, sel_body, jnp.zeros((1, Q_TILE), F32))

    for h in range(N_HEADS):
        qh = q_ref[:, h * HEAD_DIM:(h + 1) * HEAD_DIM]
        ql = lax.dot_general(wuk_ref[h], qh, (((1,), (1,)), ((), ())), preferred_element_type=F32)
        qlat_ref[:, h * Q_TILE:(h + 1) * Q_TILE] = ql.astype(qlat_ref.dtype)

    attn_scale = HEAD_DIM ** -0.5
    gw = HEADS_PER_GROUP * Q_TILE
    n_groups = N_HEADS // HEADS_PER_GROUP
    acc_ref[...] = jnp.zeros_like(acc_ref)
    m_ref[...] = jnp.full_like(m_ref, M_FLOOR)
    l_ref[...] = jnp.zeros_like(l_ref)

    def raw_scores(kb, g):
        r0 = pl.multiple_of(kb * KEY_BLOCK, KEY_BLOCK)
        return jnp.dot(c_ref[pl.ds(r0, KEY_BLOCK), :], qlat_ref[:, g * gw:(g + 1) * gw],
                       preferred_element_type=F32)

    r_ref[0] = raw_scores(0, 0)

    def attn_body(kb, carry):
        r0 = pl.multiple_of(kb * KEY_BLOCK, KEY_BLOCK)
        ct_blk = ct_ref[kb]
        b = bias_ref[pl.ds(r0, KEY_BLOCK), :]
        b_g = jnp.concatenate([b] * HEADS_PER_GROUP, axis=1)
        kb_next = jnp.minimum(kb + 1, nkb - 1)

        def pv_update(g, alpha):
            acc_ref[g] = alpha * acc_ref[g] + jnp.dot(ct_blk, p_ref[g % 2], preferred_element_type=F32)

        pending = None
        for g in range(n_groups):
            slot = g % 2
            r_ref[1 - slot] = raw_scores(kb, g + 1) if g + 1 < n_groups else raw_scores(kb_next, 0)
            if pending is not None:
                pv_update(*pending)
            s = r_ref[slot] * attn_scale + b_g
            r_ref[slot] = s
            m_old = m_ref[g]
            m_blk = jnp.max(jnp.max(s.reshape(KEY_BLOCK // 8, 8, gw), axis=0), axis=0, keepdims=True)
            m_new = jnp.maximum(m_old, m_blk)
            alpha = jnp.exp(m_old - m_new)
            p = jnp.exp(r_ref[slot] - m_new)
            p_ref[slot] = p.astype(BF16)
            m_ref[g] = m_new
            l_ref[g] = alpha * l_ref[g] + _fold8(p)
            pending = (g, alpha)
        pv_update(*pending)
        return carry

    lax.fori_loop(0, nkb, attn_body, 0)

    for g in range(n_groups):
        inv_l = 1.0 / jnp.sum(l_ref[g], axis=0, keepdims=True)
        o_lat = (acc_ref[g] * inv_l).astype(BF16)
        for hh in range(HEADS_PER_GROUP):
            h = g * HEADS_PER_GROUP + hh
            out_t = jnp.dot(wuvt_ref[h], o_lat[:, hh * Q_TILE:(hh + 1) * Q_TILE],
                            preferred_element_type=F32)
            gate = ga_ref[:, h * HEAD_DIM:(h + 1) * HEAD_DIM]
            o_ref[:, h * HEAD_DIM:(h + 1) * HEAD_DIM] = (
                out_t.T * (gate * jax.nn.sigmoid(gate))).astype(o_ref.dtype)


def _sparse_attention(qqi, wixt, kix, c, ct, w_uk, w_uvt, gates, k_top):
    B, S, _ = qqi.shape
    DA = N_HEADS * HEAD_DIM
    DI = N_IDX_HEADS * IDX_DIM
    nq = S // Q_TILE
    gw = HEADS_PER_GROUP * Q_TILE
    return pl.pallas_call(
        functools.partial(_attn_kernel, k_top=k_top),
        grid=(B, nq),
        in_specs=[pl.BlockSpec((None, Q_TILE, DA), lambda b, i: (b, i, 0)),
                  pl.BlockSpec((None, Q_TILE, DI), lambda b, i: (b, i, DA // DI)),
                  pl.BlockSpec((None, N_IDX_HEADS, Q_TILE), lambda b, i: (b, 0, i)),
                  pl.BlockSpec((None, S, 2 * LANES), lambda b, i: (b, 0, 0)),
                  pl.BlockSpec((None, S, KV_LATENT), lambda b, i: (b, 0, 0)),
                  pl.BlockSpec((None,) + ct.shape[1:], lambda b, i: (b, 0, 0, 0)),
                  pl.BlockSpec(w_uk.shape, lambda b, i: (0, 0, 0)),
                  pl.BlockSpec(w_uvt.shape, lambda b, i: (0, 0, 0)),
                  pl.BlockSpec((Q_TILE, DA), lambda b, i: (b * nq + i, 0))],
        out_specs=pl.BlockSpec((Q_TILE, DA), lambda b, i: (b * nq + i, 0)),
        out_shape=jax.ShapeDtypeStruct((B * S, DA), BF16),
        scratch_shapes=[pltpu.VMEM((S, Q_TILE), F32),
                        pltpu.VMEM((S, Q_TILE), F32),
                        pltpu.VMEM((KV_LATENT, N_HEADS * Q_TILE), BF16),
                        pltpu.VMEM((N_HEADS // HEADS_PER_GROUP, KV_LATENT, gw), F32),
                        pltpu.VMEM((N_HEADS // HEADS_PER_GROUP, 1, gw), F32),
                        pltpu.VMEM((N_HEADS // HEADS_PER_GROUP, 8, gw), F32),
                        pltpu.VMEM((2, KEY_BLOCK, gw), F32),
                        pltpu.VMEM((2, KEY_BLOCK, gw), BF16)],
        compiler_params=_cparams(("parallel", "arbitrary")),
        name="sparse_attention",
    )(qqi, qqi, wixt, kix, c, ct, w_uk, w_uvt, gates)


POOL_HALO = 16


def _pool_kernel(pin_ref, halo_ref, gate_ref, wp_ref, ps_ref, o_ref):
    i = pl.program_id(1)
    ts = pin_ref.shape[0]
    t = i * ts + lax.broadcasted_iota(jnp.int32, (ts, POOL_GROUP), 0)
    for g, win in enumerate(POOL_WINDOWS):
        cols = slice(g * POOL_GROUP, (g + 1) * POOL_GROUP)
        cur = pin_ref[:, cols]
        halo = jnp.where(i == 0, 0.0, halo_ref[:, cols])
        ssum = jnp.concatenate([halo, cur], axis=0)
        span = 1
        while span < win:
            ssum = ssum + pltpu.roll(ssum, span, 0)
            span *= 2
        count = jnp.minimum(t + 1, win).astype(F32)
        mixed = ssum[POOL_HALO:, :] / count - cur
        y = jnp.dot(mixed.astype(BF16), wp_ref[g], preferred_element_type=F32)
        gate = gate_ref[:, cols]
        o_ref[:, cols] = (y * ps_ref[:, cols] * (gate * jax.nn.sigmoid(gate))).astype(o_ref.dtype)


def _pool_mixer(gates, w_pool, pool_scale, B, S, ts=256):
    T = gates.shape[0]
    D = gates.shape[1] // 3
    nS = S // ts
    hb = ts // POOL_HALO
    return pl.pallas_call(
        _pool_kernel,
        grid=(B, nS),
        in_specs=[pl.BlockSpec((ts, D), lambda b, i: (b * nS + i, 1)),
                  pl.BlockSpec((POOL_HALO, D), lambda b, i: (jnp.maximum((b * nS + i) * hb - 1, 0), 1)),
                  pl.BlockSpec((ts, D), lambda b, i: (b * nS + i, 2)),
                  pl.BlockSpec(w_pool.shape, lambda b, i: (0, 0, 0)),
                  pl.BlockSpec((1, D), lambda b, i: (0, 0))],
        out_specs=pl.BlockSpec((ts, D), lambda b, i: (b * nS + i, 0)),
        out_shape=jax.ShapeDtypeStruct((T, D), BF16),
        compiler_params=_cparams(("parallel", "arbitrary")),
        name="pool_mixer",
    )(gates, gates, gates, w_pool, pool_scale.reshape(1, D))


OUT_COL_TILE = 512
OUT_PROJ_VMEM_LIMIT = 60 * 1024 * 1024


def _outproj_kernel(ya_ref, yb_ref, w_ref, x_ref, g_ref, o_ref):
    da = ya_ref.shape[1]
    tm, d = o_ref.shape
    n_col = d // OUT_COL_TILE
    ss = jnp.zeros((tm, LANES), F32)
    for jj in range(n_col):
        cols = slice(jj * OUT_COL_TILE, (jj + 1) * OUT_COL_TILE)
        y = (jnp.dot(ya_ref[...], w_ref[:da, cols], preferred_element_type=F32)
             + jnp.dot(yb_ref[...], w_ref[da:, cols], preferred_element_type=F32))
        o_ref[:, cols] = y
        for k in range(OUT_COL_TILE // LANES):
            yk = y[:, k * LANES:(k + 1) * LANES]
            ss = ss + yk * yk
    rs = lax.rsqrt(jnp.sum(ss, axis=-1, keepdims=True) * (1.0 / d) + NORM_EPS)
    for jj in range(n_col):
        cols = slice(jj * OUT_COL_TILE, (jj + 1) * OUT_COL_TILE)
        o_ref[:, cols] = x_ref[:, cols] + o_ref[:, cols] * rs * g_ref[:, cols]


def _out_proj(ya, yb, w_out_b, x2, post_g, tm=256):
    T, D = x2.shape
    DA = ya.shape[1]
    DB = yb.shape[1]
    return pl.pallas_call(
        _outproj_kernel,
        grid=(T // tm,),
        in_specs=[pl.BlockSpec((tm, DA), lambda i: (i, 0)),
                  pl.BlockSpec((tm, DB), lambda i: (i, 0)),
                  pl.BlockSpec((DA + DB, D), lambda i: (0, 0), pipeline_mode=pl.Buffered(1)),
                  pl.BlockSpec((tm, D), lambda i: (i, 0)),
                  pl.BlockSpec((1, D), lambda i: (0, 0))],
        out_specs=pl.BlockSpec((tm, D), lambda i: (i, 0)),
        out_shape=jax.ShapeDtypeStruct((T, D), F32),
        compiler_params=pltpu.CompilerParams(dimension_semantics=("parallel",),
                                             vmem_limit_bytes=OUT_PROJ_VMEM_LIMIT),
        name="out_proj",
    )(ya, yb, w_out_b, x2, post_g.reshape(1, D))


def _prep_kernel(w_ref, qqi_ref, ckw_ref, g_ref, *, o_c, o_qi, o_k, o_g):
    d_attn, n_qi = o_c, o_k - o_qi
    qqi_ref[:, :d_attn] = w_ref[:, :o_c].astype(qqi_ref.dtype)
    qqi_ref[:, d_attn:] = w_ref[:, o_qi:o_k].astype(qqi_ref.dtype)
    ckw_ref[:, :KV_LATENT] = w_ref[:, o_c:o_qi].astype(ckw_ref.dtype)
    kw = w_ref[:, o_k:o_k + LANES]
    lane = lax.broadcasted_iota(jnp.int32, kw.shape, 1)
    ckw_ref[:, KV_LATENT:] = jnp.where(lane < o_g - o_k, kw, 0.0).astype(ckw_ref.dtype)
    n_g = g_ref.shape[1]
    win = w_ref[:, o_k:o_k + n_g + LANES]
    g_ref[...] = pltpu.roll(win, n_g + LANES - (o_g - o_k), 1)[:, :n_g].astype(g_ref.dtype)


def _prep_w_in(w_in, o_c, o_qi, o_k, o_g, tr=256):
    K, n_in = w_in.shape
    n_g = n_in - o_g
    n_pad = o_k + n_g + LANES
    return pl.pallas_call(
        functools.partial(_prep_kernel, o_c=o_c, o_qi=o_qi, o_k=o_k, o_g=o_g),
        grid=(K // tr,),
        in_specs=[pl.BlockSpec((tr, n_pad), lambda i: (i, 0))],
        out_specs=[pl.BlockSpec((tr, o_c + o_k - o_qi), lambda i: (i, 0)),
                   pl.BlockSpec((tr, KV_LATENT + LANES), lambda i: (i, 0)),
                   pl.BlockSpec((tr, n_g), lambda i: (i, 0))],
        out_shape=[jax.ShapeDtypeStruct((K, o_c + o_k - o_qi), BF16),
                   jax.ShapeDtypeStruct((K, KV_LATENT + LANES), BF16),
                   jax.ShapeDtypeStruct((K, n_g), BF16)],
        compiler_params=_cparams(("parallel",)),
        name="prep_w_in",
    )(w_in)


def _layer(x2, B, S, pre_g, w_in, kv_g, w_uk, w_uv, ln_g, ln_b, w_pool, pool_scale, w_out, post_g, k_top):
    d_attn = N_HEADS * HEAD_DIM
    n_qi = N_IDX_HEADS * IDX_DIM
    o_c = d_attn
    o_qi = o_c + KV_LATENT
    o_k = o_qi + n_qi
    o_w = o_k + IDX_DIM
    o_g = o_w + N_IDX_HEADS
    w_qqi, w_ckw, w_gates = _prep_w_in(w_in, o_c, o_qi, o_k, o_g)
    w_uk_b = w_uk.astype(BF16)
    w_uvt_b = jnp.swapaxes(w_uv, 1, 2).astype(BF16)
    w_pool_b = w_pool.astype(BF16)
    w_out_b = w_out.astype(BF16)

    h = _prenorm(x2, pre_g)
    qqi = _matmul(h, w_qqi, BF16, 1024, 512, "proj_q_qi")
    gates = _matmul(h, w_gates, F32, 1024, 512, "proj_gates")
    c, ct, kix, wixt = _latent_proj(h, w_ckw, kv_g, ln_g, ln_b, B, S)
    ya = _sparse_attention(qqi.reshape(B, S, d_attn + n_qi), wixt, kix, c, ct, w_uk_b, w_uvt_b, gates, k_top)
    yb = _pool_mixer(gates, w_pool_b, pool_scale, B, S)
    return _out_proj(ya, yb, w_out_b, x2, post_g)


def kernel(x, pre_norm, w_in, kv_norm, w_uk, w_uv, idx_k_norm_g, idx_k_norm_b, w_pool, pool_scale, w_out, post_norm):
    B, S, D = x.shape
    k_top = min(TOPK_MAX, S // 4)
    x2 = x.reshape(B * S, D)
    for l in range(pre_norm.shape[0]):
        x2 = _layer(x2, B, S, pre_norm[l], w_in[l], kv_norm[l], w_uk[l], w_uv[l], idx_k_norm_g[l],
                    idx_k_norm_b[l], w_pool[l], pool_scale[l], w_out[l], post_norm[l], k_top)
    return x2.reshape(B, S, D)
```

```python
import functools
import math

import jax
import jax.numpy as jnp
from jax import lax
from jax.experimental import pallas as pl
from jax.experimental.pallas import tpu as pltpu

F32 = jnp.float32
BF16 = jnp.bfloat16

NORM_EPS = 1e-6
CHUNK = 64
N_HEADS = 16
HEAD_DIM = 128
KV_LATENT = 512
N_IDX_HEADS = 16
IDX_DIM = 64
TOPK_MAX = 256
POOL_WINDOWS = (2, 4, 8, 16)
POOL_GROUP = 512

LANES = 128
Q_TILE = 256
KEY_BLOCK = 256
HEADS_PER_GROUP = 2
M_FLOOR = -1e30
VMEM_LIMIT = 56 * 1024 * 1024


def _cparams(sem):
    return pltpu.CompilerParams(dimension_semantics=sem, vmem_limit_bytes=VMEM_LIMIT)


def _prenorm_kernel(x_ref, g_ref, o_ref):
    x = x_ref[...]
    ms = jnp.mean(x * x, axis=-1, keepdims=True)
    o_ref[...] = (x * lax.rsqrt(ms + NORM_EPS) * g_ref[...]).astype(o_ref.dtype)


def _prenorm(x2, g, tm=256):
    T, D = x2.shape
    return pl.pallas_call(
        _prenorm_kernel,
        grid=(T // tm,),
        in_specs=[pl.BlockSpec((tm, D), lambda i: (i, 0)),
                  pl.BlockSpec((1, D), lambda i: (0, 0))],
        out_specs=pl.BlockSpec((tm, D), lambda i: (i, 0)),
        out_shape=jax.ShapeDtypeStruct((T, D), BF16),
        compiler_params=_cparams(("parallel",)),
        name="prenorm",
    )(x2, g.reshape(1, D))


_NT = (((1,), (1,)), ((), ()))


def _matmul_nt_kernel(h_ref, wt_ref, o_ref):
    o_ref[...] = lax.dot_general(h_ref[...], wt_ref[...].astype(BF16), _NT,
                                 preferred_element_type=F32).astype(o_ref.dtype)


def _matmul_nt(h, wt, w_spec, n_out, out_dtype, tm, tn, name):
    T, K = h.shape
    tm = min(tm, T)
    return pl.pallas_call(
        _matmul_nt_kernel,
        grid=(T // tm, n_out // tn),
        in_specs=[pl.BlockSpec((tm, K), lambda i, j: (i, 0)), w_spec],
        out_specs=pl.BlockSpec((tm, tn), lambda i, j: (i, j)),
        out_shape=jax.ShapeDtypeStruct((T, n_out), out_dtype),
        compiler_params=_cparams(("parallel", "arbitrary")),
        name=name,
    )(h, wt)


def _latent_kernel(h_ref, wc_ref, wkw_ref, kvg_ref, lng_ref, lnb_ref, c_ref, ct_ref, kix_ref, wixt_ref):
    h = h_ref[...]
    c = lax.dot_general(h, wc_ref[...].astype(BF16), _NT, preferred_element_type=F32)
    cn = c * lax.rsqrt(jnp.mean(c * c, axis=-1, keepdims=True) + NORM_EPS) * kvg_ref[...]
    c_ref[...] = cn.astype(c_ref.dtype)
    for u in range(ct_ref.shape[0]):
        ct_ref[u] = cn[u * KEY_BLOCK:(u + 1) * KEY_BLOCK, :].T.astype(ct_ref.dtype)
    kw = lax.dot_general(h, wkw_ref[...].astype(BF16), _NT, preferred_element_type=F32)
    lane = lax.broadcasted_iota(jnp.int32, kw.shape, 1)
    is_k = lane < IDX_DIM
    mu = jnp.sum(jnp.where(is_k, kw, 0.0), axis=-1, keepdims=True) * (1.0 / IDX_DIM)
    d = jnp.where(is_k, kw - mu, 0.0)
    var = jnp.sum(d * d, axis=-1, keepdims=True) * (1.0 / IDX_DIM)
    kn = d * lax.rsqrt(var + NORM_EPS) * lng_ref[...] + lnb_ref[...]
    kix_ref[...] = jnp.concatenate([kn, pltpu.roll(kn, IDX_DIM, 1)], axis=1).astype(kix_ref.dtype)
    wixt_ref[...] = kw.T[IDX_DIM:IDX_DIM + N_IDX_HEADS, :] * (N_IDX_HEADS ** -0.5)


def _latent_proj(h, wt, o_c, o_k, kv_norm, ln_g, ln_b, B, S, tm=512):
    T, K = h.shape
    nS = S // tm
    pad = jnp.zeros((LANES - IDX_DIM,), F32)
    lng = jnp.concatenate([ln_g, pad]).reshape(1, LANES)
    lnb = jnp.concatenate([ln_b, pad]).reshape(1, LANES)
    assert o_c % KV_LATENT == 0 and o_k % LANES == 0
    return pl.pallas_call(
        _latent_kernel,
        grid=(B, nS),
        in_specs=[pl.BlockSpec((tm, K), lambda b, i: (b * nS + i, 0)),
                  pl.BlockSpec((KV_LATENT, K), lambda b, i: (o_c // KV_LATENT, 0)),
                  pl.BlockSpec((LANES, K), lambda b, i: (o_k // LANES, 0)),
                  pl.BlockSpec((1, KV_LATENT), lambda b, i: (0, 0)),
                  pl.BlockSpec((1, LANES), lambda b, i: (0, 0)),
                  pl.BlockSpec((1, LANES), lambda b, i: (0, 0))],
        out_specs=[pl.BlockSpec((None, tm, KV_LATENT), lambda b, i: (b, i, 0)),
                   pl.BlockSpec((None, tm // KEY_BLOCK, KV_LATENT, KEY_BLOCK), lambda b, i: (b, i, 0, 0)),
                   pl.BlockSpec((None, tm, 2 * LANES), lambda b, i: (b, i, 0)),
                   pl.BlockSpec((None, N_IDX_HEADS, tm), lambda b, i: (b, 0, i))],
        out_shape=[jax.ShapeDtypeStruct((B, S, KV_LATENT), BF16),
                   jax.ShapeDtypeStruct((B, S // KEY_BLOCK, KV_LATENT, KEY_BLOCK), BF16),
                   jax.ShapeDtypeStruct((B, S, 2 * LANES), BF16),
                   jax.ShapeDtypeStruct((B, N_IDX_HEADS, S), F32)],
        compiler_params=_cparams(("parallel", "arbitrary")),
        name="latent_proj",
    )(h, wt, wt, kv_norm.reshape(1, KV_LATENT), lng, lnb)


def _sortable_to_f32(key):
    bits = key ^ ((key >> 31) & jnp.int32(0x7FFFFFFF))
    return pltpu.bitcast(bits, F32)


def _fold8(a):
    return jnp.sum(a.reshape(a.shape[0] // 8, 8, a.shape[1]), axis=0)


def _attn_kernel(q_ref, qi_ref, wixt_ref, kix_ref, c_ref, ct_ref, wuk_ref, wuvt_ref, ga_ref,
                 o_ref, isc_ref, bias_ref, qlat_ref, acc_ref, m_ref, l_ref, r_ref, p_ref, *, k_top):
    i = pl.program_id(1)
    t0 = i * Q_TILE
    nkb = (t0 + Q_TILE + KEY_BLOCK - 1) // KEY_BLOCK
    neg_inf = jnp.float32(-jnp.inf)

    t_idx = t0 + lax.broadcasted_iota(jnp.int32, (KEY_BLOCK, Q_TILE), 1)
    q_chunk = t_idx // CHUNK

    def admissible(kb):
        s_idx = kb * KEY_BLOCK + lax.broadcasted_iota(jnp.int32, (KEY_BLOCK, Q_TILE), 0)
        return (s_idx // CHUNK) <= q_chunk

    wix = wixt_ref[...] * (IDX_DIM ** -0.5)

    def idx_body(kb, carry):
        r0 = pl.multiple_of(kb * KEY_BLOCK, KEY_BLOCK)
        kx = kix_ref[pl.ds(r0, KEY_BLOCK), :]
        kx_even, kx_odd = kx[:, :LANES], kx[:, LANES:]
        acc = jnp.zeros((KEY_BLOCK, Q_TILE), F32)
        for pair in range(N_IDX_HEADS // 2):
            rhs = qi_ref[:, pair * LANES:(pair + 1) * LANES]
            for par, kxp in ((0, kx_even), (1, kx_odd)):
                h = 2 * pair + par
                d = lax.dot_general(kxp, rhs, (((1,), (1,)), ((), ())), preferred_element_type=F32)
                acc = acc + wix[h:h + 1, :] * jnp.maximum(d, 0.0)
        isc_ref[pl.ds(r0, KEY_BLOCK), :] = jnp.where(admissible(kb), acc, neg_inf)
        return carry

    lax.fori_loop(0, nkb, idx_body, 0)

    def count(pred):
        def body(kb, cnt):
            r0 = pl.multiple_of(kb * KEY_BLOCK, KEY_BLOCK)
            v = isc_ref[pl.ds(r0, KEY_BLOCK), :]
            return cnt + _fold8(jnp.where(pred(v), 1, 0).astype(jnp.int32))
        c8 = lax.fori_loop(0, nkb, body, jnp.zeros((8, Q_TILE), jnp.int32))
        return jnp.sum(c8, axis=0, keepdims=True)

    int_min = jnp.int32(-2 ** 31)

    def search_body(step, key):
        bit = 31 - step
        inc = jnp.where(bit == 31, int_min, jnp.left_shift(jnp.int32(1), jnp.minimum(bit, 30)))
        cand = key + inc
        thr = _sortable_to_f32(cand)
        n_ge = count(lambda v: v >= thr)
        return jnp.where(n_ge >= k_top, cand, key)

    key = lax.fori_loop(0, 32, search_body, jnp.full((1, Q_TILE), int_min, jnp.int32))
    thr = _sortable_to_f32(key)
    n_gt = count(lambda v: v > thr)
    need = (k_top - n_gt).astype(F32)
    n_adm = (t_idx[0:1, :] // CHUNK + 1) * CHUNK
    thr = jnp.where(n_adm <= k_top, neg_inf, thr)

    tri = (lax.broadcasted_iota(jnp.int32, (KEY_BLOCK, KEY_BLOCK), 1)
           <= lax.broadcasted_iota(jnp.int32, (KEY_BLOCK, KEY_BLOCK), 0)).astype(BF16)

    def sel_body(kb, tie_carry):
        r0 = pl.multiple_of(kb * KEY_BLOCK, KEY_BLOCK)
        v = isc_ref[pl.ds(r0, KEY_BLOCK), :]
        tie = (v == thr).astype(BF16)
        rank = jnp.dot(tri, tie, preferred_element_type=F32) + tie_carry
        sel = ((v > thr) | ((v == thr) & (rank <= need))) & admissible(kb)
        bias_ref[pl.ds(r0, KEY_BLOCK), :] = jnp.where(sel, 0.0, neg_inf)
        return rank[KEY_BLOCK - 1:KEY_BLOCK, :]

    lax.fori_loop(0, nkb, sel_body, jnp.zeros((1, Q_TILE), F32))

    for h in range(N_HEADS):
        qh = q_ref[:, h * HEAD_DIM:(h + 1) * HEAD_DIM]
        ql = lax.dot_general(wuk_ref[h], qh, (((1,), (1,)), ((), ())), preferred_element_type=F32)
        qlat_ref[:, h * Q_TILE:(h + 1) * Q_TILE] = ql.astype(qlat_ref.dtype)

    attn_scale = HEAD_DIM ** -0.5
    gw = HEADS_PER_GROUP * Q_TILE
    n_groups = N_HEADS // HEADS_PER_GROUP
    acc_ref[...] = jnp.zeros_like(acc_ref)
    m_ref[...] = jnp.full_like(m_ref, M_FLOOR)
    l_ref[...] = jnp.zeros_like(l_ref)

    def raw_scores(kb, g):
        r0 = pl.multiple_of(kb * KEY_BLOCK, KEY_BLOCK)
        return jnp.dot(c_ref[pl.ds(r0, KEY_BLOCK), :], qlat_ref[:, g * gw:(g + 1) * gw],
                       preferred_element_type=F32)

    r_ref[0] = raw_scores(0, 0)

    def attn_body(kb, carry):
        r0 = pl.multiple_of(kb * KEY_BLOCK, KEY_BLOCK)
        ct_blk = ct_ref[kb]
        b = bias_ref[pl.ds(r0, KEY_BLOCK), :]
        b_g = jnp.concatenate([b] * HEADS_PER_GROUP, axis=1)
        kb_next = jnp.minimum(kb + 1, nkb - 1)

        def pv_update(g, alpha):
            acc_ref[g] = alpha * acc_ref[g] + jnp.dot(ct_blk, p_ref[g % 2], preferred_element_type=F32)

        pending = None
        for g in range(n_groups):
            slot = g % 2
            r_ref[1 - slot] = raw_scores(kb, g + 1) if g + 1 < n_groups else raw_scores(kb_next, 0)
            if pending is not None:
                pv_update(*pending)
            s = r_ref[slot] * attn_scale + b_g
            r_ref[slot] = s
            m_old = m_ref[g]
            m_blk = jnp.max(jnp.max(s.reshape(KEY_BLOCK // 8, 8, gw), axis=0), axis=0, keepdims=True)
            m_new = jnp.maximum(m_old, m_blk)
            alpha = jnp.exp(m_old - m_new)
            p = jnp.exp(r_ref[slot] - m_new)
            p_ref[slot] = p.astype(BF16)
            m_ref[g] = m_new
            l_ref[g] = alpha * l_ref[g] + _fold8(p)
            pending = (g, alpha)
        pv_update(*pending)
        return carry

    lax.fori_loop(0, nkb, attn_body, 0)

    for g in range(n_groups):
        inv_l = 1.0 / jnp.sum(l_ref[g], axis=0, keepdims=True)
        o_lat = (acc_ref[g] * inv_l).astype(BF16)
        for hh in range(HEADS_PER_GROUP):
            h = g * HEADS_PER_GROUP + hh
            out_t = jnp.dot(wuvt_ref[h], o_lat[:, hh * Q_TILE:(hh + 1) * Q_TILE],
                            preferred_element_type=F32)
            gate = ga_ref[:, h * HEAD_DIM:(h + 1) * HEAD_DIM]
            o_ref[:, h * HEAD_DIM:(h + 1) * HEAD_DIM] = (
                out_t.T * (gate * jax.nn.sigmoid(gate))).astype(o_ref.dtype)


def _sparse_attention(qqi, wixt, kix, c, ct, w_uk, w_uvt, gates, k_top):
    B, S, _ = qqi.shape
    DA = N_HEADS * HEAD_DIM
    DI = N_IDX_HEADS * IDX_DIM
    nq = S // Q_TILE
    gw = HEADS_PER_GROUP * Q_TILE
    return pl.pallas_call(
        functools.partial(_attn_kernel, k_top=k_top),
        grid=(B, nq),
        in_specs=[pl.BlockSpec((None, Q_TILE, DA), lambda b, i: (b, i, 0)),
                  pl.BlockSpec((None, Q_TILE, DI), lambda b, i: (b, i, DA // DI)),
                  pl.BlockSpec((None, N_IDX_HEADS, Q_TILE), lambda b, i: (b, 0, i)),
                  pl.BlockSpec((None, S, 2 * LANES), lambda b, i: (b, 0, 0)),
                  pl.BlockSpec((None, S, KV_LATENT), lambda b, i: (b, 0, 0)),
                  pl.BlockSpec((None,) + ct.shape[1:], lambda b, i: (b, 0, 0, 0)),
                  pl.BlockSpec(w_uk.shape, lambda b, i: (0, 0, 0)),
                  pl.BlockSpec(w_uvt.shape, lambda b, i: (0, 0, 0)),
                  pl.BlockSpec((Q_TILE, DA), lambda b, i: (b * nq + i, 0))],
        out_specs=pl.BlockSpec((Q_TILE, DA), lambda b, i: (b * nq + i, 0)),
        out_shape=jax.ShapeDtypeStruct((B * S, DA), BF16),
        scratch_shapes=[pltpu.VMEM((S, Q_TILE), F32),
                        pltpu.VMEM((S, Q_TILE), F32),
                        pltpu.VMEM((KV_LATENT, N_HEADS * Q_TILE), BF16),
                        pltpu.VMEM((N_HEADS // HEADS_PER_GROUP, KV_LATENT, gw), F32),
                        pltpu.VMEM((N_HEADS // HEADS_PER_GROUP, 1, gw), F32),
                        pltpu.VMEM((N_HEADS // HEADS_PER_GROUP, 8, gw), F32),
                        pltpu.VMEM((2, KEY_BLOCK, gw), F32),
                        pltpu.VMEM((2, KEY_BLOCK, gw), BF16)],
        compiler_params=_cparams(("parallel", "arbitrary")),
        name="sparse_attention",
    )(qqi, qqi, wixt, kix, c, ct, w_uk, w_uvt, gates)


POOL_HALO = 16


def _pool_kernel(pin_ref, halo_ref, gate_ref, wp_ref, ps_ref, o_ref):
    i = pl.program_id(1)
    ts = pin_ref.shape[0]
    t = i * ts + lax.broadcasted_iota(jnp.int32, (ts, POOL_GROUP), 0)
    for g, win in enumerate(POOL_WINDOWS):
        cols = slice(g * POOL_GROUP, (g + 1) * POOL_GROUP)
        cur = pin_ref[:, cols]
        halo = jnp.where(i == 0, 0.0, halo_ref[:, cols])
        ssum = jnp.concatenate([halo, cur], axis=0)
        span = 1
        while span < win:
            ssum = ssum + pltpu.roll(ssum, span, 0)
            span *= 2
        count = jnp.minimum(t + 1, win).astype(F32)
        mixed = ssum[POOL_HALO:, :] / count - cur
        y = jnp.dot(mixed.astype(BF16), wp_ref[g], preferred_element_type=F32)
        gate = gate_ref[:, cols]
        o_ref[:, cols] = (y * ps_ref[:, cols] * (gate * jax.nn.sigmoid(gate))).astype(o_ref.dtype)


def _pool_mixer(gates, w_pool, pool_scale, B, S, ts=256):
    T = gates.shape[0]
    D = gates.shape[1] // 3
    nS = S // ts
    hb = ts // POOL_HALO
    return pl.pallas_call(
        _pool_kernel,
        grid=(B, nS),
        in_specs=[pl.BlockSpec((ts, D), lambda b, i: (b * nS + i, 1)),
                  pl.BlockSpec((POOL_HALO, D), lambda b, i: (jnp.maximum((b * nS + i) * hb - 1, 0), 1)),
                  pl.BlockSpec((ts, D), lambda b, i: (b * nS + i, 2)),
                  pl.BlockSpec(w_pool.shape, lambda b, i: (0, 0, 0)),
                  pl.BlockSpec((1, D), lambda b, i: (0, 0))],
        out_specs=pl.BlockSpec((ts, D), lambda b, i: (b * nS + i, 0)),
        out_shape=jax.ShapeDtypeStruct((T, D), BF16),
        compiler_params=_cparams(("parallel", "arbitrary")),
        name="pool_mixer",
    )(gates, gates, gates, w_pool, pool_scale.reshape(1, D))


OUT_COL_TILE = 512
OUT_PROJ_VMEM_LIMIT = 60 * 1024 * 1024


def _outproj_kernel(ya_ref, yb_ref, w_ref, x_ref, g_ref, o_ref):
    da = ya_ref.shape[1]
    tm, d = o_ref.shape
    n_col = d // OUT_COL_TILE
    ss = jnp.zeros((tm, LANES), F32)
    for jj in range(n_col):
        cols = slice(jj * OUT_COL_TILE, (jj + 1) * OUT_COL_TILE)
        y = (jnp.dot(ya_ref[...], w_ref[:da, cols], preferred_element_type=F32)
             + jnp.dot(yb_ref[...], w_ref[da:, cols], preferred_element_type=F32))
        o_ref[:, cols] = y
        for k in range(OUT_COL_TILE // LANES):
            yk = y[:, k * LANES:(k + 1) * LANES]
            ss = ss + yk * yk
    rs = lax.rsqrt(jnp.sum(ss, axis=-1, keepdims=True) * (1.0 / d) + NORM_EPS)
    for jj in range(n_col):
        cols = slice(jj * OUT_COL_TILE, (jj + 1) * OUT_COL_TILE)
        o_ref[:, cols] = x_ref[:, cols] + o_ref[:, cols] * rs * g_ref[:, cols]


def _out_proj(ya, yb, w_out_b, x2, post_g, tm=256):
    T, D = x2.shape
    DA = ya.shape[1]
    DB = yb.shape[1]
    return pl.pallas_call(
        _outproj_kernel,
        grid=(T // tm,),
        in_specs=[pl.BlockSpec((tm, DA), lambda i: (i, 0)),
                  pl.BlockSpec((tm, DB), lambda i: (i, 0)),
                  pl.BlockSpec((DA + DB, D), lambda i: (0, 0), pipeline_mode=pl.Buffered(1)),
                  pl.BlockSpec((tm, D), lambda i: (i, 0)),
                  pl.BlockSpec((1, D), lambda i: (0, 0))],
        out_specs=pl.BlockSpec((tm, D), lambda i: (i, 0)),
        out_shape=jax.ShapeDtypeStruct((T, D), F32),
        compiler_params=pltpu.CompilerParams(dimension_semantics=("parallel",),
                                             vmem_limit_bytes=OUT_PROJ_VMEM_LIMIT),
        name="out_proj",
    )(ya, yb, w_out_b, x2, post_g.reshape(1, D))


PROJ_TN = 512


def _layer(x2, B, S, pre_g, w_in, kv_g, w_uk, w_uv, ln_g, ln_b, w_pool, pool_scale, w_out, post_g, k_top):
    d_attn = N_HEADS * HEAD_DIM
    n_qi = N_IDX_HEADS * IDX_DIM
    o_c = d_attn
    o_qi = o_c + KV_LATENT
    o_k = o_qi + n_qi
    o_w = o_k + IDX_DIM
    o_g = o_w + N_IDX_HEADS
    K, n_in = w_in.shape
    wt = w_in.T
    w_uk_b = w_uk.astype(BF16)
    w_uvt_b = jnp.swapaxes(w_uv, 1, 2).astype(BF16)
    w_pool_b = w_pool.astype(BF16)
    w_out_b = w_out.astype(BF16)

    h = _prenorm(x2, pre_g)
    tn = PROJ_TN
    assert o_c % tn == 0 and o_qi % tn == 0 and n_qi % tn == 0
    qqi_spec = pl.BlockSpec((tn, K), lambda i, j: (jnp.where(j < o_c // tn, j, j + (o_qi - o_c) // tn), 0))
    qqi = _matmul_nt(h, wt, qqi_spec, d_attn + n_qi, BF16, 1024, tn, "proj_q_qi")
    assert o_g % 16 == 0
    gates_spec = pl.BlockSpec((pl.Element(tn), pl.Element(K)),
                              lambda i, j: (pl.multiple_of(o_g + j * tn, 16), 0))
    gates = _matmul_nt(h, wt, gates_spec, n_in - o_g, F32, 1024, tn, "proj_gates")
    c, ct, kix, wixt = _latent_proj(h, wt, o_c, o_k, kv_g, ln_g, ln_b, B, S)
    ya = _sparse_attention(qqi.reshape(B, S, d_attn + n_qi), wixt, kix, c, ct, w_uk_b, w_uvt_b, gates, k_top)
    yb = _pool_mixer(gates, w_pool_b, pool_scale, B, S)
    return _out_proj(ya, yb, w_out_b, x2, post_g)


def kernel(x, pre_norm, w_in, kv_norm, w_uk, w_uv, idx_k_norm_g, idx_k_norm_b, w_pool, pool_scale, w_out, post_norm):
    B, S, D = x.shape
    k_top = min(TOPK_MAX, S // 4)
    x2 = x.reshape(B * S, D)
    for l in range(pre_norm.shape[0]):
        x2 = _layer(x2, B, S, pre_norm[l], w_in[l], kv_norm[l], w_uk[l], w_uv[l], idx_k_norm_g[l],
                    idx_k_norm_b[l], w_pool[l], pool_scale[l], w_out[l], post_norm[l], k_top)
    return x2.reshape(B, S, D)
```

```python
import functools
import math

import jax
import jax.numpy as jnp
from jax import lax
from jax.experimental import pallas as pl
from jax.experimental.pallas import tpu as pltpu

F32 = jnp.float32
BF16 = jnp.bfloat16

NORM_EPS = 1e-6
CHUNK = 64
N_HEADS = 16
HEAD_DIM = 128
KV_LATENT = 512
N_IDX_HEADS = 16
IDX_DIM = 64
TOPK_MAX = 256
POOL_WINDOWS = (2, 4, 8, 16)
POOL_GROUP = 512

LANES = 128
Q_TILE = 256
KEY_BLOCK = 256
BF16_SUBLANES = 16
V_ROWS = HEAD_DIM + BF16_SUBLANES
M_FLOOR = -1e30
LOG2E = math.log2(math.e)
VMEM_LIMIT = 56 * 1024 * 1024


def _cparams(sem):
    return pltpu.CompilerParams(dimension_semantics=sem, vmem_limit_bytes=VMEM_LIMIT)


def _prenorm_kernel(x_ref, g_ref, o_ref):
    x = x_ref[...]
    ms = jnp.mean(x * x, axis=-1, keepdims=True)
    o_ref[...] = (x * lax.rsqrt(ms + NORM_EPS) * g_ref[...]).astype(o_ref.dtype)


def _prenorm(x2, g, tm=256):
    T, D = x2.shape
    return pl.pallas_call(
        _prenorm_kernel,
        grid=(T // tm,),
        in_specs=[pl.BlockSpec((tm, D), lambda i: (i, 0)),
                  pl.BlockSpec((1, D), lambda i: (0, 0))],
        out_specs=pl.BlockSpec((tm, D), lambda i: (i, 0)),
        out_shape=jax.ShapeDtypeStruct((T, D), BF16),
        compiler_params=_cparams(("parallel",)),
        name="prenorm",
    )(x2, g.reshape(1, D))


_NT = (((1,), (1,)), ((), ()))


def _matmul_nt_kernel(h_ref, wt_ref, o_ref):
    o_ref[...] = lax.dot_general(h_ref[...], wt_ref[...].astype(BF16), _NT,
                                 preferred_element_type=F32).astype(o_ref.dtype)


def _matmul_nt(h, wt, w_spec, n_out, out_dtype, tm, tn, name):
    T, K = h.shape
    tm = min(tm, T)
    return pl.pallas_call(
        _matmul_nt_kernel,
        grid=(T // tm, n_out // tn),
        in_specs=[pl.BlockSpec((tm, K), lambda i, j: (i, 0)), w_spec],
        out_specs=pl.BlockSpec((tm, tn), lambda i, j: (i, j)),
        out_shape=jax.ShapeDtypeStruct((T, n_out), out_dtype),
        compiler_params=_cparams(("parallel", "arbitrary")),
        name=name,
    )(h, wt)


def _latent_kernel(h_ref, wc_ref, wkw_ref, kvg_ref, lng_ref, lnb_ref, wuk_ref, wuvt_ref,
                   kh_ref, vt_ref, kix_ref, wixt_ref):
    h = h_ref[...]
    c = lax.dot_general(h, wc_ref[...].astype(BF16), _NT, preferred_element_type=F32)
    cn = c * lax.rsqrt(jnp.mean(c * c, axis=-1, keepdims=True) + NORM_EPS) * kvg_ref[...]
    cb = cn.astype(BF16)
    n_kb = kh_ref.shape[0]
    for pair in range(N_HEADS // 2):
        k2 = jnp.dot(cb, wuk_ref[:, pair * 2 * HEAD_DIM:(pair + 1) * 2 * HEAD_DIM],
                     preferred_element_type=F32)
        for u in range(n_kb):
            for e in range(2):
                kh_ref[u, 2 * pair + e] = k2[u * KEY_BLOCK:(u + 1) * KEY_BLOCK,
                                             e * HEAD_DIM:(e + 1) * HEAD_DIM].astype(kh_ref.dtype)
    for u in range(n_kb):
        ct_u = cn[u * KEY_BLOCK:(u + 1) * KEY_BLOCK, :].T.astype(BF16)
        v_all = jnp.dot(wuvt_ref[...], ct_u, preferred_element_type=F32)
        for hh in range(N_HEADS):
            vt_ref[u, hh, :HEAD_DIM, :] = v_all[hh * HEAD_DIM:(hh + 1) * HEAD_DIM, :].astype(vt_ref.dtype)
            vt_ref[u, hh, HEAD_DIM:, :] = jnp.ones((V_ROWS - HEAD_DIM, KEY_BLOCK), vt_ref.dtype)
    kw = lax.dot_general(h, wkw_ref[...].astype(BF16), _NT, preferred_element_type=F32)
    lane = lax.broadcasted_iota(jnp.int32, kw.shape, 1)
    is_k = lane < IDX_DIM
    mu = jnp.sum(jnp.where(is_k, kw, 0.0), axis=-1, keepdims=True) * (1.0 / IDX_DIM)
    d = jnp.where(is_k, kw - mu, 0.0)
    var = jnp.sum(d * d, axis=-1, keepdims=True) * (1.0 / IDX_DIM)
    kn = d * lax.rsqrt(var + NORM_EPS) * lng_ref[...] + lnb_ref[...]
    kix_ref[...] = jnp.concatenate([kn, pltpu.roll(kn, IDX_DIM, 1)], axis=1).astype(kix_ref.dtype)
    wixt_ref[...] = kw.T[IDX_DIM:IDX_DIM + N_IDX_HEADS, :] * (N_IDX_HEADS ** -0.5)


def _latent_proj(h, wt, o_c, o_k, kv_norm, ln_g, ln_b, w_uk_all, w_uvt_all, B, S, tm=512):
    T, K = h.shape
    nS = S // tm
    pad = jnp.zeros((LANES - IDX_DIM,), F32)
    lng = jnp.concatenate([ln_g, pad]).reshape(1, LANES)
    lnb = jnp.concatenate([ln_b, pad]).reshape(1, LANES)
    assert o_c % KV_LATENT == 0 and o_k % LANES == 0
    return pl.pallas_call(
        _latent_kernel,
        grid=(B, nS),
        in_specs=[pl.BlockSpec((tm, K), lambda b, i: (b * nS + i, 0)),
                  pl.BlockSpec((KV_LATENT, K), lambda b, i: (o_c // KV_LATENT, 0)),
                  pl.BlockSpec((LANES, K), lambda b, i: (o_k // LANES, 0)),
                  pl.BlockSpec((1, KV_LATENT), lambda b, i: (0, 0)),
                  pl.BlockSpec((1, LANES), lambda b, i: (0, 0)),
                  pl.BlockSpec((1, LANES), lambda b, i: (0, 0)),
                  pl.BlockSpec(w_uk_all.shape, lambda b, i: (0, 0)),
                  pl.BlockSpec(w_uvt_all.shape, lambda b, i: (0, 0))],
        out_specs=[pl.BlockSpec((None, tm // KEY_BLOCK, N_HEADS, KEY_BLOCK, HEAD_DIM),
                                lambda b, i: (b, i, 0, 0, 0)),
                   pl.BlockSpec((None, tm // KEY_BLOCK, N_HEADS, V_ROWS, KEY_BLOCK),
                                lambda b, i: (b, i, 0, 0, 0)),
                   pl.BlockSpec((None, tm, 2 * LANES), lambda b, i: (b, i, 0)),
                   pl.BlockSpec((None, N_IDX_HEADS, tm), lambda b, i: (b, 0, i))],
        out_shape=[jax.ShapeDtypeStruct((B, S // KEY_BLOCK, N_HEADS, KEY_BLOCK, HEAD_DIM), BF16),
                   jax.ShapeDtypeStruct((B, S // KEY_BLOCK, N_HEADS, V_ROWS, KEY_BLOCK), BF16),
                   jax.ShapeDtypeStruct((B, S, 2 * LANES), BF16),
                   jax.ShapeDtypeStruct((B, N_IDX_HEADS, S), F32)],
        compiler_params=_cparams(("parallel", "arbitrary")),
        name="latent_proj",
    )(h, wt, wt, kv_norm.reshape(1, KV_LATENT), lng, lnb, w_uk_all, w_uvt_all)


def _sortable_to_f32(key):
    bits = key ^ ((key >> 31) & jnp.int32(0x7FFFFFFF))
    return pltpu.bitcast(bits, F32)


def _fold8(a):
    return jnp.sum(a.reshape(a.shape[0] // 8, 8, a.shape[1]), axis=0)


def _attn_kernel(q_ref, qi_ref, wixt_ref, kix_ref, kh_ref, vt_ref, ga_ref,
                 o_ref, isc_ref, bias_ref, acc_ref, m_ref, r_ref, p_ref, *, k_top):
    i = pl.program_id(1)
    t0 = i * Q_TILE
    nkb = (t0 + Q_TILE + KEY_BLOCK - 1) // KEY_BLOCK
    neg_inf = jnp.float32(-jnp.inf)

    t_idx = t0 + lax.broadcasted_iota(jnp.int32, (KEY_BLOCK, Q_TILE), 1)
    q_chunk = t_idx // CHUNK

    def admissible(kb):
        s_idx = kb * KEY_BLOCK + lax.broadcasted_iota(jnp.int32, (KEY_BLOCK, Q_TILE), 0)
        return (s_idx // CHUNK) <= q_chunk

    wix = wixt_ref[...] * (IDX_DIM ** -0.5)

    def idx_body(kb, carry):
        r0 = pl.multiple_of(kb * KEY_BLOCK, KEY_BLOCK)
        kx = kix_ref[pl.ds(r0, KEY_BLOCK), :]
        kx_even, kx_odd = kx[:, :LANES], kx[:, LANES:]
        acc = jnp.zeros((KEY_BLOCK, Q_TILE), F32)
        for pair in range(N_IDX_HEADS // 2):
            rhs = qi_ref[:, pair * LANES:(pair + 1) * LANES]
            for par, kxp in ((0, kx_even), (1, kx_odd)):
                h = 2 * pair + par
                d = lax.dot_general(kxp, rhs, (((1,), (1,)), ((), ())), preferred_element_type=F32)
                acc = acc + wix[h:h + 1, :] * jnp.maximum(d, 0.0)
        isc_ref[pl.ds(r0, KEY_BLOCK), :] = jnp.where(admissible(kb), acc, neg_inf)
        return carry

    lax.fori_loop(0, nkb, idx_body, 0)

    def count(pred):
        def body(kb, cnt):
            r0 = pl.multiple_of(kb * KEY_BLOCK, KEY_BLOCK)
            v = isc_ref[pl.ds(r0, KEY_BLOCK), :]
            return cnt + _fold8(jnp.where(pred(v), 1, 0).astype(jnp.int32))
        c8 = lax.fori_loop(0, nkb, body, jnp.zeros((8, Q_TILE), jnp.int32))
        return jnp.sum(c8, axis=0, keepdims=True)

    int_min = jnp.int32(-2 ** 31)

    def search_body(step, key):
        bit = 31 - step
        inc = jnp.where(bit == 31, int_min, jnp.left_shift(jnp.int32(1), jnp.minimum(bit, 30)))
        cand = key + inc
        thr = _sortable_to_f32(cand)
        n_ge = count(lambda v: v >= thr)
        return jnp.where(n_ge >= k_top, cand, key)

    key = lax.fori_loop(0, 32, search_body, jnp.full((1, Q_TILE), int_min, jnp.int32))
    thr = _sortable_to_f32(key)
    n_gt = count(lambda v: v > thr)
    need = (k_top - n_gt).astype(F32)
    n_adm = (t_idx[0:1, :] // CHUNK + 1) * CHUNK
    thr = jnp.where(n_adm <= k_top, neg_inf, thr)

    tri = (lax.broadcasted_iota(jnp.int32, (KEY_BLOCK, KEY_BLOCK), 1)
           <= lax.broadcasted_iota(jnp.int32, (KEY_BLOCK, KEY_BLOCK), 0)).astype(BF16)

    def sel_body(kb, tie_carry):
        r0 = pl.multiple_of(kb * KEY_BLOCK, KEY_BLOCK)
        v = isc_ref[pl.ds(r0, KEY_BLOCK), :]
        tie = (v == thr).astype(BF16)
        rank = jnp.dot(tri, tie, preferred_element_type=F32) + tie_carry
        sel = ((v > thr) | ((v == thr) & (rank <= need))) & admissible(kb)
        bias_ref[pl.ds(r0, KEY_BLOCK), :] = jnp.where(sel, 0.0, neg_inf)
        return rank[KEY_BLOCK - 1:KEY_BLOCK, :]

    lax.fori_loop(0, nkb, sel_body, jnp.zeros((1, Q_TILE), F32))

    exp_scale = (HEAD_DIM ** -0.5) * LOG2E
    acc_ref[...] = jnp.zeros_like(acc_ref)
    m_ref[...] = jnp.full_like(m_ref, M_FLOOR)

    def raw_scores(kb, h):
        return lax.dot_general(kh_ref[kb, h], q_ref[:, h * HEAD_DIM:(h + 1) * HEAD_DIM], _NT,
                               preferred_element_type=F32)

    r_ref[0] = raw_scores(0, 0)

    def attn_body(kb, carry):
        r0 = pl.multiple_of(kb * KEY_BLOCK, KEY_BLOCK)
        kb_next = jnp.minimum(kb + 1, nkb - 1)

        def pv_update(h, alpha):
            acc_ref[h] = alpha * acc_ref[h] + jnp.dot(vt_ref[kb, h], p_ref[h % 2],
                                                      preferred_element_type=F32)

        pending = None
        for h in range(N_HEADS):
            slot = h % 2
            r_ref[1 - slot] = raw_scores(kb, h + 1) if h + 1 < N_HEADS else raw_scores(kb_next, 0)
            if pending is not None:
                pv_update(*pending)
            s = r_ref[slot] * exp_scale + bias_ref[pl.ds(r0, KEY_BLOCK), :]
            r_ref[slot] = s
            m_old = m_ref[h]
            m_blk = jnp.max(jnp.max(s.reshape(KEY_BLOCK // 8, 8, Q_TILE), axis=0), axis=0, keepdims=True)
            m_new = jnp.maximum(m_old, m_blk)
            alpha = jnp.exp2(m_old - m_new)
            p_ref[slot] = jnp.exp2(r_ref[slot] - m_new).astype(BF16)
            m_ref[h] = m_new
            pending = (h, alpha)
        pv_update(*pending)
        return carry

    lax.fori_loop(0, nkb, attn_body, 0)

    for h in range(N_HEADS):
        a = acc_ref[h]
        out_t = a[:HEAD_DIM, :] * (1.0 / a[HEAD_DIM:HEAD_DIM + 1, :])
        gate = ga_ref[:, h * HEAD_DIM:(h + 1) * HEAD_DIM]
        o_ref[:, h * HEAD_DIM:(h + 1) * HEAD_DIM] = (
            out_t.T * (gate * jax.nn.sigmoid(gate))).astype(o_ref.dtype)


def _sparse_attention(qqi, wixt, kix, kh, vt, gates, k_top):
    B, S, _ = qqi.shape
    DA = N_HEADS * HEAD_DIM
    DI = N_IDX_HEADS * IDX_DIM
    nq = S // Q_TILE
    return pl.pallas_call(
        functools.partial(_attn_kernel, k_top=k_top),
        grid=(B, nq),
        in_specs=[pl.BlockSpec((None, Q_TILE, DA), lambda b, i: (b, i, 0)),
                  pl.BlockSpec((None, Q_TILE, DI), lambda b, i: (b, i, DA // DI)),
                  pl.BlockSpec((None, N_IDX_HEADS, Q_TILE), lambda b, i: (b, 0, i)),
                  pl.BlockSpec((None, S, 2 * LANES), lambda b, i: (b, 0, 0)),
                  pl.BlockSpec((None,) + kh.shape[1:], lambda b, i: (b, 0, 0, 0, 0)),
                  pl.BlockSpec((None,) + vt.shape[1:], lambda b, i: (b, 0, 0, 0, 0),
                               pipeline_mode=pl.Buffered(1)),
                  pl.BlockSpec((Q_TILE, DA), lambda b, i: (b * nq + i, 0))],
        out_specs=pl.BlockSpec((Q_TILE, DA), lambda b, i: (b * nq + i, 0)),
        out_shape=jax.ShapeDtypeStruct((B * S, DA), BF16),
        scratch_shapes=[pltpu.VMEM((S, Q_TILE), F32),
                        pltpu.VMEM((S, Q_TILE), F32),
                        pltpu.VMEM((N_HEADS, V_ROWS, Q_TILE), F32),
                        pltpu.VMEM((N_HEADS, 1, Q_TILE), F32),
                        pltpu.VMEM((2, KEY_BLOCK, Q_TILE), F32),
                        pltpu.VMEM((2, KEY_BLOCK, Q_TILE), BF16)],
        compiler_params=_cparams(("parallel", "arbitrary")),
        name="sparse_attention",
    )(qqi, qqi, wixt, kix, kh, vt, gates)


POOL_HALO = 16


def _pool_kernel(pin_ref, halo_ref, gate_ref, wp_ref, ps_ref, o_ref):
    i = pl.program_id(1)
    ts = pin_ref.shape[0]
    t = i * ts + lax.broadcasted_iota(jnp.int32, (ts, POOL_GROUP), 0)
    for g, win in enumerate(POOL_WINDOWS):
        cols = slice(g * POOL_GROUP, (g + 1) * POOL_GROUP)
        cur = pin_ref[:, cols]
        halo = jnp.where(i == 0, 0.0, halo_ref[:, cols])
        ssum = jnp.concatenate([halo, cur], axis=0)
        span = 1
        while span < win:
            ssum = ssum + pltpu.roll(ssum, span, 0)
            span *= 2
        count = jnp.minimum(t + 1, win).astype(F32)
        mixed = ssum[POOL_HALO:, :] / count - cur
        y = jnp.dot(mixed.astype(BF16), wp_ref[g], preferred_element_type=F32)
        gate = gate_ref[:, cols]
        o_ref[:, cols] = (y * ps_ref[:, cols] * (gate * jax.nn.sigmoid(gate))).astype(o_ref.dtype)


def _pool_mixer(gates, w_pool, pool_scale, B, S, ts=256):
    T = gates.shape[0]
    D = gates.shape[1] // 3
    nS = S // ts
    hb = ts // POOL_HALO
    return pl.pallas_call(
        _pool_kernel,
        grid=(B, nS),
        in_specs=[pl.BlockSpec((ts, D), lambda b, i: (b * nS + i, 1)),
                  pl.BlockSpec((POOL_HALO, D), lambda b, i: (jnp.maximum((b * nS + i) * hb - 1, 0), 1)),
                  pl.BlockSpec((ts, D), lambda b, i: (b * nS + i, 2)),
                  pl.BlockSpec(w_pool.shape, lambda b, i: (0, 0, 0)),
                  pl.BlockSpec((1, D), lambda b, i: (0, 0))],
        out_specs=pl.BlockSpec((ts, D), lambda b, i: (b * nS + i, 0)),
        out_shape=jax.ShapeDtypeStruct((T, D), BF16),
        compiler_params=_cparams(("parallel", "arbitrary")),
        name="pool_mixer",
    )(gates, gates, gates, w_pool, pool_scale.reshape(1, D))


OUT_COL_TILE = 512
OUT_PROJ_VMEM_LIMIT = 60 * 1024 * 1024


def _outproj_kernel(ya_ref, yb_ref, w_ref, x_ref, g_ref, o_ref):
    da = ya_ref.shape[1]
    tm, d = o_ref.shape
    n_col = d // OUT_COL_TILE
    ss = jnp.zeros((tm, LANES), F32)
    for jj in range(n_col):
        cols = slice(jj * OUT_COL_TILE, (jj + 1) * OUT_COL_TILE)
        y = (jnp.dot(ya_ref[...], w_ref[:da, cols], preferred_element_type=F32)
             + jnp.dot(yb_ref[...], w_ref[da:, cols], preferred_element_type=F32))
        o_ref[:, cols] = y
        for k in range(OUT_COL_TILE // LANES):
            yk = y[:, k * LANES:(k + 1) * LANES]
            ss = ss + yk * yk
    rs = lax.rsqrt(jnp.sum(ss, axis=-1, keepdims=True) * (1.0 / d) + NORM_EPS)
    for jj in range(n_col):
        cols = slice(jj * OUT_COL_TILE, (jj + 1) * OUT_COL_TILE)
        o_ref[:, cols] = x_ref[:, cols] + o_ref[:, cols] * rs * g_ref[:, cols]


def _out_proj(ya, yb, w_out_b, x2, post_g, tm=256):
    T, D = x2.shape
    DA = ya.shape[1]
    DB = yb.shape[1]
    return pl.pallas_call(
        _outproj_kernel,
        grid=(T // tm,),
        in_specs=[pl.BlockSpec((tm, DA), lambda i: (i, 0)),
                  pl.BlockSpec((tm, DB), lambda i: (i, 0)),
                  pl.BlockSpec((DA + DB, D), lambda i: (0, 0), pipeline_mode=pl.Buffered(1)),
                  pl.BlockSpec((tm, D), lambda i: (i, 0)),
                  pl.BlockSpec((1, D), lambda i: (0, 0))],
        out_specs=pl.BlockSpec((tm, D), lambda i: (i, 0)),
        out_shape=jax.ShapeDtypeStruct((T, D), F32),
        compiler_params=pltpu.CompilerParams(dimension_semantics=("parallel",),
                                             vmem_limit_bytes=OUT_PROJ_VMEM_LIMIT),
        name="out_proj",
    )(ya, yb, w_out_b, x2, post_g.reshape(1, D))


PROJ_TN = 512


def _layer(x2, B, S, pre_g, w_in, kv_g, w_uk, w_uv, ln_g, ln_b, w_pool, pool_scale, w_out, post_g, k_top):
    d_attn = N_HEADS * HEAD_DIM
    n_qi = N_IDX_HEADS * IDX_DIM
    o_c = d_attn
    o_qi = o_c + KV_LATENT
    o_k = o_qi + n_qi
    o_w = o_k + IDX_DIM
    o_g = o_w + N_IDX_HEADS
    K, n_in = w_in.shape
    wt = w_in.T
    w_uk_all = jnp.transpose(w_uk, (1, 0, 2)).reshape(KV_LATENT, d_attn).astype(BF16)
    w_uvt_all = jnp.swapaxes(w_uv, 1, 2).reshape(d_attn, KV_LATENT).astype(BF16)
    w_pool_b = w_pool.astype(BF16)
    w_out_b = w_out.astype(BF16)

    h = _prenorm(x2, pre_g)
    tn = PROJ_TN
    assert o_c % tn == 0 and o_qi % tn == 0 and n_qi % tn == 0
    qqi_spec = pl.BlockSpec((tn, K), lambda i, j: (jnp.where(j < o_c // tn, j, j + (o_qi - o_c) // tn), 0))
    qqi = _matmul_nt(h, wt, qqi_spec, d_attn + n_qi, BF16, 1024, tn, "proj_q_qi")
    assert o_g % 16 == 0
    gates_spec = pl.BlockSpec((pl.Element(tn), pl.Element(K)),
                              lambda i, j: (pl.multiple_of(o_g + j * tn, 16), 0))
    gates = _matmul_nt(h, wt, gates_spec, n_in - o_g, F32, 1024, tn, "proj_gates")
    kh, vt, kix, wixt = _latent_proj(h, wt, o_c, o_k, kv_g, ln_g, ln_b, w_uk_all, w_uvt_all, B, S)
    ya = _sparse_attention(qqi.reshape(B, S, d_attn + n_qi), wixt, kix, kh, vt, gates, k_top)
    yb = _pool_mixer(gates, w_pool_b, pool_scale, B, S)
    return _out_proj(ya, yb, w_out_b, x2, post_g)


def kernel(x, pre_norm, w_in, kv_norm, w_uk, w_uv, idx_k_norm_g, idx_k_norm_b, w_pool, pool_scale, w_out, post_norm):
    B, S, D = x.shape
    k_top = min(TOPK_MAX, S // 4)
    x2 = x.reshape(B * S, D)
    for l in range(pre_norm.shape[0]):
        x2 = _layer(x2, B, S, pre_norm[l], w_in[l], kv_norm[l], w_uk[l], w_uv[l], idx_k_norm_g[l],
                    idx_k_norm_b[l], w_pool[l], pool_scale[l], w_out[l], post_norm[l], k_top)
    return x2.reshape(B, S, D)
```

```python
import functools
import math

import jax
import jax.numpy as jnp
from jax import lax
from jax.experimental import pallas as pl
from jax.experimental.pallas import tpu as pltpu

F32 = jnp.float32
BF16 = jnp.bfloat16

NORM_EPS = 1e-6
CHUNK = 64
N_HEADS = 16
HEAD_DIM = 128
KV_LATENT = 512
N_IDX_HEADS = 16
IDX_DIM = 64
TOPK_MAX = 256
POOL_WINDOWS = (2, 4, 8, 16)
POOL_GROUP = 512

LANES = 128
Q_TILE = 256
KEY_BLOCK = 256
BF16_SUBLANES = 16
V_ROWS = HEAD_DIM + BF16_SUBLANES
R_SLOTS = 4
P_SLOTS = 2
M_FLOOR = -1e30
LOG2E = math.log2(math.e)
VMEM_LIMIT = 56 * 1024 * 1024


def _cparams(sem):
    return pltpu.CompilerParams(dimension_semantics=sem, vmem_limit_bytes=VMEM_LIMIT)


def _prenorm_kernel(x_ref, g_ref, o_ref):
    x = x_ref[...]
    ms = jnp.mean(x * x, axis=-1, keepdims=True)
    o_ref[...] = (x * lax.rsqrt(ms + NORM_EPS) * g_ref[...]).astype(o_ref.dtype)


def _prenorm(x2, g, tm=256):
    T, D = x2.shape
    return pl.pallas_call(
        _prenorm_kernel,
        grid=(T // tm,),
        in_specs=[pl.BlockSpec((tm, D), lambda i: (i, 0)),
                  pl.BlockSpec((1, D), lambda i: (0, 0))],
        out_specs=pl.BlockSpec((tm, D), lambda i: (i, 0)),
        out_shape=jax.ShapeDtypeStruct((T, D), BF16),
        compiler_params=_cparams(("parallel",)),
        name="prenorm",
    )(x2, g.reshape(1, D))


_NT = (((1,), (1,)), ((), ()))


def _matmul_nt_kernel(h_ref, wt_ref, o_ref, *, n_scaled, scale):
    z = lax.dot_general(h_ref[...], wt_ref[...].astype(BF16), _NT, preferred_element_type=F32)
    if n_scaled:
        z = z * jnp.where(pl.program_id(1) < n_scaled, jnp.float32(scale), jnp.float32(1.0))
    o_ref[...] = z.astype(o_ref.dtype)


def _matmul_nt(h, wt, w_spec, n_out, out_dtype, tm, tn, name, n_scaled=0, scale=1.0):
    T, K = h.shape
    tm = min(tm, T)
    return pl.pallas_call(
        functools.partial(_matmul_nt_kernel, n_scaled=n_scaled, scale=scale),
        grid=(T // tm, n_out // tn),
        in_specs=[pl.BlockSpec((tm, K), lambda i, j: (i, 0)), w_spec],
        out_specs=pl.BlockSpec((tm, tn), lambda i, j: (i, j)),
        out_shape=jax.ShapeDtypeStruct((T, n_out), out_dtype),
        compiler_params=_cparams(("parallel", "arbitrary")),
        name=name,
    )(h, wt)


def _latent_kernel(h_ref, wc_ref, wkw_ref, kvg_ref, lng_ref, lnb_ref, wuk_ref, wuvt_ref,
                   kh_ref, vt_ref, kix_ref, wixt_ref):
    h = h_ref[...]
    c = lax.dot_general(h, wc_ref[...].astype(BF16), _NT, preferred_element_type=F32)
    cn = c * lax.rsqrt(jnp.mean(c * c, axis=-1, keepdims=True) + NORM_EPS) * kvg_ref[...]
    cb = cn.astype(BF16)
    n_kb = kh_ref.shape[0]
    for pair in range(N_HEADS // 2):
        k2 = jnp.dot(cb, wuk_ref[:, pair * 2 * HEAD_DIM:(pair + 1) * 2 * HEAD_DIM],
                     preferred_element_type=F32)
        for u in range(n_kb):
            for e in range(2):
                kh_ref[u, 2 * pair + e] = k2[u * KEY_BLOCK:(u + 1) * KEY_BLOCK,
                                             e * HEAD_DIM:(e + 1) * HEAD_DIM].astype(kh_ref.dtype)
    for u in range(n_kb):
        ct_u = cn[u * KEY_BLOCK:(u + 1) * KEY_BLOCK, :].T.astype(BF16)
        v_all = jnp.dot(wuvt_ref[...], ct_u, preferred_element_type=F32)
        for hh in range(N_HEADS):
            vt_ref[u, hh, :HEAD_DIM, :] = v_all[hh * HEAD_DIM:(hh + 1) * HEAD_DIM, :].astype(vt_ref.dtype)
            vt_ref[u, hh, HEAD_DIM:, :] = jnp.ones((V_ROWS - HEAD_DIM, KEY_BLOCK), vt_ref.dtype)
    kw = lax.dot_general(h, wkw_ref[...].astype(BF16), _NT, preferred_element_type=F32)
    lane = lax.broadcasted_iota(jnp.int32, kw.shape, 1)
    is_k = lane < IDX_DIM
    mu = jnp.sum(jnp.where(is_k, kw, 0.0), axis=-1, keepdims=True) * (1.0 / IDX_DIM)
    d = jnp.where(is_k, kw - mu, 0.0)
    var = jnp.sum(d * d, axis=-1, keepdims=True) * (1.0 / IDX_DIM)
    kn = d * lax.rsqrt(var + NORM_EPS) * lng_ref[...] + lnb_ref[...]
    kix_ref[...] = jnp.concatenate([kn, pltpu.roll(kn, IDX_DIM, 1)], axis=1).astype(kix_ref.dtype)
    wixt_ref[...] = kw.T[IDX_DIM:IDX_DIM + N_IDX_HEADS, :] * (N_IDX_HEADS ** -0.5)


def _latent_proj(h, wt, o_c, o_k, kv_norm, ln_g, ln_b, w_uk_all, w_uvt_all, B, S, tm=512):
    T, K = h.shape
    nS = S // tm
    pad = jnp.zeros((LANES - IDX_DIM,), F32)
    lng = jnp.concatenate([ln_g, pad]).reshape(1, LANES)
    lnb = jnp.concatenate([ln_b, pad]).reshape(1, LANES)
    assert o_c % KV_LATENT == 0 and o_k % LANES == 0
    return pl.pallas_call(
        _latent_kernel,
        grid=(B, nS),
        in_specs=[pl.BlockSpec((tm, K), lambda b, i: (b * nS + i, 0)),
                  pl.BlockSpec((KV_LATENT, K), lambda b, i: (o_c // KV_LATENT, 0)),
                  pl.BlockSpec((LANES, K), lambda b, i: (o_k // LANES, 0)),
                  pl.BlockSpec((1, KV_LATENT), lambda b, i: (0, 0)),
                  pl.BlockSpec((1, LANES), lambda b, i: (0, 0)),
                  pl.BlockSpec((1, LANES), lambda b, i: (0, 0)),
                  pl.BlockSpec(w_uk_all.shape, lambda b, i: (0, 0)),
                  pl.BlockSpec(w_uvt_all.shape, lambda b, i: (0, 0))],
        out_specs=[pl.BlockSpec((None, tm // KEY_BLOCK, N_HEADS, KEY_BLOCK, HEAD_DIM),
                                lambda b, i: (b, i, 0, 0, 0)),
                   pl.BlockSpec((None, tm // KEY_BLOCK, N_HEADS, V_ROWS, KEY_BLOCK),
                                lambda b, i: (b, i, 0, 0, 0)),
                   pl.BlockSpec((None, tm, 2 * LANES), lambda b, i: (b, i, 0)),
                   pl.BlockSpec((None, N_IDX_HEADS, tm), lambda b, i: (b, 0, i))],
        out_shape=[jax.ShapeDtypeStruct((B, S // KEY_BLOCK, N_HEADS, KEY_BLOCK, HEAD_DIM), BF16),
                   jax.ShapeDtypeStruct((B, S // KEY_BLOCK, N_HEADS, V_ROWS, KEY_BLOCK), BF16),
                   jax.ShapeDtypeStruct((B, S, 2 * LANES), BF16),
                   jax.ShapeDtypeStruct((B, N_IDX_HEADS, S), F32)],
        compiler_params=_cparams(("parallel", "arbitrary")),
        name="latent_proj",
    )(h, wt, wt, kv_norm.reshape(1, KV_LATENT), lng, lnb, w_uk_all, w_uvt_all)


def _sortable_to_f32(key):
    bits = key ^ ((key >> 31) & jnp.int32(0x7FFFFFFF))
    return pltpu.bitcast(bits, F32)


def _fold8(a):
    return jnp.sum(a.reshape(a.shape[0] // 8, 8, a.shape[1]), axis=0)


def _attn_kernel(q_ref, qi_ref, wixt_ref, kix_ref, kh_ref, vt_ref, ga_ref,
                 o_ref, isc_ref, bias_ref, acc_ref, m_ref, *stage_refs, k_top):
    r_ref, p_ref = stage_refs[:R_SLOTS], stage_refs[R_SLOTS:]
    i = pl.program_id(1)
    t0 = i * Q_TILE
    nkb = (t0 + Q_TILE + KEY_BLOCK - 1) // KEY_BLOCK
    neg_inf = jnp.float32(-jnp.inf)

    t_idx = t0 + lax.broadcasted_iota(jnp.int32, (KEY_BLOCK, Q_TILE), 1)
    q_chunk = t_idx // CHUNK

    def admissible(kb):
        s_idx = kb * KEY_BLOCK + lax.broadcasted_iota(jnp.int32, (KEY_BLOCK, Q_TILE), 0)
        return (s_idx // CHUNK) <= q_chunk

    wix = wixt_ref[...] * (IDX_DIM ** -0.5)

    def idx_body(kb, carry):
        r0 = pl.multiple_of(kb * KEY_BLOCK, KEY_BLOCK)
        kx = kix_ref[pl.ds(r0, KEY_BLOCK), :]
        kx_even, kx_odd = kx[:, :LANES], kx[:, LANES:]
        acc = jnp.zeros((KEY_BLOCK, Q_TILE), F32)
        for pair in range(N_IDX_HEADS // 2):
            rhs = qi_ref[:, pair * LANES:(pair + 1) * LANES]
            for par, kxp in ((0, kx_even), (1, kx_odd)):
                h = 2 * pair + par
                d = lax.dot_general(kxp, rhs, (((1,), (1,)), ((), ())), preferred_element_type=F32)
                acc = acc + wix[h:h + 1, :] * jnp.maximum(d, 0.0)
        isc_ref[pl.ds(r0, KEY_BLOCK), :] = jnp.where(admissible(kb), acc, neg_inf)
        return carry

    lax.fori_loop(0, nkb, idx_body, 0)

    def count(pred):
        def body(kb, cnt):
            r0 = pl.multiple_of(kb * KEY_BLOCK, KEY_BLOCK)
            v = isc_ref[pl.ds(r0, KEY_BLOCK), :]
            return cnt + _fold8(jnp.where(pred(v), 1, 0).astype(jnp.int32))
        c8 = lax.fori_loop(0, nkb, body, jnp.zeros((8, Q_TILE), jnp.int32))
        return jnp.sum(c8, axis=0, keepdims=True)

    int_min = jnp.int32(-2 ** 31)

    def search_body(step, key):
        bit = 31 - step
        inc = jnp.where(bit == 31, int_min, jnp.left_shift(jnp.int32(1), jnp.minimum(bit, 30)))
        cand = key + inc
        thr = _sortable_to_f32(cand)
        n_ge = count(lambda v: v >= thr)
        return jnp.where(n_ge >= k_top, cand, key)

    key = lax.fori_loop(0, 32, search_body, jnp.full((1, Q_TILE), int_min, jnp.int32))
    thr = _sortable_to_f32(key)
    n_gt = count(lambda v: v > thr)
    need = (k_top - n_gt).astype(F32)
    n_adm = (t_idx[0:1, :] // CHUNK + 1) * CHUNK
    thr = jnp.where(n_adm <= k_top, neg_inf, thr)

    tri = (lax.broadcasted_iota(jnp.int32, (KEY_BLOCK, KEY_BLOCK), 1)
           <= lax.broadcasted_iota(jnp.int32, (KEY_BLOCK, KEY_BLOCK), 0)).astype(BF16)

    def sel_body(kb, tie_carry):
        r0 = pl.multiple_of(kb * KEY_BLOCK, KEY_BLOCK)
        v = isc_ref[pl.ds(r0, KEY_BLOCK), :]
        tie = (v == thr).astype(BF16)
        rank = jnp.dot(tri, tie, preferred_element_type=F32) + tie_carry
        sel = ((v > thr) | ((v == thr) & (rank <= need))) & admissible(kb)
        bias_ref[pl.ds(r0, KEY_BLOCK), :] = jnp.where(sel, 0.0, neg_inf)
        return rank[KEY_BLOCK - 1:KEY_BLOCK, :]

    lax.fori_loop(0, nkb, sel_body, jnp.zeros((1, Q_TILE), F32))

    acc_ref[...] = jnp.zeros_like(acc_ref)
    m_ref[...] = jnp.full_like(m_ref, M_FLOOR)

    def scores_stage(kb, h):
        r0 = pl.multiple_of(kb * KEY_BLOCK, KEY_BLOCK)
        s = lax.dot_general(kh_ref[kb, h], q_ref[:, h * HEAD_DIM:(h + 1) * HEAD_DIM], _NT,
                            preferred_element_type=F32) + bias_ref[pl.ds(r0, KEY_BLOCK), :]
        r_ref[h % R_SLOTS][...] = s
        m_old = m_ref[h]
        m_blk = jnp.max(jnp.max(s.reshape(KEY_BLOCK // 8, 8, Q_TILE), axis=0), axis=0, keepdims=True)
        m_new = jnp.maximum(m_old, m_blk)
        m_ref[h] = m_new
        return m_new, jnp.exp2(m_old - m_new)

    first = scores_stage(0, 0) + scores_stage(0, 1)

    def attn_body(kb, carry):
        kb_next = jnp.minimum(kb + 1, nkb - 1)

        def pv_update(h, alpha):
            acc_ref[h] = alpha * acc_ref[h] + jnp.dot(vt_ref[kb, h], p_ref[h % P_SLOTS][...],
                                                      preferred_element_type=F32)

        stats = [carry[0:2], carry[2:4]]
        pending = None
        for h in range(N_HEADS):
            kb2, h2 = (kb, h + 2) if h + 2 < N_HEADS else (kb_next, h + 2 - N_HEADS)
            stats.append(scores_stage(kb2, h2))
            if pending is not None:
                pv_update(*pending)
            m_new, alpha = stats.pop(0)
            p_ref[h % P_SLOTS][...] = jnp.exp2(r_ref[h % R_SLOTS][...] - m_new).astype(BF16)
            pending = (h, alpha)
        pv_update(*pending)
        return stats[0] + stats[1]

    lax.fori_loop(0, nkb, attn_body, first)

    for h in range(N_HEADS):
        a = acc_ref[h]
        out_t = a[:HEAD_DIM, :] * (1.0 / a[HEAD_DIM:HEAD_DIM + 1, :])
        gate = ga_ref[:, h * HEAD_DIM:(h + 1) * HEAD_DIM].astype(F32)
        o_ref[:, h * HEAD_DIM:(h + 1) * HEAD_DIM] = (
            out_t.T * (gate * jax.nn.sigmoid(gate))).astype(o_ref.dtype)


def _sparse_attention(qqi, wixt, kix, kh, vt, gates, k_top):
    B, S, _ = qqi.shape
    DA = N_HEADS * HEAD_DIM
    DI = N_IDX_HEADS * IDX_DIM
    nq = S // Q_TILE
    return pl.pallas_call(
        functools.partial(_attn_kernel, k_top=k_top),
        grid=(B, nq),
        in_specs=[pl.BlockSpec((None, Q_TILE, DA), lambda b, i: (b, i, 0)),
                  pl.BlockSpec((None, Q_TILE, DI), lambda b, i: (b, i, DA // DI)),
                  pl.BlockSpec((None, N_IDX_HEADS, Q_TILE), lambda b, i: (b, 0, i)),
                  pl.BlockSpec((None, S, 2 * LANES), lambda b, i: (b, 0, 0)),
                  pl.BlockSpec((None,) + kh.shape[1:], lambda b, i: (b, 0, 0, 0, 0)),
                  pl.BlockSpec((None,) + vt.shape[1:], lambda b, i: (b, 0, 0, 0, 0),
                               pipeline_mode=pl.Buffered(1)),
                  pl.BlockSpec((Q_TILE, DA), lambda b, i: (b * nq + i, 0))],
        out_specs=pl.BlockSpec((Q_TILE, DA), lambda b, i: (b * nq + i, 0)),
        out_shape=jax.ShapeDtypeStruct((B * S, DA), BF16),
        scratch_shapes=[pltpu.VMEM((S, Q_TILE), F32),
                        pltpu.VMEM((S, Q_TILE), F32),
                        pltpu.VMEM((N_HEADS, V_ROWS, Q_TILE), F32),
                        pltpu.VMEM((N_HEADS, 1, Q_TILE), F32)]
                       + [pltpu.VMEM((KEY_BLOCK, Q_TILE), F32)] * R_SLOTS
                       + [pltpu.VMEM((KEY_BLOCK, Q_TILE), BF16)] * P_SLOTS,
        compiler_params=_cparams(("parallel", "arbitrary")),
        name="sparse_attention",
    )(qqi, qqi, wixt, kix, kh, vt, gates)


POOL_HALO = 16


def _pool_kernel(pin_ref, halo_ref, gate_ref, wp_ref, ps_ref, o_ref):
    i = pl.program_id(1)
    ts = pin_ref.shape[0]
    t = i * ts + lax.broadcasted_iota(jnp.int32, (ts, POOL_GROUP), 0)
    for g, win in enumerate(POOL_WINDOWS):
        cols = slice(g * POOL_GROUP, (g + 1) * POOL_GROUP)
        cur = pin_ref[:, cols].astype(F32)
        halo = jnp.where(i == 0, 0.0, halo_ref[:, cols].astype(F32))
        ssum = jnp.concatenate([halo, cur], axis=0)
        span = 1
        while span < win:
            ssum = ssum + pltpu.roll(ssum, span, 0)
            span *= 2
        count = jnp.minimum(t + 1, win).astype(F32)
        mixed = ssum[POOL_HALO:, :] / count - cur
        y = jnp.dot(mixed.astype(BF16), wp_ref[g], preferred_element_type=F32)
        gate = gate_ref[:, cols].astype(F32)
        o_ref[:, cols] = (y * ps_ref[:, cols] * (gate * jax.nn.sigmoid(gate))).astype(o_ref.dtype)


def _pool_mixer(gates, w_pool, pool_scale, B, S, ts=256):
    T = gates.shape[0]
    D = gates.shape[1] // 3
    nS = S // ts
    hb = ts // POOL_HALO
    return pl.pallas_call(
        _pool_kernel,
        grid=(B, nS),
        in_specs=[pl.BlockSpec((ts, D), lambda b, i: (b * nS + i, 1)),
                  pl.BlockSpec((POOL_HALO, D), lambda b, i: (jnp.maximum((b * nS + i) * hb - 1, 0), 1)),
                  pl.BlockSpec((ts, D), lambda b, i: (b * nS + i, 2)),
                  pl.BlockSpec(w_pool.shape, lambda b, i: (0, 0, 0)),
                  pl.BlockSpec((1, D), lambda b, i: (0, 0))],
        out_specs=pl.BlockSpec((ts, D), lambda b, i: (b * nS + i, 0)),
        out_shape=jax.ShapeDtypeStruct((T, D), BF16),
        compiler_params=_cparams(("parallel", "arbitrary")),
        name="pool_mixer",
    )(gates, gates, gates, w_pool, pool_scale.reshape(1, D))


OUT_COL_TILE = 512
OUT_PROJ_VMEM_LIMIT = 60 * 1024 * 1024


def _outproj_kernel(ya_ref, yb_ref, w_ref, x_ref, g_ref, o_ref):
    da = ya_ref.shape[1]
    tm, d = o_ref.shape
    n_col = d // OUT_COL_TILE
    ss = jnp.zeros((tm, LANES), F32)
    for jj in range(n_col):
        cols = slice(jj * OUT_COL_TILE, (jj + 1) * OUT_COL_TILE)
        y = (jnp.dot(ya_ref[...], w_ref[:da, cols], preferred_element_type=F32)
             + jnp.dot(yb_ref[...], w_ref[da:, cols], preferred_element_type=F32))
        o_ref[:, cols] = y
        for k in range(OUT_COL_TILE // LANES):
            yk = y[:, k * LANES:(k + 1) * LANES]
            ss = ss + yk * yk
    rs = lax.rsqrt(jnp.sum(ss, axis=-1, keepdims=True) * (1.0 / d) + NORM_EPS)
    for jj in range(n_col):
        cols = slice(jj * OUT_COL_TILE, (jj + 1) * OUT_COL_TILE)
        o_ref[:, cols] = x_ref[:, cols] + o_ref[:, cols] * rs * g_ref[:, cols]


def _out_proj(ya, yb, w_out_b, x2, post_g, tm=256):
    T, D = x2.shape
    DA = ya.shape[1]
    DB = yb.shape[1]
    return pl.pallas_call(
        _outproj_kernel,
        grid=(T // tm,),
        in_specs=[pl.BlockSpec((tm, DA), lambda i: (i, 0)),
                  pl.BlockSpec((tm, DB), lambda i: (i, 0)),
                  pl.BlockSpec((DA + DB, D), lambda i: (0, 0), pipeline_mode=pl.Buffered(1)),
                  pl.BlockSpec((tm, D), lambda i: (i, 0)),
                  pl.BlockSpec((1, D), lambda i: (0, 0))],
        out_specs=pl.BlockSpec((tm, D), lambda i: (i, 0)),
        out_shape=jax.ShapeDtypeStruct((T, D), F32),
        compiler_params=pltpu.CompilerParams(dimension_semantics=("parallel",),
                                             vmem_limit_bytes=OUT_PROJ_VMEM_LIMIT),
        name="out_proj",
    )(ya, yb, w_out_b, x2, post_g.reshape(1, D))


PROJ_TN = 512


def _layer(x2, B, S, pre_g, w_in, kv_g, w_uk, w_uv, ln_g, ln_b, w_pool, pool_scale, w_out, post_g, k_top):
    d_attn = N_HEADS * HEAD_DIM
    n_qi = N_IDX_HEADS * IDX_DIM
    o_c = d_attn
    o_qi = o_c + KV_LATENT
    o_k = o_qi + n_qi
    o_w = o_k + IDX_DIM
    o_g = o_w + N_IDX_HEADS
    K, n_in = w_in.shape
    wt = w_in.T
    w_uk_all = jnp.transpose(w_uk, (1, 0, 2)).reshape(KV_LATENT, d_attn).astype(BF16)
    w_uvt_all = jnp.swapaxes(w_uv, 1, 2).reshape(d_attn, KV_LATENT).astype(BF16)
    w_pool_b = w_pool.astype(BF16)
    w_out_b = w_out.astype(BF16)

    h = _prenorm(x2, pre_g)
    tn = PROJ_TN
    assert o_c % tn == 0 and o_qi % tn == 0 and n_qi % tn == 0
    qqi_spec = pl.BlockSpec((tn, K), lambda i, j: (jnp.where(j < o_c // tn, j, j + (o_qi - o_c) // tn), 0))
    qqi = _matmul_nt(h, wt, qqi_spec, d_attn + n_qi, BF16, 1024, tn, "proj_q_qi",
                     n_scaled=o_c // tn, scale=(HEAD_DIM ** -0.5) * LOG2E)
    assert o_g % 16 == 0
    gates_spec = pl.BlockSpec((pl.Element(tn), pl.Element(K)),
                              lambda i, j: (pl.multiple_of(o_g + j * tn, 16), 0))
    gates = _matmul_nt(h, wt, gates_spec, n_in - o_g, BF16, 1024, tn, "proj_gates")
    kh, vt, kix, wixt = _latent_proj(h, wt, o_c, o_k, kv_g, ln_g, ln_b, w_uk_all, w_uvt_all, B, S)
    ya = _sparse_attention(qqi.reshape(B, S, d_attn + n_qi), wixt, kix, kh, vt, gates, k_top)
    yb = _pool_mixer(gates, w_pool_b, pool_scale, B, S)
    return _out_proj(ya, yb, w_out_b, x2, post_g)


def kernel(x, pre_norm, w_in, kv_norm, w_uk, w_uv, idx_k_norm_g, idx_k_norm_b, w_pool, pool_scale, w_out, post_norm):
    B, S, D = x.shape
    k_top = min(TOPK_MAX, S // 4)
    x2 = x.reshape(B * S, D)
    for l in range(pre_norm.shape[0]):
        x2 = _layer(x2, B, S, pre_norm[l], w_in[l], kv_norm[l], w_uk[l], w_uv[l], idx_k_norm_g[l],
                    idx_k_norm_b[l], w_pool[l], pool_scale[l], w_out[l], post_norm[l], k_top)
    return x2.reshape(B, S, D)
```

```python
import functools
import math

import jax
import jax.numpy as jnp
from jax import lax
from jax.experimental import pallas as pl
from jax.experimental.pallas import tpu as pltpu

F32 = jnp.float32
BF16 = jnp.bfloat16

NORM_EPS = 1e-6
CHUNK = 64
N_HEADS = 16
HEAD_DIM = 128
KV_LATENT = 512
N_IDX_HEADS = 16
IDX_DIM = 64
TOPK_MAX = 256
POOL_WINDOWS = (2, 4, 8, 16)
POOL_GROUP = 512

LANES = 128
Q_TILE = 256
KEY_BLOCK = 256
BF16_SUBLANES = 16
V_ROWS = HEAD_DIM + BF16_SUBLANES
R_SLOTS = 4
P_SLOTS = 2
assert KEY_BLOCK % Q_TILE == 0
M_FLOOR = -1e30
LOG2E = math.log2(math.e)
VMEM_LIMIT = 56 * 1024 * 1024


def _cparams(sem):
    return pltpu.CompilerParams(dimension_semantics=sem, vmem_limit_bytes=VMEM_LIMIT)


def _prenorm_kernel(x_ref, g_ref, o_ref):
    x = x_ref[...]
    ms = jnp.mean(x * x, axis=-1, keepdims=True)
    o_ref[...] = (x * lax.rsqrt(ms + NORM_EPS) * g_ref[...]).astype(o_ref.dtype)


def _prenorm(x2, g, tm=256):
    T, D = x2.shape
    return pl.pallas_call(
        _prenorm_kernel,
        grid=(T // tm,),
        in_specs=[pl.BlockSpec((tm, D), lambda i: (i, 0)),
                  pl.BlockSpec((1, D), lambda i: (0, 0))],
        out_specs=pl.BlockSpec((tm, D), lambda i: (i, 0)),
        out_shape=jax.ShapeDtypeStruct((T, D), BF16),
        compiler_params=_cparams(("parallel",)),
        name="prenorm",
    )(x2, g.reshape(1, D))


_NT = (((1,), (1,)), ((), ()))


def _matmul_nt_kernel(h_ref, wt_ref, o_ref, *, n_scaled, scale):
    z = lax.dot_general(h_ref[...], wt_ref[...].astype(BF16), _NT, preferred_element_type=F32)
    if n_scaled:
        z = z * jnp.where(pl.program_id(1) < n_scaled, jnp.float32(scale), jnp.float32(1.0))
    o_ref[...] = z.astype(o_ref.dtype)


def _matmul_nt(h, wt, w_spec, n_out, out_dtype, tm, tn, name, n_scaled=0, scale=1.0):
    T, K = h.shape
    tm = min(tm, T)
    return pl.pallas_call(
        functools.partial(_matmul_nt_kernel, n_scaled=n_scaled, scale=scale),
        grid=(T // tm, n_out // tn),
        in_specs=[pl.BlockSpec((tm, K), lambda i, j: (i, 0)), w_spec],
        out_specs=pl.BlockSpec((tm, tn), lambda i, j: (i, j)),
        out_shape=jax.ShapeDtypeStruct((T, n_out), out_dtype),
        compiler_params=_cparams(("parallel", "arbitrary")),
        name=name,
    )(h, wt)


def _latent_kernel(h_ref, wc_ref, wkw_ref, kvg_ref, lng_ref, lnb_ref, wuk_ref, wuvt_ref,
                   kh_ref, vt_ref, kix_ref, wixt_ref):
    h = h_ref[...]
    c = lax.dot_general(h, wc_ref[...].astype(BF16), _NT, preferred_element_type=F32)
    cn = c * lax.rsqrt(jnp.mean(c * c, axis=-1, keepdims=True) + NORM_EPS) * kvg_ref[...]
    cb = cn.astype(BF16)
    n_kb = kh_ref.shape[0]
    for pair in range(N_HEADS // 2):
        k2 = jnp.dot(cb, wuk_ref[:, pair * 2 * HEAD_DIM:(pair + 1) * 2 * HEAD_DIM],
                     preferred_element_type=F32)
        for u in range(n_kb):
            for e in range(2):
                kh_ref[u, 2 * pair + e] = k2[u * KEY_BLOCK:(u + 1) * KEY_BLOCK,
                                             e * HEAD_DIM:(e + 1) * HEAD_DIM].astype(kh_ref.dtype)
    for u in range(n_kb):
        ct_u = cn[u * KEY_BLOCK:(u + 1) * KEY_BLOCK, :].T.astype(BF16)
        v_all = jnp.dot(wuvt_ref[...], ct_u, preferred_element_type=F32)
        for hh in range(N_HEADS):
            vt_ref[u, hh, :HEAD_DIM, :] = v_all[hh * HEAD_DIM:(hh + 1) * HEAD_DIM, :].astype(vt_ref.dtype)
            vt_ref[u, hh, HEAD_DIM:, :] = jnp.ones((V_ROWS - HEAD_DIM, KEY_BLOCK), vt_ref.dtype)
    kw = lax.dot_general(h, wkw_ref[...].astype(BF16), _NT, preferred_element_type=F32)
    lane = lax.broadcasted_iota(jnp.int32, kw.shape, 1)
    is_k = lane < IDX_DIM
    mu = jnp.sum(jnp.where(is_k, kw, 0.0), axis=-1, keepdims=True) * (1.0 / IDX_DIM)
    d = jnp.where(is_k, kw - mu, 0.0)
    var = jnp.sum(d * d, axis=-1, keepdims=True) * (1.0 / IDX_DIM)
    kn = d * lax.rsqrt(var + NORM_EPS) * lng_ref[...] + lnb_ref[...]
    kix_ref[...] = jnp.concatenate([kn, pltpu.roll(kn, IDX_DIM, 1)], axis=1).astype(kix_ref.dtype)
    wixt_ref[...] = kw.T[IDX_DIM:IDX_DIM + N_IDX_HEADS, :] * (N_IDX_HEADS ** -0.5)


def _latent_proj(h, wt, o_c, o_k, kv_norm, ln_g, ln_b, w_uk_all, w_uvt_all, B, S, tm=512):
    T, K = h.shape
    nS = S // tm
    pad = jnp.zeros((LANES - IDX_DIM,), F32)
    lng = jnp.concatenate([ln_g, pad]).reshape(1, LANES)
    lnb = jnp.concatenate([ln_b, pad]).reshape(1, LANES)
    assert o_c % KV_LATENT == 0 and o_k % LANES == 0
    return pl.pallas_call(
        _latent_kernel,
        grid=(B, nS),
        in_specs=[pl.BlockSpec((tm, K), lambda b, i: (b * nS + i, 0)),
                  pl.BlockSpec((KV_LATENT, K), lambda b, i: (o_c // KV_LATENT, 0)),
                  pl.BlockSpec((LANES, K), lambda b, i: (o_k // LANES, 0)),
                  pl.BlockSpec((1, KV_LATENT), lambda b, i: (0, 0)),
                  pl.BlockSpec((1, LANES), lambda b, i: (0, 0)),
                  pl.BlockSpec((1, LANES), lambda b, i: (0, 0)),
                  pl.BlockSpec(w_uk_all.shape, lambda b, i: (0, 0)),
                  pl.BlockSpec(w_uvt_all.shape, lambda b, i: (0, 0))],
        out_specs=[pl.BlockSpec((None, tm // KEY_BLOCK, N_HEADS, KEY_BLOCK, HEAD_DIM),
                                lambda b, i: (b, i, 0, 0, 0)),
                   pl.BlockSpec((None, tm // KEY_BLOCK, N_HEADS, V_ROWS, KEY_BLOCK),
                                lambda b, i: (b, i, 0, 0, 0)),
                   pl.BlockSpec((None, tm, 2 * LANES), lambda b, i: (b, i, 0)),
                   pl.BlockSpec((None, N_IDX_HEADS, tm), lambda b, i: (b, 0, i))],
        out_shape=[jax.ShapeDtypeStruct((B, S // KEY_BLOCK, N_HEADS, KEY_BLOCK, HEAD_DIM), BF16),
                   jax.ShapeDtypeStruct((B, S // KEY_BLOCK, N_HEADS, V_ROWS, KEY_BLOCK), BF16),
                   jax.ShapeDtypeStruct((B, S, 2 * LANES), BF16),
                   jax.ShapeDtypeStruct((B, N_IDX_HEADS, S), F32)],
        compiler_params=_cparams(("parallel", "arbitrary")),
        name="latent_proj",
    )(h, wt, wt, kv_norm.reshape(1, KV_LATENT), lng, lnb, w_uk_all, w_uvt_all)


def _sortable_to_f32(key):
    bits = key ^ ((key >> 31) & jnp.int32(0x7FFFFFFF))
    return pltpu.bitcast(bits, F32)


def _fold8(a):
    return jnp.sum(a.reshape(a.shape[0] // 8, 8, a.shape[1]), axis=0)


def _attn_kernel(q_ref, qi_ref, wixt_ref, kix_ref, kh_ref, vt_ref, ga_ref,
                 o_ref, isc_ref, bias_ref, acc_ref, m_ref, *stage_refs, k_top):
    r_ref, p_ref = stage_refs[:R_SLOTS], stage_refs[R_SLOTS:]
    i = pl.program_id(1)
    t0 = i * Q_TILE
    nkb = (t0 + Q_TILE + KEY_BLOCK - 1) // KEY_BLOCK
    neg_inf = jnp.float32(-jnp.inf)

    t_idx = t0 + lax.broadcasted_iota(jnp.int32, (KEY_BLOCK, Q_TILE), 1)
    q_chunk = t_idx // CHUNK

    def admissible(kb):
        s_idx = kb * KEY_BLOCK + lax.broadcasted_iota(jnp.int32, (KEY_BLOCK, Q_TILE), 0)
        return (s_idx // CHUNK) <= q_chunk

    wix = wixt_ref[...] * (IDX_DIM ** -0.5)

    def idx_body(kb, carry):
        r0 = pl.multiple_of(kb * KEY_BLOCK, KEY_BLOCK)
        kx = kix_ref[pl.ds(r0, KEY_BLOCK), :]
        kx_even, kx_odd = kx[:, :LANES], kx[:, LANES:]
        acc = jnp.zeros((KEY_BLOCK, Q_TILE), F32)
        for pair in range(N_IDX_HEADS // 2):
            rhs = qi_ref[:, pair * LANES:(pair + 1) * LANES]
            for par, kxp in ((0, kx_even), (1, kx_odd)):
                h = 2 * pair + par
                d = lax.dot_general(kxp, rhs, (((1,), (1,)), ((), ())), preferred_element_type=F32)
                acc = acc + wix[h:h + 1, :] * jnp.maximum(d, 0.0)
        isc_ref[pl.ds(r0, KEY_BLOCK), :] = acc
        return carry

    lax.fori_loop(0, nkb, idx_body, 0)
    diag_r0 = pl.multiple_of((nkb - 1) * KEY_BLOCK, KEY_BLOCK)
    isc_ref[pl.ds(diag_r0, KEY_BLOCK), :] = jnp.where(admissible(nkb - 1),
                                                      isc_ref[pl.ds(diag_r0, KEY_BLOCK), :], neg_inf)

    def count(pred):
        def body(kb, cnt):
            r0 = pl.multiple_of(kb * KEY_BLOCK, KEY_BLOCK)
            v = isc_ref[pl.ds(r0, KEY_BLOCK), :]
            return cnt + _fold8(jnp.where(pred(v), 1, 0).astype(jnp.int32))
        c8 = lax.fori_loop(0, nkb, body, jnp.zeros((8, Q_TILE), jnp.int32))
        return jnp.sum(c8, axis=0, keepdims=True)

    int_min = jnp.int32(-2 ** 31)

    def search_body(step, carry):
        key, key_cnt = carry
        bit = 31 - step
        inc = jnp.where(bit == 31, int_min, jnp.left_shift(jnp.int32(1), jnp.minimum(bit, 30)))
        cand = key + inc
        n_ge = count(lambda v: v >= _sortable_to_f32(cand))
        take = n_ge >= k_top
        return jnp.where(take, cand, key), jnp.where(take, n_ge, key_cnt)

    n_steps = jnp.where(t0 + Q_TILE <= k_top, 0, 32)
    key, key_cnt = lax.fori_loop(0, n_steps, search_body,
                                 (jnp.full((1, Q_TILE), int_min, jnp.int32),
                                  jnp.full((1, Q_TILE), k_top, jnp.int32)))
    n_adm = (t_idx[0:1, :] // CHUNK + 1) * CHUNK
    take_all = n_adm <= k_top
    thr = jnp.where(take_all, neg_inf, _sortable_to_f32(key))
    has_ties = jnp.max(jnp.where(take_all, 0, key_cnt - k_top)) > 0

    @pl.when(jnp.logical_not(has_ties))
    def _():
        def sel_body(kb, carry):
            r0 = pl.multiple_of(kb * KEY_BLOCK, KEY_BLOCK)
            bias_ref[pl.ds(r0, KEY_BLOCK), :] = jnp.where(isc_ref[pl.ds(r0, KEY_BLOCK), :] >= thr,
                                                          0.0, neg_inf)
            return carry

        lax.fori_loop(0, nkb - 1, sel_body, 0)
        sel = (isc_ref[pl.ds(diag_r0, KEY_BLOCK), :] >= thr) & admissible(nkb - 1)
        bias_ref[pl.ds(diag_r0, KEY_BLOCK), :] = jnp.where(sel, 0.0, neg_inf)

    @pl.when(has_ties)
    def _():
        n_gt = count(lambda v: v > thr)
        need = (k_top - n_gt).astype(F32)
        tri = (lax.broadcasted_iota(jnp.int32, (KEY_BLOCK, KEY_BLOCK), 1)
               <= lax.broadcasted_iota(jnp.int32, (KEY_BLOCK, KEY_BLOCK), 0)).astype(BF16)

        def sel_body(kb, tie_carry):
            r0 = pl.multiple_of(kb * KEY_BLOCK, KEY_BLOCK)
            v = isc_ref[pl.ds(r0, KEY_BLOCK), :]
            tie = (v == thr).astype(BF16)
            rank = jnp.dot(tri, tie, preferred_element_type=F32) + tie_carry
            sel = ((v > thr) | ((v == thr) & (rank <= need))) & admissible(kb)
            bias_ref[pl.ds(r0, KEY_BLOCK), :] = jnp.where(sel, 0.0, neg_inf)
            return rank[KEY_BLOCK - 1:KEY_BLOCK, :]

        lax.fori_loop(0, nkb, sel_body, jnp.zeros((1, Q_TILE), F32))

    acc_ref[...] = jnp.zeros_like(acc_ref)
    m_ref[...] = jnp.full_like(m_ref, M_FLOOR)

    def scores_stage(kb, h):
        r0 = pl.multiple_of(kb * KEY_BLOCK, KEY_BLOCK)
        s = lax.dot_general(kh_ref[kb, h], q_ref[:, h * HEAD_DIM:(h + 1) * HEAD_DIM], _NT,
                            preferred_element_type=F32) + bias_ref[pl.ds(r0, KEY_BLOCK), :]
        r_ref[h % R_SLOTS][...] = s
        m_old = m_ref[h]
        m_blk = jnp.max(jnp.max(s.reshape(KEY_BLOCK // 8, 8, Q_TILE), axis=0), axis=0, keepdims=True)
        m_new = jnp.maximum(m_old, m_blk)
        m_ref[h] = m_new
        return m_new, jnp.exp2(m_old - m_new)

    first = scores_stage(0, 0) + scores_stage(0, 1)

    def attn_body(kb, carry):
        kb_next = jnp.minimum(kb + 1, nkb - 1)

        def pv_update(h, alpha):
            acc_ref[h] = alpha * acc_ref[h] + jnp.dot(vt_ref[kb, h], p_ref[h % P_SLOTS][...],
                                                      preferred_element_type=F32)

        stats = [carry[0:2], carry[2:4]]
        pending = None
        for h in range(N_HEADS):
            kb2, h2 = (kb, h + 2) if h + 2 < N_HEADS else (kb_next, h + 2 - N_HEADS)
            stats.append(scores_stage(kb2, h2))
            if pending is not None:
                pv_update(*pending)
            m_new, alpha = stats.pop(0)
            p_ref[h % P_SLOTS][...] = jnp.exp2(r_ref[h % R_SLOTS][...] - m_new).astype(BF16)
            pending = (h, alpha)
        pv_update(*pending)
        return stats[0] + stats[1]

    lax.fori_loop(0, nkb, attn_body, first)

    for h in range(N_HEADS):
        a = acc_ref[h]
        out_t = a[:HEAD_DIM, :] * (1.0 / a[HEAD_DIM:HEAD_DIM + 1, :])
        gate = ga_ref[:, h * HEAD_DIM:(h + 1) * HEAD_DIM].astype(F32)
        o_ref[:, h * HEAD_DIM:(h + 1) * HEAD_DIM] = (
            out_t.T * (gate * jax.nn.sigmoid(gate))).astype(o_ref.dtype)


def _sparse_attention(qqi, wixt, kix, kh, vt, gates, k_top):
    B, S, _ = qqi.shape
    DA = N_HEADS * HEAD_DIM
    DI = N_IDX_HEADS * IDX_DIM
    nq = S // Q_TILE
    return pl.pallas_call(
        functools.partial(_attn_kernel, k_top=k_top),
        grid=(B, nq),
        in_specs=[pl.BlockSpec((None, Q_TILE, DA), lambda b, i: (b, i, 0)),
                  pl.BlockSpec((None, Q_TILE, DI), lambda b, i: (b, i, DA // DI)),
                  pl.BlockSpec((None, N_IDX_HEADS, Q_TILE), lambda b, i: (b, 0, i)),
                  pl.BlockSpec((None, S, 2 * LANES), lambda b, i: (b, 0, 0)),
                  pl.BlockSpec((None,) + kh.shape[1:], lambda b, i: (b, 0, 0, 0, 0)),
                  pl.BlockSpec((None,) + vt.shape[1:], lambda b, i: (b, 0, 0, 0, 0),
                               pipeline_mode=pl.Buffered(1)),
                  pl.BlockSpec((Q_TILE, DA), lambda b, i: (b * nq + i, 0))],
        out_specs=pl.BlockSpec((Q_TILE, DA), lambda b, i: (b * nq + i, 0)),
        out_shape=jax.ShapeDtypeStruct((B * S, DA), BF16),
        scratch_shapes=[pltpu.VMEM((S, Q_TILE), F32),
                        pltpu.VMEM((S, Q_TILE), F32),
                        pltpu.VMEM((N_HEADS, V_ROWS, Q_TILE), F32),
                        pltpu.VMEM((N_HEADS, 1, Q_TILE), F32)]
                       + [pltpu.VMEM((KEY_BLOCK, Q_TILE), F32)] * R_SLOTS
                       + [pltpu.VMEM((KEY_BLOCK, Q_TILE), BF16)] * P_SLOTS,
        compiler_params=_cparams(("parallel", "arbitrary")),
        name="sparse_attention",
    )(qqi, qqi, wixt, kix, kh, vt, gates)


POOL_HALO = 16


def _pool_kernel(pin_ref, halo_ref, gate_ref, wp_ref, ps_ref, o_ref):
    i = pl.program_id(1)
    ts = pin_ref.shape[0]
    t = i * ts + lax.broadcasted_iota(jnp.int32, (ts, POOL_GROUP), 0)
    for g, win in enumerate(POOL_WINDOWS):
        cols = slice(g * POOL_GROUP, (g + 1) * POOL_GROUP)
        cur = pin_ref[:, cols].astype(F32)
        halo = jnp.where(i == 0, 0.0, halo_ref[:, cols].astype(F32))
        ssum = jnp.concatenate([halo, cur], axis=0)
        span = 1
        while span < win:
            ssum = ssum + pltpu.roll(ssum, span, 0)
            span *= 2
        count = jnp.minimum(t + 1, win).astype(F32)
        mixed = ssum[POOL_HALO:, :] / count - cur
        y = jnp.dot(mixed.astype(BF16), wp_ref[g], preferred_element_type=F32)
        gate = gate_ref[:, cols].astype(F32)
        o_ref[:, cols] = (y * ps_ref[:, cols] * (gate * jax.nn.sigmoid(gate))).astype(o_ref.dtype)


def _pool_mixer(gates, w_pool, pool_scale, B, S, ts=256):
    T = gates.shape[0]
    D = gates.shape[1] // 3
    nS = S // ts
    hb = ts // POOL_HALO
    return pl.pallas_call(
        _pool_kernel,
        grid=(B, nS),
        in_specs=[pl.BlockSpec((ts, D), lambda b, i: (b * nS + i, 1)),
                  pl.BlockSpec((POOL_HALO, D), lambda b, i: (jnp.maximum((b * nS + i) * hb - 1, 0), 1)),
                  pl.BlockSpec((ts, D), lambda b, i: (b * nS + i, 2)),
                  pl.BlockSpec(w_pool.shape, lambda b, i: (0, 0, 0)),
                  pl.BlockSpec((1, D), lambda b, i: (0, 0))],
        out_specs=pl.BlockSpec((ts, D), lambda b, i: (b * nS + i, 0)),
        out_shape=jax.ShapeDtypeStruct((T, D), BF16),
        compiler_params=_cparams(("parallel", "arbitrary")),
        name="pool_mixer",
    )(gates, gates, gates, w_pool, pool_scale.reshape(1, D))


OUT_COL_TILE = 512
OUT_PROJ_VMEM_LIMIT = 60 * 1024 * 1024


def _outproj_kernel(ya_ref, yb_ref, w_ref, x_ref, g_ref, o_ref):
    da = ya_ref.shape[1]
    tm, d = o_ref.shape
    n_col = d // OUT_COL_TILE
    ss = jnp.zeros((tm, LANES), F32)
    for jj in range(n_col):
        cols = slice(jj * OUT_COL_TILE, (jj + 1) * OUT_COL_TILE)
        y = (jnp.dot(ya_ref[...], w_ref[:da, cols], preferred_element_type=F32)
             + jnp.dot(yb_ref[...], w_ref[da:, cols], preferred_element_type=F32))
        o_ref[:, cols] = y
        for k in range(OUT_COL_TILE // LANES):
            yk = y[:, k * LANES:(k + 1) * LANES]
            ss = ss + yk * yk
    rs = lax.rsqrt(jnp.sum(ss, axis=-1, keepdims=True) * (1.0 / d) + NORM_EPS)
    for jj in range(n_col):
        cols = slice(jj * OUT_COL_TILE, (jj + 1) * OUT_COL_TILE)
        o_ref[:, cols] = x_ref[:, cols] + o_ref[:, cols] * rs * g_ref[:, cols]


def _out_proj(ya, yb, w_out_b, x2, post_g, tm=256):
    T, D = x2.shape
    DA = ya.shape[1]
    DB = yb.shape[1]
    return pl.pallas_call(
        _outproj_kernel,
        grid=(T // tm,),
        in_specs=[pl.BlockSpec((tm, DA), lambda i: (i, 0)),
                  pl.BlockSpec((tm, DB), lambda i: (i, 0)),
                  pl.BlockSpec((DA + DB, D), lambda i: (0, 0), pipeline_mode=pl.Buffered(1)),
                  pl.BlockSpec((tm, D), lambda i: (i, 0)),
                  pl.BlockSpec((1, D), lambda i: (0, 0))],
        out_specs=pl.BlockSpec((tm, D), lambda i: (i, 0)),
        out_shape=jax.ShapeDtypeStruct((T, D), F32),
        compiler_params=pltpu.CompilerParams(dimension_semantics=("parallel",),
                                             vmem_limit_bytes=OUT_PROJ_VMEM_LIMIT),
        name="out_proj",
    )(ya, yb, w_out_b, x2, post_g.reshape(1, D))


PROJ_TN = 512


def _layer(x2, B, S, pre_g, w_in, kv_g, w_uk, w_uv, ln_g, ln_b, w_pool, pool_scale, w_out, post_g, k_top):
    d_attn = N_HEADS * HEAD_DIM
    n_qi = N_IDX_HEADS * IDX_DIM
    o_c = d_attn
    o_qi = o_c + KV_LATENT
    o_k = o_qi + n_qi
    o_w = o_k + IDX_DIM
    o_g = o_w + N_IDX_HEADS
    K, n_in = w_in.shape
    wt = w_in.T
    w_uk_all = jnp.transpose(w_uk, (1, 0, 2)).reshape(KV_LATENT, d_attn).astype(BF16)
    w_uvt_all = jnp.swapaxes(w_uv, 1, 2).reshape(d_attn, KV_LATENT).astype(BF16)
    w_pool_b = w_pool.astype(BF16)
    w_out_b = w_out.astype(BF16)

    h = _prenorm(x2, pre_g)
    tn = PROJ_TN
    assert o_c % tn == 0 and o_qi % tn == 0 and n_qi % tn == 0
    qqi_spec = pl.BlockSpec((tn, K), lambda i, j: (jnp.where(j < o_c // tn, j, j + (o_qi - o_c) // tn), 0))
    qqi = _matmul_nt(h, wt, qqi_spec, d_attn + n_qi, BF16, 1024, tn, "proj_q_qi",
                     n_scaled=o_c // tn, scale=(HEAD_DIM ** -0.5) * LOG2E)
    assert o_g % 16 == 0
    gates_spec = pl.BlockSpec((pl.Element(tn), pl.Element(K)),
                              lambda i, j: (pl.multiple_of(o_g + j * tn, 16), 0))
    gates = _matmul_nt(h, wt, gates_spec, n_in - o_g, BF16, 1024, tn, "proj_gates")
    kh, vt, kix, wixt = _latent_proj(h, wt, o_c, o_k, kv_g, ln_g, ln_b, w_uk_all, w_uvt_all, B, S)
    ya = _sparse_attention(qqi.reshape(B, S, d_attn + n_qi), wixt, kix, kh, vt, gates, k_top)
    yb = _pool_mixer(gates, w_pool_b, pool_scale, B, S)
    return _out_proj(ya, yb, w_out_b, x2, post_g)


def kernel(x, pre_norm, w_in, kv_norm, w_uk, w_uv, idx_k_norm_g, idx_k_norm_b, w_pool, pool_scale, w_out, post_norm):
    B, S, D = x.shape
    k_top = min(TOPK_MAX, S // 4)
    x2 = x.reshape(B * S, D)
    for l in range(pre_norm.shape[0]):
        x2 = _layer(x2, B, S, pre_norm[l], w_in[l], kv_norm[l], w_uk[l], w_uv[l], idx_k_norm_g[l],
                    idx_k_norm_b[l], w_pool[l], pool_scale[l], w_out[l], post_norm[l], k_top)
    return x2.reshape(B, S, D)
```

```python
import functools
import math

import jax
import jax.numpy as jnp
from jax import lax
from jax.experimental import pallas as pl
from jax.experimental.pallas import tpu as pltpu

F32 = jnp.float32
BF16 = jnp.bfloat16

NORM_EPS = 1e-6
CHUNK = 64
N_HEADS = 16
HEAD_DIM = 128
KV_LATENT = 512
N_IDX_HEADS = 16
IDX_DIM = 64
TOPK_MAX = 256
POOL_WINDOWS = (2, 4, 8, 16)
POOL_GROUP = 512

LANES = 128
Q_TILE = 256
KEY_BLOCK = 256
BF16_SUBLANES = 16
V_ROWS = HEAD_DIM + BF16_SUBLANES
R_SLOTS = 4
P_SLOTS = 2
assert KEY_BLOCK % Q_TILE == 0
HALF_BITS = 16
FINE_BITS = 18
M_FLOOR = -1e30
LOG2E = math.log2(math.e)
VMEM_LIMIT = 56 * 1024 * 1024


def _cparams(sem):
    return pltpu.CompilerParams(dimension_semantics=sem, vmem_limit_bytes=VMEM_LIMIT)


def _prenorm_kernel(x_ref, g_ref, o_ref):
    x = x_ref[...]
    ms = jnp.mean(x * x, axis=-1, keepdims=True)
    o_ref[...] = (x * lax.rsqrt(ms + NORM_EPS) * g_ref[...]).astype(o_ref.dtype)


def _prenorm(x2, g, tm=256):
    T, D = x2.shape
    return pl.pallas_call(
        _prenorm_kernel,
        grid=(T // tm,),
        in_specs=[pl.BlockSpec((tm, D), lambda i: (i, 0)),
                  pl.BlockSpec((1, D), lambda i: (0, 0))],
        out_specs=pl.BlockSpec((tm, D), lambda i: (i, 0)),
        out_shape=jax.ShapeDtypeStruct((T, D), BF16),
        compiler_params=_cparams(("parallel",)),
        name="prenorm",
    )(x2, g.reshape(1, D))


_NT = (((1,), (1,)), ((), ()))


def _matmul_nt_kernel(h_ref, wt_ref, o_ref, *, n_scaled, scale):
    z = lax.dot_general(h_ref[...], wt_ref[...].astype(BF16), _NT, preferred_element_type=F32)
    if n_scaled:
        z = z * jnp.where(pl.program_id(1) < n_scaled, jnp.float32(scale), jnp.float32(1.0))
    o_ref[...] = z.astype(o_ref.dtype)


def _matmul_nt(h, wt, w_spec, n_out, out_dtype, tm, tn, name, n_scaled=0, scale=1.0):
    T, K = h.shape
    tm = min(tm, T)
    return pl.pallas_call(
        functools.partial(_matmul_nt_kernel, n_scaled=n_scaled, scale=scale),
        grid=(T // tm, n_out // tn),
        in_specs=[pl.BlockSpec((tm, K), lambda i, j: (i, 0)), w_spec],
        out_specs=pl.BlockSpec((tm, tn), lambda i, j: (i, j)),
        out_shape=jax.ShapeDtypeStruct((T, n_out), out_dtype),
        compiler_params=_cparams(("parallel", "arbitrary")),
        name=name,
    )(h, wt)


def _latent_kernel(h_ref, wc_ref, wkw_ref, kvg_ref, lng_ref, lnb_ref, wuk_ref, wuvt_ref,
                   kh_ref, vt_ref, kix_ref, wixt_ref):
    h = h_ref[...]
    c = lax.dot_general(h, wc_ref[...].astype(BF16), _NT, preferred_element_type=F32)
    cn = c * lax.rsqrt(jnp.mean(c * c, axis=-1, keepdims=True) + NORM_EPS) * kvg_ref[...]
    cb = cn.astype(BF16)
    n_kb = kh_ref.shape[0]
    for pair in range(N_HEADS // 2):
        k2 = jnp.dot(cb, wuk_ref[:, pair * 2 * HEAD_DIM:(pair + 1) * 2 * HEAD_DIM],
                     preferred_element_type=F32)
        for u in range(n_kb):
            for e in range(2):
                kh_ref[u, 2 * pair + e] = k2[u * KEY_BLOCK:(u + 1) * KEY_BLOCK,
                                             e * HEAD_DIM:(e + 1) * HEAD_DIM].astype(kh_ref.dtype)
    for u in range(n_kb):
        ct_u = cn[u * KEY_BLOCK:(u + 1) * KEY_BLOCK, :].T.astype(BF16)
        v_all = jnp.dot(wuvt_ref[...], ct_u, preferred_element_type=F32)
        for hh in range(N_HEADS):
            vt_ref[u, hh, :HEAD_DIM, :] = v_all[hh * HEAD_DIM:(hh + 1) * HEAD_DIM, :].astype(vt_ref.dtype)
            vt_ref[u, hh, HEAD_DIM:, :] = jnp.ones((V_ROWS - HEAD_DIM, KEY_BLOCK), vt_ref.dtype)
    kw = lax.dot_general(h, wkw_ref[...].astype(BF16), _NT, preferred_element_type=F32)
    lane = lax.broadcasted_iota(jnp.int32, kw.shape, 1)
    is_k = lane < IDX_DIM
    mu = jnp.sum(jnp.where(is_k, kw, 0.0), axis=-1, keepdims=True) * (1.0 / IDX_DIM)
    d = jnp.where(is_k, kw - mu, 0.0)
    var = jnp.sum(d * d, axis=-1, keepdims=True) * (1.0 / IDX_DIM)
    kn = d * lax.rsqrt(var + NORM_EPS) * lng_ref[...] + lnb_ref[...]
    kix_ref[...] = jnp.concatenate([kn, pltpu.roll(kn, IDX_DIM, 1)], axis=1).astype(kix_ref.dtype)
    wixt_ref[...] = kw.T[IDX_DIM:IDX_DIM + N_IDX_HEADS, :] * (N_IDX_HEADS ** -0.5)


def _latent_proj(h, wt, o_c, o_k, kv_norm, ln_g, ln_b, w_uk_all, w_uvt_all, B, S, tm=512):
    T, K = h.shape
    nS = S // tm
    pad = jnp.zeros((LANES - IDX_DIM,), F32)
    lng = jnp.concatenate([ln_g, pad]).reshape(1, LANES)
    lnb = jnp.concatenate([ln_b, pad]).reshape(1, LANES)
    assert o_c % KV_LATENT == 0 and o_k % LANES == 0
    return pl.pallas_call(
        _latent_kernel,
        grid=(B, nS),
        in_specs=[pl.BlockSpec((tm, K), lambda b, i: (b * nS + i, 0)),
                  pl.BlockSpec((KV_LATENT, K), lambda b, i: (o_c // KV_LATENT, 0)),
                  pl.BlockSpec((LANES, K), lambda b, i: (o_k // LANES, 0)),
                  pl.BlockSpec((1, KV_LATENT), lambda b, i: (0, 0)),
                  pl.BlockSpec((1, LANES), lambda b, i: (0, 0)),
                  pl.BlockSpec((1, LANES), lambda b, i: (0, 0)),
                  pl.BlockSpec(w_uk_all.shape, lambda b, i: (0, 0)),
                  pl.BlockSpec(w_uvt_all.shape, lambda b, i: (0, 0))],
        out_specs=[pl.BlockSpec((None, tm // KEY_BLOCK, N_HEADS, KEY_BLOCK, HEAD_DIM),
                                lambda b, i: (b, i, 0, 0, 0)),
                   pl.BlockSpec((None, tm // KEY_BLOCK, N_HEADS, V_ROWS, KEY_BLOCK),
                                lambda b, i: (b, i, 0, 0, 0)),
                   pl.BlockSpec((None, tm, 2 * LANES), lambda b, i: (b, i, 0)),
                   pl.BlockSpec((None, N_IDX_HEADS, tm), lambda b, i: (b, 0, i))],
        out_shape=[jax.ShapeDtypeStruct((B, S // KEY_BLOCK, N_HEADS, KEY_BLOCK, HEAD_DIM), BF16),
                   jax.ShapeDtypeStruct((B, S // KEY_BLOCK, N_HEADS, V_ROWS, KEY_BLOCK), BF16),
                   jax.ShapeDtypeStruct((B, S, 2 * LANES), BF16),
                   jax.ShapeDtypeStruct((B, N_IDX_HEADS, S), F32)],
        compiler_params=_cparams(("parallel", "arbitrary")),
        name="latent_proj",
    )(h, wt, wt, kv_norm.reshape(1, KV_LATENT), lng, lnb, w_uk_all, w_uvt_all)


def _sortable_to_f32(key):
    bits = key ^ ((key >> 31) & jnp.int32(0x7FFFFFFF))
    return pltpu.bitcast(bits, F32)


def _fold8(a):
    return jnp.sum(a.reshape(a.shape[0] // 8, 8, a.shape[1]), axis=0)


def _attn_kernel(q_ref, qi_ref, wixt_ref, kix_ref, kh_ref, vt_ref, ga_ref,
                 o_ref, isc_ref, hsc_ref, bias_ref, acc_ref, m_ref, *stage_refs, k_top):
    r_ref, p_ref = stage_refs[:R_SLOTS], stage_refs[R_SLOTS:]
    i = pl.program_id(1)
    t0 = i * Q_TILE
    nkb = (t0 + Q_TILE + KEY_BLOCK - 1) // KEY_BLOCK
    neg_inf = jnp.float32(-jnp.inf)

    t_idx = t0 + lax.broadcasted_iota(jnp.int32, (KEY_BLOCK, Q_TILE), 1)
    q_chunk = t_idx // CHUNK

    def admissible(kb):
        s_idx = kb * KEY_BLOCK + lax.broadcasted_iota(jnp.int32, (KEY_BLOCK, Q_TILE), 0)
        return (s_idx // CHUNK) <= q_chunk

    wix = wixt_ref[...] * (IDX_DIM ** -0.5)

    def idx_body(kb, carry):
        r0 = pl.multiple_of(kb * KEY_BLOCK, KEY_BLOCK)
        kx = kix_ref[pl.ds(r0, KEY_BLOCK), :]
        kx_even, kx_odd = kx[:, :LANES], kx[:, LANES:]
        acc = jnp.zeros((KEY_BLOCK, Q_TILE), F32)
        for pair in range(N_IDX_HEADS // 2):
            rhs = qi_ref[:, pair * LANES:(pair + 1) * LANES]
            for par, kxp in ((0, kx_even), (1, kx_odd)):
                h = 2 * pair + par
                d = lax.dot_general(kxp, rhs, (((1,), (1,)), ((), ())), preferred_element_type=F32)
                acc = acc + wix[h:h + 1, :] * jnp.maximum(d, 0.0)
        isc_ref[pl.ds(r0, KEY_BLOCK), :] = acc
        hsc_ref[pl.ds(r0, KEY_BLOCK), :] = acc.astype(BF16)
        return carry

    lax.fori_loop(0, nkb, idx_body, 0)
    diag_r0 = pl.multiple_of((nkb - 1) * KEY_BLOCK, KEY_BLOCK)
    diag = jnp.where(admissible(nkb - 1), isc_ref[pl.ds(diag_r0, KEY_BLOCK), :], neg_inf)
    isc_ref[pl.ds(diag_r0, KEY_BLOCK), :] = diag
    hsc_ref[pl.ds(diag_r0, KEY_BLOCK), :] = diag.astype(BF16)

    def count(pred):
        def body(kb, cnt):
            r0 = pl.multiple_of(kb * KEY_BLOCK, KEY_BLOCK)
            v = isc_ref[pl.ds(r0, KEY_BLOCK), :]
            return cnt + _fold8(jnp.where(pred(v), 1, 0).astype(jnp.int32))
        c8 = lax.fori_loop(0, nkb, body, jnp.zeros((8, Q_TILE), jnp.int32))
        return jnp.sum(c8, axis=0, keepdims=True)

    def count_half(cand):
        one, zero = jnp.ones((), BF16), jnp.zeros((), BF16)

        def body(kb, cnt):
            r0 = pl.multiple_of(kb * KEY_BLOCK, KEY_BLOCK)
            ind = jnp.where(hsc_ref[pl.ds(r0, KEY_BLOCK), :] >= cand, one, zero)
            parts = [ind[k * BF16_SUBLANES:(k + 1) * BF16_SUBLANES, :]
                     for k in range(KEY_BLOCK // BF16_SUBLANES)]
            while len(parts) > 1:
                parts = [a + b for a, b in zip(parts[0::2], parts[1::2])]
            return cnt + parts[0].astype(F32)

        c16 = lax.fori_loop(0, nkb, body, jnp.zeros((BF16_SUBLANES, Q_TILE), F32))
        return jnp.sum(c16, axis=0, keepdims=True)

    def half_to_bf16(k16):
        bits = k16 ^ ((k16 >> 31) & jnp.int32(0x7FFF))
        return pltpu.bitcast(jnp.left_shift(bits, 16), F32).astype(BF16)

    def coarse_body(step, k16):
        cand = k16 + jnp.left_shift(jnp.int32(1), HALF_BITS - 1 - step)
        return jnp.where(count_half(half_to_bf16(cand)) >= k_top, cand, k16)

    def fine_body(step, carry):
        key, key_cnt = carry
        cand = key + jnp.left_shift(jnp.int32(1), FINE_BITS - 1 - step)
        n_ge = count(lambda v: v >= _sortable_to_f32(cand))
        take = n_ge >= k_top
        return jnp.where(take, cand, key), jnp.where(take, n_ge, key_cnt)

    searching = t0 + Q_TILE > k_top
    k16 = lax.fori_loop(0, jnp.where(searching, HALF_BITS, 0), coarse_body,
                        jnp.full((1, Q_TILE), -2 ** (HALF_BITS - 1), jnp.int32))
    key_lo = jnp.left_shift(k16, 16) - jnp.int32(1 << 16)
    key, key_cnt = lax.fori_loop(0, jnp.where(searching, FINE_BITS, 0), fine_body,
                                 (key_lo, jnp.full((1, Q_TILE), k_top + 1, jnp.int32)))
    n_adm = (t_idx[0:1, :] // CHUNK + 1) * CHUNK
    take_all = n_adm <= k_top
    thr = jnp.where(take_all, neg_inf, _sortable_to_f32(key))
    has_ties = jnp.max(jnp.where(take_all, 0, key_cnt - k_top)) > 0

    @pl.when(jnp.logical_not(has_ties))
    def _():
        def sel_body(kb, carry):
            r0 = pl.multiple_of(kb * KEY_BLOCK, KEY_BLOCK)
            bias_ref[pl.ds(r0, KEY_BLOCK), :] = jnp.where(isc_ref[pl.ds(r0, KEY_BLOCK), :] >= thr,
                                                          0.0, neg_inf)
            return carry

        lax.fori_loop(0, nkb - 1, sel_body, 0)
        sel = (isc_ref[pl.ds(diag_r0, KEY_BLOCK), :] >= thr) & admissible(nkb - 1)
        bias_ref[pl.ds(diag_r0, KEY_BLOCK), :] = jnp.where(sel, 0.0, neg_inf)

    @pl.when(has_ties)
    def _():
        n_gt = count(lambda v: v > thr)
        need = (k_top - n_gt).astype(F32)
        tri = (lax.broadcasted_iota(jnp.int32, (KEY_BLOCK, KEY_BLOCK), 1)
               <= lax.broadcasted_iota(jnp.int32, (KEY_BLOCK, KEY_BLOCK), 0)).astype(BF16)

        def sel_body(kb, tie_carry):
            r0 = pl.multiple_of(kb * KEY_BLOCK, KEY_BLOCK)
            v = isc_ref[pl.ds(r0, KEY_BLOCK), :]
            tie = (v == thr).astype(BF16)
            rank = jnp.dot(tri, tie, preferred_element_type=F32) + tie_carry
            sel = ((v > thr) | ((v == thr) & (rank <= need))) & admissible(kb)
            bias_ref[pl.ds(r0, KEY_BLOCK), :] = jnp.where(sel, 0.0, neg_inf)
            return rank[KEY_BLOCK - 1:KEY_BLOCK, :]

        lax.fori_loop(0, nkb, sel_body, jnp.zeros((1, Q_TILE), F32))

    acc_ref[...] = jnp.zeros_like(acc_ref)
    m_ref[...] = jnp.full_like(m_ref, M_FLOOR)

    def scores_stage(kb, h):
        r0 = pl.multiple_of(kb * KEY_BLOCK, KEY_BLOCK)
        s = lax.dot_general(kh_ref[kb, h], q_ref[:, h * HEAD_DIM:(h + 1) * HEAD_DIM], _NT,
                            preferred_element_type=F32) + bias_ref[pl.ds(r0, KEY_BLOCK), :]
        r_ref[h % R_SLOTS][...] = s
        m_old = m_ref[h]
        m_blk = jnp.max(jnp.max(s.reshape(KEY_BLOCK // 8, 8, Q_TILE), axis=0), axis=0, keepdims=True)
        m_new = jnp.maximum(m_old, m_blk)
        m_ref[h] = m_new
        return m_new, jnp.exp2(m_old - m_new)

    first = scores_stage(0, 0) + scores_stage(0, 1)

    def attn_body(kb, carry):
        kb_next = jnp.minimum(kb + 1, nkb - 1)

        def pv_update(h, alpha):
            acc_ref[h] = alpha * acc_ref[h] + jnp.dot(vt_ref[kb, h], p_ref[h % P_SLOTS][...],
                                                      preferred_element_type=F32)

        stats = [carry[0:2], carry[2:4]]
        pending = None
        for h in range(N_HEADS):
            kb2, h2 = (kb, h + 2) if h + 2 < N_HEADS else (kb_next, h + 2 - N_HEADS)
            stats.append(scores_stage(kb2, h2))
            if pending is not None:
                pv_update(*pending)
            m_new, alpha = stats.pop(0)
            p_ref[h % P_SLOTS][...] = jnp.exp2(r_ref[h % R_SLOTS][...] - m_new).astype(BF16)
            pending = (h, alpha)
        pv_update(*pending)
        return stats[0] + stats[1]

    lax.fori_loop(0, nkb, attn_body, first)

    for h in range(N_HEADS):
        a = acc_ref[h]
        out_t = a[:HEAD_DIM, :] * (1.0 / a[HEAD_DIM:HEAD_DIM + 1, :])
        gate = ga_ref[:, h * HEAD_DIM:(h + 1) * HEAD_DIM].astype(F32)
        o_ref[:, h * HEAD_DIM:(h + 1) * HEAD_DIM] = (
            out_t.T * (gate * jax.nn.sigmoid(gate))).astype(o_ref.dtype)


def _sparse_attention(qqi, wixt, kix, kh, vt, gates, k_top):
    B, S, _ = qqi.shape
    DA = N_HEADS * HEAD_DIM
    DI = N_IDX_HEADS * IDX_DIM
    nq = S // Q_TILE
    return pl.pallas_call(
        functools.partial(_attn_kernel, k_top=k_top),
        grid=(B, nq),
        in_specs=[pl.BlockSpec((None, Q_TILE, DA), lambda b, i: (b, i, 0)),
                  pl.BlockSpec((None, Q_TILE, DI), lambda b, i: (b, i, DA // DI)),
                  pl.BlockSpec((None, N_IDX_HEADS, Q_TILE), lambda b, i: (b, 0, i)),
                  pl.BlockSpec((None, S, 2 * LANES), lambda b, i: (b, 0, 0)),
                  pl.BlockSpec((None,) + kh.shape[1:], lambda b, i: (b, 0, 0, 0, 0)),
                  pl.BlockSpec((None,) + vt.shape[1:], lambda b, i: (b, 0, 0, 0, 0),
                               pipeline_mode=pl.Buffered(1)),
                  pl.BlockSpec((Q_TILE, DA), lambda b, i: (b * nq + i, 0))],
        out_specs=pl.BlockSpec((Q_TILE, DA), lambda b, i: (b * nq + i, 0)),
        out_shape=jax.ShapeDtypeStruct((B * S, DA), BF16),
        scratch_shapes=[pltpu.VMEM((S, Q_TILE), F32),
                        pltpu.VMEM((S, Q_TILE), BF16),
                        pltpu.VMEM((S, Q_TILE), F32),
                        pltpu.VMEM((N_HEADS, V_ROWS, Q_TILE), F32),
                        pltpu.VMEM((N_HEADS, 1, Q_TILE), F32)]
                       + [pltpu.VMEM((KEY_BLOCK, Q_TILE), F32)] * R_SLOTS
                       + [pltpu.VMEM((KEY_BLOCK, Q_TILE), BF16)] * P_SLOTS,
        compiler_params=_cparams(("parallel", "arbitrary")),
        name="sparse_attention",
    )(qqi, qqi, wixt, kix, kh, vt, gates)


POOL_HALO = 16


def _pool_kernel(pin_ref, halo_ref, gate_ref, wp_ref, ps_ref, o_ref):
    i = pl.program_id(1)
    ts = pin_ref.shape[0]
    t = i * ts + lax.broadcasted_iota(jnp.int32, (ts, POOL_GROUP), 0)
    for g, win in enumerate(POOL_WINDOWS):
        cols = slice(g * POOL_GROUP, (g + 1) * POOL_GROUP)
        cur = pin_ref[:, cols].astype(F32)
        halo = jnp.where(i == 0, 0.0, halo_ref[:, cols].astype(F32))
        ssum = jnp.concatenate([halo, cur], axis=0)
        span = 1
        while span < win:
            ssum = ssum + pltpu.roll(ssum, span, 0)
            span *= 2
        count = jnp.minimum(t + 1, win).astype(F32)
        mixed = ssum[POOL_HALO:, :] / count - cur
        y = jnp.dot(mixed.astype(BF16), wp_ref[g], preferred_element_type=F32)
        gate = gate_ref[:, cols].astype(F32)
        o_ref[:, cols] = (y * ps_ref[:, cols] * (gate * jax.nn.sigmoid(gate))).astype(o_ref.dtype)


def _pool_mixer(gates, w_pool, pool_scale, B, S, ts=256):
    T = gates.shape[0]
    D = gates.shape[1] // 3
    nS = S // ts
    hb = ts // POOL_HALO
    return pl.pallas_call(
        _pool_kernel,
        grid=(B, nS),
        in_specs=[pl.BlockSpec((ts, D), lambda b, i: (b * nS + i, 1)),
                  pl.BlockSpec((POOL_HALO, D), lambda b, i: (jnp.maximum((b * nS + i) * hb - 1, 0), 1)),
                  pl.BlockSpec((ts, D), lambda b, i: (b * nS + i, 2)),
                  pl.BlockSpec(w_pool.shape, lambda b, i: (0, 0, 0)),
                  pl.BlockSpec((1, D), lambda b, i: (0, 0))],
        out_specs=pl.BlockSpec((ts, D), lambda b, i: (b * nS + i, 0)),
        out_shape=jax.ShapeDtypeStruct((T, D), BF16),
        compiler_params=_cparams(("parallel", "arbitrary")),
        name="pool_mixer",
    )(gates, gates, gates, w_pool, pool_scale.reshape(1, D))


OUT_COL_TILE = 512
OUT_PROJ_VMEM_LIMIT = 60 * 1024 * 1024


def _outproj_kernel(ya_ref, yb_ref, w_ref, x_ref, g_ref, o_ref):
    da = ya_ref.shape[1]
    tm, d = o_ref.shape
    n_col = d // OUT_COL_TILE
    ss = jnp.zeros((tm, LANES), F32)
    for jj in range(n_col):
        cols = slice(jj * OUT_COL_TILE, (jj + 1) * OUT_COL_TILE)
        y = (jnp.dot(ya_ref[...], w_ref[:da, cols], preferred_element_type=F32)
             + jnp.dot(yb_ref[...], w_ref[da:, cols], preferred_element_type=F32))
        o_ref[:, cols] = y
        for k in range(OUT_COL_TILE // LANES):
            yk = y[:, k * LANES:(k + 1) * LANES]
            ss = ss + yk * yk
    rs = lax.rsqrt(jnp.sum(ss, axis=-1, keepdims=True) * (1.0 / d) + NORM_EPS)
    for jj in range(n_col):
        cols = slice(jj * OUT_COL_TILE, (jj + 1) * OUT_COL_TILE)
        o_ref[:, cols] = x_ref[:, cols] + o_ref[:, cols] * rs * g_ref[:, cols]


def _out_proj(ya, yb, w_out_b, x2, post_g, tm=256):
    T, D = x2.shape
    DA = ya.shape[1]
    DB = yb.shape[1]
    return pl.pallas_call(
        _outproj_kernel,
        grid=(T // tm,),
        in_specs=[pl.BlockSpec((tm, DA), lambda i: (i, 0)),
                  pl.BlockSpec((tm, DB), lambda i: (i, 0)),
                  pl.BlockSpec((DA + DB, D), lambda i: (0, 0), pipeline_mode=pl.Buffered(1)),
                  pl.BlockSpec((tm, D), lambda i: (i, 0)),
                  pl.BlockSpec((1, D), lambda i: (0, 0))],
        out_specs=pl.BlockSpec((tm, D), lambda i: (i, 0)),
        out_shape=jax.ShapeDtypeStruct((T, D), F32),
        compiler_params=pltpu.CompilerParams(dimension_semantics=("parallel",),
                                             vmem_limit_bytes=OUT_PROJ_VMEM_LIMIT),
        name="out_proj",
    )(ya, yb, w_out_b, x2, post_g.reshape(1, D))


PROJ_TN = 512


def _layer(x2, B, S, pre_g, w_in, kv_g, w_uk, w_uv, ln_g, ln_b, w_pool, pool_scale, w_out, post_g, k_top):
    d_attn = N_HEADS * HEAD_DIM
    n_qi = N_IDX_HEADS * IDX_DIM
    o_c = d_attn
    o_qi = o_c + KV_LATENT
    o_k = o_qi + n_qi
    o_w = o_k + IDX_DIM
    o_g = o_w + N_IDX_HEADS
    K, n_in = w_in.shape
    wt = w_in.T
    w_uk_all = jnp.transpose(w_uk, (1, 0, 2)).reshape(KV_LATENT, d_attn).astype(BF16)
    w_uvt_all = jnp.swapaxes(w_uv, 1, 2).reshape(d_attn, KV_LATENT).astype(BF16)
    w_pool_b = w_pool.astype(BF16)
    w_out_b = w_out.astype(BF16)

    h = _prenorm(x2, pre_g)
    tn = PROJ_TN
    assert o_c % tn == 0 and o_qi % tn == 0 and n_qi % tn == 0
    qqi_spec = pl.BlockSpec((tn, K), lambda i, j: (jnp.where(j < o_c // tn, j, j + (o_qi - o_c) // tn), 0))
    qqi = _matmul_nt(h, wt, qqi_spec, d_attn + n_qi, BF16, 1024, tn, "proj_q_qi",
                     n_scaled=o_c // tn, scale=(HEAD_DIM ** -0.5) * LOG2E)
    assert o_g % 16 == 0
    gates_spec = pl.BlockSpec((pl.Element(tn), pl.Element(K)),
                              lambda i, j: (pl.multiple_of(o_g + j * tn, 16), 0))
    gates = _matmul_nt(h, wt, gates_spec, n_in - o_g, BF16, 1024, tn, "proj_gates")
    kh, vt, kix, wixt = _latent_proj(h, wt, o_c, o_k, kv_g, ln_g, ln_b, w_uk_all, w_uvt_all, B, S)
    ya = _sparse_attention(qqi.reshape(B, S, d_attn + n_qi), wixt, kix, kh, vt, gates, k_top)
    yb = _pool_mixer(gates, w_pool_b, pool_scale, B, S)
    return _out_proj(ya, yb, w_out_b, x2, post_g)


def kernel(x, pre_norm, w_in, kv_norm, w_uk, w_uv, idx_k_norm_g, idx_k_norm_b, w_pool, pool_scale, w_out, post_norm):
    B, S, D = x.shape
    k_top = min(TOPK_MAX, S // 4)
    x2 = x.reshape(B * S, D)
    for l in range(pre_norm.shape[0]):
        x2 = _layer(x2, B, S, pre_norm[l], w_in[l], kv_norm[l], w_uk[l], w_uv[l], idx_k_norm_g[l],
                    idx_k_norm_b[l], w_pool[l], pool_scale[l], w_out[l], post_norm[l], k_top)
    return x2.reshape(B, S, D)
```

```python
import functools
import math

import jax
import jax.numpy as jnp
from jax import lax
from jax.experimental import pallas as pl
from jax.experimental.pallas import tpu as pltpu

F32 = jnp.float32
BF16 = jnp.bfloat16

NORM_EPS = 1e-6
CHUNK = 64
N_HEADS = 16
HEAD_DIM = 128
KV_LATENT = 512
N_IDX_HEADS = 16
IDX_DIM = 64
TOPK_MAX = 256
POOL_WINDOWS = (2, 4, 8, 16)
POOL_GROUP = 512

LANES = 128
Q_TILE = 256
KEY_BLOCK = 256
BF16_SUBLANES = 16
V_ROWS = HEAD_DIM + BF16_SUBLANES
R_SLOTS = 4
P_SLOTS = 2
assert KEY_BLOCK % Q_TILE == 0
HALF_BITS = 16
FINE_BITS = 18
M_FLOOR = -1e30
LOG2E = math.log2(math.e)
VMEM_LIMIT = 56 * 1024 * 1024


def _cparams(sem):
    return pltpu.CompilerParams(dimension_semantics=sem, vmem_limit_bytes=VMEM_LIMIT)


def _prenorm_kernel(x_ref, g_ref, o_ref):
    x = x_ref[...]
    ms = jnp.mean(x * x, axis=-1, keepdims=True)
    o_ref[...] = (x * lax.rsqrt(ms + NORM_EPS) * g_ref[...]).astype(o_ref.dtype)


def _prenorm(x2, g, tm=256):
    T, D = x2.shape
    return pl.pallas_call(
        _prenorm_kernel,
        grid=(T // tm,),
        in_specs=[pl.BlockSpec((tm, D), lambda i: (i, 0)),
                  pl.BlockSpec((1, D), lambda i: (0, 0))],
        out_specs=pl.BlockSpec((tm, D), lambda i: (i, 0)),
        out_shape=jax.ShapeDtypeStruct((T, D), BF16),
        compiler_params=_cparams(("parallel",)),
        name="prenorm",
    )(x2, g.reshape(1, D))


_NT = (((1,), (1,)), ((), ()))


def _matmul_nt_kernel(h_ref, wt_ref, *rest, n_scaled, scale):
    o_ref = rest[-1] if len(rest) == 1 else rest[1]
    z = lax.dot_general(h_ref[...], wt_ref[...].astype(BF16), _NT, preferred_element_type=F32)
    if n_scaled:
        z = z * jnp.where(pl.program_id(1) < n_scaled, jnp.float32(scale), jnp.float32(1.0))
    o_ref[...] = z.astype(o_ref.dtype)
    if len(rest) == 3:
        side_ref, _, side_out_ref = rest
        side_out_ref[...] = side_ref[...].astype(side_out_ref.dtype)


def _matmul_nt(h, wt, w_spec, n_out, out_dtype, tm, tn, name, n_scaled=0, scale=1.0, side=None):
    T, K = h.shape
    tm = min(tm, T)
    n_i, n_j = T // tm, n_out // tn
    in_specs = [pl.BlockSpec((tm, K), lambda i, j: (i, 0)), w_spec]
    out_specs = [pl.BlockSpec((tm, tn), lambda i, j: (i, j))]
    out_shape = [jax.ShapeDtypeStruct((T, n_out), out_dtype)]
    args = [h, wt]
    if side is not None:
        R, C = side.shape
        nb = 1
        while nb * 2 <= min(n_i * n_j, R // BF16_SUBLANES):
            nb *= 2
        assert R % nb == 0
        side_map = lambda i, j: (jnp.minimum(i * n_j + j, nb - 1), 0)
        in_specs.append(pl.BlockSpec((R // nb, C), side_map))
        out_specs.append(pl.BlockSpec((R // nb, C), side_map))
        out_shape.append(jax.ShapeDtypeStruct((R, C), BF16))
        args.append(side)
    res = pl.pallas_call(
        functools.partial(_matmul_nt_kernel, n_scaled=n_scaled, scale=scale),
        grid=(n_i, n_j),
        in_specs=in_specs,
        out_specs=out_specs,
        out_shape=out_shape,
        compiler_params=_cparams(("parallel", "arbitrary")),
        name=name,
    )(*args)
    return res if side is not None else res[0]


def _latent_kernel(h_ref, wc_ref, wkw_ref, kvg_ref, lng_ref, lnb_ref, wuk_ref, wuvt_ref,
                   kh_ref, vt_ref, kix_ref, wixt_ref):
    h = h_ref[...]
    c = lax.dot_general(h, wc_ref[...].astype(BF16), _NT, preferred_element_type=F32)
    cn = c * lax.rsqrt(jnp.mean(c * c, axis=-1, keepdims=True) + NORM_EPS) * kvg_ref[...]
    cb = cn.astype(BF16)
    n_kb = kh_ref.shape[0]
    for pair in range(N_HEADS // 2):
        k2 = jnp.dot(cb, wuk_ref[:, pair * 2 * HEAD_DIM:(pair + 1) * 2 * HEAD_DIM],
                     preferred_element_type=F32)
        for u in range(n_kb):
            for e in range(2):
                kh_ref[u, 2 * pair + e] = k2[u * KEY_BLOCK:(u + 1) * KEY_BLOCK,
                                             e * HEAD_DIM:(e + 1) * HEAD_DIM].astype(kh_ref.dtype)
    for u in range(n_kb):
        ct_u = cn[u * KEY_BLOCK:(u + 1) * KEY_BLOCK, :].T.astype(BF16)
        v_all = jnp.dot(wuvt_ref[...], ct_u, preferred_element_type=F32)
        for hh in range(N_HEADS):
            vt_ref[u, hh, :HEAD_DIM, :] = v_all[hh * HEAD_DIM:(hh + 1) * HEAD_DIM, :].astype(vt_ref.dtype)
            vt_ref[u, hh, HEAD_DIM:, :] = jnp.ones((V_ROWS - HEAD_DIM, KEY_BLOCK), vt_ref.dtype)
    kw = lax.dot_general(h, wkw_ref[...].astype(BF16), _NT, preferred_element_type=F32)
    lane = lax.broadcasted_iota(jnp.int32, kw.shape, 1)
    is_k = lane < IDX_DIM
    mu = jnp.sum(jnp.where(is_k, kw, 0.0), axis=-1, keepdims=True) * (1.0 / IDX_DIM)
    d = jnp.where(is_k, kw - mu, 0.0)
    var = jnp.sum(d * d, axis=-1, keepdims=True) * (1.0 / IDX_DIM)
    kn = d * lax.rsqrt(var + NORM_EPS) * lng_ref[...] + lnb_ref[...]
    kix_ref[...] = jnp.concatenate([kn, pltpu.roll(kn, IDX_DIM, 1)], axis=1).astype(kix_ref.dtype)
    wixt_ref[...] = kw.T[IDX_DIM:IDX_DIM + N_IDX_HEADS, :] * (N_IDX_HEADS ** -0.5)


def _latent_proj(h, wt, o_c, o_k, kv_norm, ln_g, ln_b, w_uk_all, w_uvt_all, B, S, tm=512):
    T, K = h.shape
    nS = S // tm
    pad = jnp.zeros((LANES - IDX_DIM,), F32)
    lng = jnp.concatenate([ln_g, pad]).reshape(1, LANES)
    lnb = jnp.concatenate([ln_b, pad]).reshape(1, LANES)
    assert o_c % KV_LATENT == 0 and o_k % LANES == 0
    return pl.pallas_call(
        _latent_kernel,
        grid=(B, nS),
        in_specs=[pl.BlockSpec((tm, K), lambda b, i: (b * nS + i, 0)),
                  pl.BlockSpec((KV_LATENT, K), lambda b, i: (o_c // KV_LATENT, 0)),
                  pl.BlockSpec((LANES, K), lambda b, i: (o_k // LANES, 0)),
                  pl.BlockSpec((1, KV_LATENT), lambda b, i: (0, 0)),
                  pl.BlockSpec((1, LANES), lambda b, i: (0, 0)),
                  pl.BlockSpec((1, LANES), lambda b, i: (0, 0)),
                  pl.BlockSpec(w_uk_all.shape, lambda b, i: (0, 0)),
                  pl.BlockSpec(w_uvt_all.shape, lambda b, i: (0, 0))],
        out_specs=[pl.BlockSpec((None, tm // KEY_BLOCK, N_HEADS, KEY_BLOCK, HEAD_DIM),
                                lambda b, i: (b, i, 0, 0, 0)),
                   pl.BlockSpec((None, tm // KEY_BLOCK, N_HEADS, V_ROWS, KEY_BLOCK),
                                lambda b, i: (b, i, 0, 0, 0)),
                   pl.BlockSpec((None, tm, 2 * LANES), lambda b, i: (b, i, 0)),
                   pl.BlockSpec((None, N_IDX_HEADS, tm), lambda b, i: (b, 0, i))],
        out_shape=[jax.ShapeDtypeStruct((B, S // KEY_BLOCK, N_HEADS, KEY_BLOCK, HEAD_DIM), BF16),
                   jax.ShapeDtypeStruct((B, S // KEY_BLOCK, N_HEADS, V_ROWS, KEY_BLOCK), BF16),
                   jax.ShapeDtypeStruct((B, S, 2 * LANES), BF16),
                   jax.ShapeDtypeStruct((B, N_IDX_HEADS, S), F32)],
        compiler_params=_cparams(("parallel", "arbitrary")),
        name="latent_proj",
    )(h, wt, wt, kv_norm.reshape(1, KV_LATENT), lng, lnb, w_uk_all, w_uvt_all)


def _sortable_to_f32(key):
    bits = key ^ ((key >> 31) & jnp.int32(0x7FFFFFFF))
    return pltpu.bitcast(bits, F32)


def _fold8(a):
    return jnp.sum(a.reshape(a.shape[0] // 8, 8, a.shape[1]), axis=0)


def _attn_kernel(q_ref, qi_ref, wixt_ref, kix_ref, kh_ref, vt_ref, ga_ref,
                 o_ref, isc_ref, hsc_ref, bias_ref, acc_ref, m_ref, *stage_refs, k_top):
    r_ref, p_ref = stage_refs[:R_SLOTS], stage_refs[R_SLOTS:]
    i = pl.program_id(1)
    t0 = i * Q_TILE
    nkb = (t0 + Q_TILE + KEY_BLOCK - 1) // KEY_BLOCK
    neg_inf = jnp.float32(-jnp.inf)

    t_idx = t0 + lax.broadcasted_iota(jnp.int32, (KEY_BLOCK, Q_TILE), 1)
    q_chunk = t_idx // CHUNK

    def admissible(kb):
        s_idx = kb * KEY_BLOCK + lax.broadcasted_iota(jnp.int32, (KEY_BLOCK, Q_TILE), 0)
        return (s_idx // CHUNK) <= q_chunk

    wix = wixt_ref[...] * (IDX_DIM ** -0.5)

    def idx_body(kb, carry):
        r0 = pl.multiple_of(kb * KEY_BLOCK, KEY_BLOCK)
        kx = kix_ref[pl.ds(r0, KEY_BLOCK), :]
        kx_even, kx_odd = kx[:, :LANES], kx[:, LANES:]
        acc = jnp.zeros((KEY_BLOCK, Q_TILE), F32)
        for pair in range(N_IDX_HEADS // 2):
            rhs = qi_ref[:, pair * LANES:(pair + 1) * LANES]
            for par, kxp in ((0, kx_even), (1, kx_odd)):
                h = 2 * pair + par
                d = lax.dot_general(kxp, rhs, (((1,), (1,)), ((), ())), preferred_element_type=F32)
                acc = acc + wix[h:h + 1, :] * jnp.maximum(d, 0.0)
        isc_ref[pl.ds(r0, KEY_BLOCK), :] = acc
        hsc_ref[pl.ds(r0, KEY_BLOCK), :] = acc.astype(BF16)
        return carry

    lax.fori_loop(0, nkb, idx_body, 0)
    diag_r0 = pl.multiple_of((nkb - 1) * KEY_BLOCK, KEY_BLOCK)
    diag = jnp.where(admissible(nkb - 1), isc_ref[pl.ds(diag_r0, KEY_BLOCK), :], neg_inf)
    isc_ref[pl.ds(diag_r0, KEY_BLOCK), :] = diag
    hsc_ref[pl.ds(diag_r0, KEY_BLOCK), :] = diag.astype(BF16)

    def count(pred):
        def body(kb, cnt):
            r0 = pl.multiple_of(kb * KEY_BLOCK, KEY_BLOCK)
            v = isc_ref[pl.ds(r0, KEY_BLOCK), :]
            return cnt + _fold8(jnp.where(pred(v), 1, 0).astype(jnp.int32))
        c8 = lax.fori_loop(0, nkb, body, jnp.zeros((8, Q_TILE), jnp.int32))
        return jnp.sum(c8, axis=0, keepdims=True)

    def count_half(cand):
        one, zero = jnp.ones((), BF16), jnp.zeros((), BF16)

        def body(kb, cnt):
            r0 = pl.multiple_of(kb * KEY_BLOCK, KEY_BLOCK)
            ind = jnp.where(hsc_ref[pl.ds(r0, KEY_BLOCK), :] >= cand, one, zero)
            parts = [ind[k * BF16_SUBLANES:(k + 1) * BF16_SUBLANES, :]
                     for k in range(KEY_BLOCK // BF16_SUBLANES)]
            while len(parts) > 1:
                parts = [a + b for a, b in zip(parts[0::2], parts[1::2])]
            return cnt + parts[0].astype(F32)

        c16 = lax.fori_loop(0, nkb, body, jnp.zeros((BF16_SUBLANES, Q_TILE), F32))
        return jnp.sum(c16, axis=0, keepdims=True)

    def half_to_bf16(k16):
        bits = k16 ^ ((k16 >> 31) & jnp.int32(0x7FFF))
        return pltpu.bitcast(jnp.left_shift(bits, 16), F32).astype(BF16)

    def coarse_body(step, k16):
        cand = k16 + jnp.left_shift(jnp.int32(1), HALF_BITS - 1 - step)
        return jnp.where(count_half(half_to_bf16(cand)) >= k_top, cand, k16)

    def fine_body(step, carry):
        key, key_cnt = carry
        cand = key + jnp.left_shift(jnp.int32(1), FINE_BITS - 1 - step)
        n_ge = count(lambda v: v >= _sortable_to_f32(cand))
        take = n_ge >= k_top
        return jnp.where(take, cand, key), jnp.where(take, n_ge, key_cnt)

    searching = t0 + Q_TILE > k_top
    k16 = lax.fori_loop(0, jnp.where(searching, HALF_BITS, 0), coarse_body,
                        jnp.full((1, Q_TILE), -2 ** (HALF_BITS - 1), jnp.int32))
    key_lo = jnp.left_shift(k16, 16) - jnp.int32(1 << 16)
    key, key_cnt = lax.fori_loop(0, jnp.where(searching, FINE_BITS, 0), fine_body,
                                 (key_lo, jnp.full((1, Q_TILE), k_top + 1, jnp.int32)))
    n_adm = (t_idx[0:1, :] // CHUNK + 1) * CHUNK
    take_all = n_adm <= k_top
    thr = jnp.where(take_all, neg_inf, _sortable_to_f32(key))
    has_ties = jnp.max(jnp.where(take_all, 0, key_cnt - k_top)) > 0

    @pl.when(jnp.logical_not(has_ties))
    def _():
        def sel_body(kb, carry):
            r0 = pl.multiple_of(kb * KEY_BLOCK, KEY_BLOCK)
            bias_ref[pl.ds(r0, KEY_BLOCK), :] = jnp.where(isc_ref[pl.ds(r0, KEY_BLOCK), :] >= thr,
                                                          0.0, neg_inf)
            return carry

        lax.fori_loop(0, nkb - 1, sel_body, 0)
        sel = (isc_ref[pl.ds(diag_r0, KEY_BLOCK), :] >= thr) & admissible(nkb - 1)
        bias_ref[pl.ds(diag_r0, KEY_BLOCK), :] = jnp.where(sel, 0.0, neg_inf)

    @pl.when(has_ties)
    def _():
        n_gt = count(lambda v: v > thr)
        need = (k_top - n_gt).astype(F32)
        tri = (lax.broadcasted_iota(jnp.int32, (KEY_BLOCK, KEY_BLOCK), 1)
               <= lax.broadcasted_iota(jnp.int32, (KEY_BLOCK, KEY_BLOCK), 0)).astype(BF16)

        def sel_body(kb, tie_carry):
            r0 = pl.multiple_of(kb * KEY_BLOCK, KEY_BLOCK)
            v = isc_ref[pl.ds(r0, KEY_BLOCK), :]
            tie = (v == thr).astype(BF16)
            rank = jnp.dot(tri, tie, preferred_element_type=F32) + tie_carry
            sel = ((v > thr) | ((v == thr) & (rank <= need))) & admissible(kb)
            bias_ref[pl.ds(r0, KEY_BLOCK), :] = jnp.where(sel, 0.0, neg_inf)
            return rank[KEY_BLOCK - 1:KEY_BLOCK, :]

        lax.fori_loop(0, nkb, sel_body, jnp.zeros((1, Q_TILE), F32))

    acc_ref[...] = jnp.zeros_like(acc_ref)
    m_ref[...] = jnp.full_like(m_ref, M_FLOOR)

    def scores_stage(kb, h):
        r0 = pl.multiple_of(kb * KEY_BLOCK, KEY_BLOCK)
        s = lax.dot_general(kh_ref[kb, h], q_ref[:, h * HEAD_DIM:(h + 1) * HEAD_DIM], _NT,
                            preferred_element_type=F32) + bias_ref[pl.ds(r0, KEY_BLOCK), :]
        r_ref[h % R_SLOTS][...] = s
        m_old = m_ref[h]
        m_blk = jnp.max(jnp.max(s.reshape(KEY_BLOCK // 8, 8, Q_TILE), axis=0), axis=0, keepdims=True)
        m_new = jnp.maximum(m_old, m_blk)
        m_ref[h] = m_new
        return m_new, jnp.exp2(m_old - m_new)

    first = scores_stage(0, 0) + scores_stage(0, 1)

    def attn_body(kb, carry):
        kb_next = jnp.minimum(kb + 1, nkb - 1)

        def pv_update(h, alpha):
            acc_ref[h] = alpha * acc_ref[h] + jnp.dot(vt_ref[kb, h], p_ref[h % P_SLOTS][...],
                                                      preferred_element_type=F32)

        stats = [carry[0:2], carry[2:4]]
        pending = None
        for h in range(N_HEADS):
            kb2, h2 = (kb, h + 2) if h + 2 < N_HEADS else (kb_next, h + 2 - N_HEADS)
            stats.append(scores_stage(kb2, h2))
            if pending is not None:
                pv_update(*pending)
            m_new, alpha = stats.pop(0)
            p_ref[h % P_SLOTS][...] = jnp.exp2(r_ref[h % R_SLOTS][...] - m_new).astype(BF16)
            pending = (h, alpha)
        pv_update(*pending)
        return stats[0] + stats[1]

    lax.fori_loop(0, nkb, attn_body, first)

    for h in range(N_HEADS):
        a = acc_ref[h]
        out_t = a[:HEAD_DIM, :] * (1.0 / a[HEAD_DIM:HEAD_DIM + 1, :])
        gate = ga_ref[:, h * HEAD_DIM:(h + 1) * HEAD_DIM].astype(F32)
        o_ref[:, h * HEAD_DIM:(h + 1) * HEAD_DIM] = (
            out_t.T * (gate * jax.nn.sigmoid(gate))).astype(o_ref.dtype)


def _sparse_attention(qqi, wixt, kix, kh, vt, gates, k_top):
    B, S, _ = qqi.shape
    DA = N_HEADS * HEAD_DIM
    DI = N_IDX_HEADS * IDX_DIM
    nq = S // Q_TILE
    return pl.pallas_call(
        functools.partial(_attn_kernel, k_top=k_top),
        grid=(B, nq),
        in_specs=[pl.BlockSpec((None, Q_TILE, DA), lambda b, i: (b, i, 0)),
                  pl.BlockSpec((None, Q_TILE, DI), lambda b, i: (b, i, DA // DI)),
                  pl.BlockSpec((None, N_IDX_HEADS, Q_TILE), lambda b, i: (b, 0, i)),
                  pl.BlockSpec((None, S, 2 * LANES), lambda b, i: (b, 0, 0)),
                  pl.BlockSpec((None,) + kh.shape[1:], lambda b, i: (b, 0, 0, 0, 0)),
                  pl.BlockSpec((None,) + vt.shape[1:], lambda b, i: (b, 0, 0, 0, 0),
                               pipeline_mode=pl.Buffered(1)),
                  pl.BlockSpec((Q_TILE, DA), lambda b, i: (b * nq + i, 0))],
        out_specs=pl.BlockSpec((Q_TILE, DA), lambda b, i: (b * nq + i, 0)),
        out_shape=jax.ShapeDtypeStruct((B * S, DA), BF16),
        scratch_shapes=[pltpu.VMEM((S, Q_TILE), F32),
                        pltpu.VMEM((S, Q_TILE), BF16),
                        pltpu.VMEM((S, Q_TILE), F32),
                        pltpu.VMEM((N_HEADS, V_ROWS, Q_TILE), F32),
                        pltpu.VMEM((N_HEADS, 1, Q_TILE), F32)]
                       + [pltpu.VMEM((KEY_BLOCK, Q_TILE), F32)] * R_SLOTS
                       + [pltpu.VMEM((KEY_BLOCK, Q_TILE), BF16)] * P_SLOTS,
        compiler_params=_cparams(("parallel", "arbitrary")),
        name="sparse_attention",
    )(qqi, qqi, wixt, kix, kh, vt, gates)


POOL_HALO = 16


def _pool_bands(ts, dtype):
    r = lax.broadcasted_iota(jnp.int32, (ts, ts), 0)
    c = lax.broadcasted_iota(jnp.int32, (ts, ts), 1)
    hr = lax.broadcasted_iota(jnp.int32, (POOL_HALO, POOL_HALO), 0)
    hc = lax.broadcasted_iota(jnp.int32, (POOL_HALO, POOL_HALO), 1)
    bands = [jnp.where((c <= r) & (c > r - win), 1.0 / win, 0.0) - jnp.where(c == r, 1.0, 0.0)
             for win in POOL_WINDOWS]
    hbands = [jnp.where(hc >= hr + (POOL_HALO + 1 - win), 1.0 / win, 0.0) for win in POOL_WINDOWS]
    return jnp.stack(bands).astype(dtype), jnp.stack(hbands).astype(dtype)


def _pool_kernel(pin_ref, halo_ref, gate_ref, wp_ref, ps_ref, band_ref, hband_ref, o_ref, mix_ref):
    i = pl.program_id(1)
    ts = pin_ref.shape[0]

    @pl.when(i == 0)
    def _():
        t = lax.broadcasted_iota(jnp.int32, (ts, POOL_GROUP), 0)
        for g, win in enumerate(POOL_WINDOWS):
            cols = slice(g * POOL_GROUP, (g + 1) * POOL_GROUP)
            cur = pin_ref[:, cols].astype(F32)
            ssum = jnp.concatenate([jnp.zeros((POOL_HALO, POOL_GROUP), F32), cur], axis=0)
            span = 1
            while span < win:
                ssum = ssum + pltpu.roll(ssum, span, 0)
                span *= 2
            count = jnp.minimum(t + 1, win).astype(F32)
            mix_ref[:, cols] = (ssum[POOL_HALO:, :] / count - cur).astype(mix_ref.dtype)

    @pl.when(i > 0)
    def _():
        for g in range(len(POOL_WINDOWS)):
            cols = slice(g * POOL_GROUP, (g + 1) * POOL_GROUP)
            m = jnp.dot(band_ref[g], pin_ref[:, cols], preferred_element_type=F32)
            top = m[:POOL_HALO, :] + jnp.dot(hband_ref[g], halo_ref[:, cols], preferred_element_type=F32)
            mix_ref[:POOL_HALO, cols] = top.astype(mix_ref.dtype)
            mix_ref[POOL_HALO:, cols] = m[POOL_HALO:, :].astype(mix_ref.dtype)

    for g in range(len(POOL_WINDOWS)):
        cols = slice(g * POOL_GROUP, (g + 1) * POOL_GROUP)
        y = jnp.dot(mix_ref[:, cols], wp_ref[g], preferred_element_type=F32)
        gate = gate_ref[:, cols].astype(F32)
        o_ref[:, cols] = (y * ps_ref[:, cols] * (gate * jax.nn.sigmoid(gate))).astype(o_ref.dtype)


def _pool_mixer(gates, w_pool, pool_scale, B, S, ts=256):
    T = gates.shape[0]
    D = gates.shape[1] // 3
    nS = S // ts
    hb = ts // POOL_HALO
    bands, hbands = _pool_bands(ts, gates.dtype)
    return pl.pallas_call(
        _pool_kernel,
        grid=(B, nS),
        in_specs=[pl.BlockSpec((ts, D), lambda b, i: (b * nS + i, 1)),
                  pl.BlockSpec((POOL_HALO, D), lambda b, i: (jnp.maximum((b * nS + i) * hb - 1, 0), 1)),
                  pl.BlockSpec((ts, D), lambda b, i: (b * nS + i, 2)),
                  pl.BlockSpec(w_pool.shape, lambda b, i: (0, 0, 0)),
                  pl.BlockSpec((1, D), lambda b, i: (0, 0)),
                  pl.BlockSpec(bands.shape, lambda b, i: (0, 0, 0)),
                  pl.BlockSpec(hbands.shape, lambda b, i: (0, 0, 0))],
        out_specs=pl.BlockSpec((ts, D), lambda b, i: (b * nS + i, 0)),
        out_shape=jax.ShapeDtypeStruct((T, D), BF16),
        scratch_shapes=[pltpu.VMEM((ts, D), BF16)],
        compiler_params=_cparams(("parallel", "arbitrary")),
        name="pool_mixer",
    )(gates, gates, gates, w_pool, pool_scale.reshape(1, D), bands, hbands)


OUT_COL_TILE = 512
OUT_PROJ_VMEM_LIMIT = 60 * 1024 * 1024


def _outproj_kernel(ya_ref, yb_ref, w_ref, x_ref, g_ref, o_ref):
    da = ya_ref.shape[1]
    tm, d = o_ref.shape
    n_col = d // OUT_COL_TILE
    ss = jnp.zeros((tm, LANES), F32)
    for jj in range(n_col):
        cols = slice(jj * OUT_COL_TILE, (jj + 1) * OUT_COL_TILE)
        y = (jnp.dot(ya_ref[...], w_ref[:da, cols], preferred_element_type=F32)
             + jnp.dot(yb_ref[...], w_ref[da:, cols], preferred_element_type=F32))
        o_ref[:, cols] = y
        for k in range(OUT_COL_TILE // LANES):
            yk = y[:, k * LANES:(k + 1) * LANES]
            ss = ss + yk * yk
    rs = lax.rsqrt(jnp.sum(ss, axis=-1, keepdims=True) * (1.0 / d) + NORM_EPS)
    for jj in range(n_col):
        cols = slice(jj * OUT_COL_TILE, (jj + 1) * OUT_COL_TILE)
        o_ref[:, cols] = x_ref[:, cols] + o_ref[:, cols] * rs * g_ref[:, cols]


def _out_proj(ya, yb, w_out_b, x2, post_g, tm=256):
    T, D = x2.shape
    DA = ya.shape[1]
    DB = yb.shape[1]
    return pl.pallas_call(
        _outproj_kernel,
        grid=(T // tm,),
        in_specs=[pl.BlockSpec((tm, DA), lambda i: (i, 0)),
                  pl.BlockSpec((tm, DB), lambda i: (i, 0)),
                  pl.BlockSpec((DA + DB, D), lambda i: (0, 0), pipeline_mode=pl.Buffered(1)),
                  pl.BlockSpec((tm, D), lambda i: (i, 0)),
                  pl.BlockSpec((1, D), lambda i: (0, 0))],
        out_specs=pl.BlockSpec((tm, D), lambda i: (i, 0)),
        out_shape=jax.ShapeDtypeStruct((T, D), F32),
        compiler_params=pltpu.CompilerParams(dimension_semantics=("parallel",),
                                             vmem_limit_bytes=OUT_PROJ_VMEM_LIMIT),
        name="out_proj",
    )(ya, yb, w_out_b, x2, post_g.reshape(1, D))


PROJ_TN = 512


def _layer(x2, B, S, pre_g, w_in, kv_g, w_uk, w_uv, ln_g, ln_b, w_pool, pool_scale, w_out, post_g, k_top):
    d_attn = N_HEADS * HEAD_DIM
    n_qi = N_IDX_HEADS * IDX_DIM
    o_c = d_attn
    o_qi = o_c + KV_LATENT
    o_k = o_qi + n_qi
    o_w = o_k + IDX_DIM
    o_g = o_w + N_IDX_HEADS
    K, n_in = w_in.shape
    wt = w_in.T
    w_uk_all = jnp.transpose(w_uk, (1, 0, 2)).reshape(KV_LATENT, d_attn).astype(BF16)
    w_uvt_all = jnp.swapaxes(w_uv, 1, 2).reshape(d_attn, KV_LATENT).astype(BF16)
    w_pool_b = w_pool.astype(BF16)

    h = _prenorm(x2, pre_g)
    tn = PROJ_TN
    assert o_c % tn == 0 and o_qi % tn == 0 and n_qi % tn == 0
    qqi_spec = pl.BlockSpec((tn, K), lambda i, j: (jnp.where(j < o_c // tn, j, j + (o_qi - o_c) // tn), 0))
    qqi = _matmul_nt(h, wt, qqi_spec, d_attn + n_qi, BF16, 1024, tn, "proj_q_qi",
                     n_scaled=o_c // tn, scale=(HEAD_DIM ** -0.5) * LOG2E)
    assert o_g % 16 == 0
    gates_spec = pl.BlockSpec((pl.Element(tn), pl.Element(K)),
                              lambda i, j: (pl.multiple_of(o_g + j * tn, 16), 0))
    gates, w_out_b = _matmul_nt(h, wt, gates_spec, n_in - o_g, BF16, 1024, tn, "proj_gates", side=w_out)
    kh, vt, kix, wixt = _latent_proj(h, wt, o_c, o_k, kv_g, ln_g, ln_b, w_uk_all, w_uvt_all, B, S)
    ya = _sparse_attention(qqi.reshape(B, S, d_attn + n_qi), wixt, kix, kh, vt, gates, k_top)
    yb = _pool_mixer(gates, w_pool_b, pool_scale, B, S)
    return _out_proj(ya, yb, w_out_b, x2, post_g)


def kernel(x, pre_norm, w_in, kv_norm, w_uk, w_uv, idx_k_norm_g, idx_k_norm_b, w_pool, pool_scale, w_out, post_norm):
    B, S, D = x.shape
    k_top = min(TOPK_MAX, S // 4)
    x2 = x.reshape(B * S, D)
    for l in range(pre_norm.shape[0]):
        x2 = _layer(x2, B, S, pre_norm[l], w_in[l], kv_norm[l], w_uk[l], w_uv[l], idx_k_norm_g[l],
                    idx_k_norm_b[l], w_pool[l], pool_scale[l], w_out[l], post_norm[l], k_top)
    return x2.reshape(B, S, D)
```

```python
import functools
import math

import jax
import jax.numpy as jnp
from jax import lax
from jax.experimental import pallas as pl
from jax.experimental.pallas import tpu as pltpu

F32 = jnp.float32
BF16 = jnp.bfloat16

NORM_EPS = 1e-6
CHUNK = 64
N_HEADS = 16
HEAD_DIM = 128
KV_LATENT = 512
N_IDX_HEADS = 16
IDX_DIM = 64
TOPK_MAX = 256
POOL_WINDOWS = (2, 4, 8, 16)
POOL_GROUP = 512

LANES = 128
Q_TILE = 256
KEY_BLOCK = 256
BF16_SUBLANES = 16
V_ROWS = HEAD_DIM + BF16_SUBLANES
R_SLOTS = 4
P_SLOTS = 2
assert KEY_BLOCK % Q_TILE == 0
HALF_BITS = 16
FINE_BITS = 18
M_FLOOR = -1e30
LOG2E = math.log2(math.e)
VMEM_LIMIT = 56 * 1024 * 1024


def _cparams(sem):
    return pltpu.CompilerParams(dimension_semantics=sem, vmem_limit_bytes=VMEM_LIMIT)


def _silu(x):
    h = 0.5 * x
    return h + h * jnp.tanh(h)


def _prenorm_kernel(x_ref, g_ref, o_ref):
    x = x_ref[...]
    ms = jnp.mean(x * x, axis=-1, keepdims=True)
    o_ref[...] = (x * lax.rsqrt(ms + NORM_EPS) * g_ref[...]).astype(o_ref.dtype)


def _prenorm(x2, g, tm=256):
    T, D = x2.shape
    return pl.pallas_call(
        _prenorm_kernel,
        grid=(T // tm,),
        in_specs=[pl.BlockSpec((tm, D), lambda i: (i, 0)),
                  pl.BlockSpec((1, D), lambda i: (0, 0))],
        out_specs=pl.BlockSpec((tm, D), lambda i: (i, 0)),
        out_shape=jax.ShapeDtypeStruct((T, D), BF16),
        compiler_params=_cparams(("parallel",)),
        name="prenorm",
    )(x2, g.reshape(1, D))


_NT = (((1,), (1,)), ((), ()))


def _matmul_nt_kernel(h_ref, wt_ref, *rest, n_scaled, scale):
    o_ref = rest[-1] if len(rest) == 1 else rest[1]
    z = lax.dot_general(h_ref[...], wt_ref[...].astype(BF16), _NT, preferred_element_type=F32)
    if n_scaled:
        z = z * jnp.where(pl.program_id(1) < n_scaled, jnp.float32(scale), jnp.float32(1.0))
    o_ref[...] = z.astype(o_ref.dtype)
    if len(rest) == 3:
        side_ref, _, side_out_ref = rest
        side_out_ref[...] = side_ref[...].astype(side_out_ref.dtype)


def _matmul_nt(h, wt, w_spec, n_out, out_dtype, tm, tn, name, n_scaled=0, scale=1.0, side=None):
    T, K = h.shape
    tm = min(tm, T)
    n_i, n_j = T // tm, n_out // tn
    in_specs = [pl.BlockSpec((tm, K), lambda i, j: (i, 0)), w_spec]
    out_specs = [pl.BlockSpec((tm, tn), lambda i, j: (i, j))]
    out_shape = [jax.ShapeDtypeStruct((T, n_out), out_dtype)]
    args = [h, wt]
    if side is not None:
        R, C = side.shape
        nb = 1
        while nb * 2 <= min(n_i * n_j, R // BF16_SUBLANES):
            nb *= 2
        assert R % nb == 0
        side_map = lambda i, j: (jnp.minimum(i * n_j + j, nb - 1), 0)
        in_specs.append(pl.BlockSpec((R // nb, C), side_map))
        out_specs.append(pl.BlockSpec((R // nb, C), side_map))
        out_shape.append(jax.ShapeDtypeStruct((R, C), BF16))
        args.append(side)
    res = pl.pallas_call(
        functools.partial(_matmul_nt_kernel, n_scaled=n_scaled, scale=scale),
        grid=(n_i, n_j),
        in_specs=in_specs,
        out_specs=out_specs,
        out_shape=out_shape,
        compiler_params=_cparams(("parallel", "arbitrary")),
        name=name,
    )(*args)
    return res if side is not None else res[0]


def _latent_kernel(h_ref, wc_ref, wkw_ref, kvg_ref, lng_ref, lnb_ref, wuk_ref, wuvt_ref,
                   kh_ref, vt_ref, kix_ref, wixt_ref):
    h = h_ref[...]
    c = lax.dot_general(h, wc_ref[...].astype(BF16), _NT, preferred_element_type=F32)
    cn = c * lax.rsqrt(jnp.mean(c * c, axis=-1, keepdims=True) + NORM_EPS) * kvg_ref[...]
    cb = cn.astype(BF16)
    n_kb = kh_ref.shape[0]
    for pair in range(N_HEADS // 2):
        k2 = jnp.dot(cb, wuk_ref[:, pair * 2 * HEAD_DIM:(pair + 1) * 2 * HEAD_DIM],
                     preferred_element_type=F32)
        for u in range(n_kb):
            for e in range(2):
                kh_ref[u, 2 * pair + e] = k2[u * KEY_BLOCK:(u + 1) * KEY_BLOCK,
                                             e * HEAD_DIM:(e + 1) * HEAD_DIM].astype(kh_ref.dtype)
    for u in range(n_kb):
        ct_u = cn[u * KEY_BLOCK:(u + 1) * KEY_BLOCK, :].T.astype(BF16)
        v_all = jnp.dot(wuvt_ref[...], ct_u, preferred_element_type=F32)
        for hh in range(N_HEADS):
            vt_ref[u, hh, :HEAD_DIM, :] = v_all[hh * HEAD_DIM:(hh + 1) * HEAD_DIM, :].astype(vt_ref.dtype)
            vt_ref[u, hh, HEAD_DIM:, :] = jnp.ones((V_ROWS - HEAD_DIM, KEY_BLOCK), vt_ref.dtype)
    kw = lax.dot_general(h, wkw_ref[...].astype(BF16), _NT, preferred_element_type=F32)
    lane = lax.broadcasted_iota(jnp.int32, kw.shape, 1)
    is_k = lane < IDX_DIM
    mu = jnp.sum(jnp.where(is_k, kw, 0.0), axis=-1, keepdims=True) * (1.0 / IDX_DIM)
    d = jnp.where(is_k, kw - mu, 0.0)
    var = jnp.sum(d * d, axis=-1, keepdims=True) * (1.0 / IDX_DIM)
    kn = d * lax.rsqrt(var + NORM_EPS) * lng_ref[...] + lnb_ref[...]
    kix_ref[...] = jnp.concatenate([kn, pltpu.roll(kn, IDX_DIM, 1)], axis=1).astype(kix_ref.dtype)
    wixt_ref[...] = kw.T[IDX_DIM:IDX_DIM + N_IDX_HEADS, :] * (N_IDX_HEADS ** -0.5)


def _latent_proj(h, wt, o_c, o_k, kv_norm, ln_g, ln_b, w_uk_all, w_uvt_all, B, S, tm=512):
    T, K = h.shape
    nS = S // tm
    pad = jnp.zeros((LANES - IDX_DIM,), F32)
    lng = jnp.concatenate([ln_g, pad]).reshape(1, LANES)
    lnb = jnp.concatenate([ln_b, pad]).reshape(1, LANES)
    assert o_c % KV_LATENT == 0 and o_k % LANES == 0
    return pl.pallas_call(
        _latent_kernel,
        grid=(B, nS),
        in_specs=[pl.BlockSpec((tm, K), lambda b, i: (b * nS + i, 0)),
                  pl.BlockSpec((KV_LATENT, K), lambda b, i: (o_c // KV_LATENT, 0)),
                  pl.BlockSpec((LANES, K), lambda b, i: (o_k // LANES, 0)),
                  pl.BlockSpec((1, KV_LATENT), lambda b, i: (0, 0)),
                  pl.BlockSpec((1, LANES), lambda b, i: (0, 0)),
                  pl.BlockSpec((1, LANES), lambda b, i: (0, 0)),
                  pl.BlockSpec(w_uk_all.shape, lambda b, i: (0, 0)),
                  pl.BlockSpec(w_uvt_all.shape, lambda b, i: (0, 0))],
        out_specs=[pl.BlockSpec((None, tm // KEY_BLOCK, N_HEADS, KEY_BLOCK, HEAD_DIM),
                                lambda b, i: (b, i, 0, 0, 0)),
                   pl.BlockSpec((None, tm // KEY_BLOCK, N_HEADS, V_ROWS, KEY_BLOCK),
                                lambda b, i: (b, i, 0, 0, 0)),
                   pl.BlockSpec((None, tm, 2 * LANES), lambda b, i: (b, i, 0)),
                   pl.BlockSpec((None, N_IDX_HEADS, tm), lambda b, i: (b, 0, i))],
        out_shape=[jax.ShapeDtypeStruct((B, S // KEY_BLOCK, N_HEADS, KEY_BLOCK, HEAD_DIM), BF16),
                   jax.ShapeDtypeStruct((B, S // KEY_BLOCK, N_HEADS, V_ROWS, KEY_BLOCK), BF16),
                   jax.ShapeDtypeStruct((B, S, 2 * LANES), BF16),
                   jax.ShapeDtypeStruct((B, N_IDX_HEADS, S), F32)],
        compiler_params=_cparams(("parallel", "arbitrary")),
        name="latent_proj",
    )(h, wt, wt, kv_norm.reshape(1, KV_LATENT), lng, lnb, w_uk_all, w_uvt_all)


def _sortable_to_f32(key):
    bits = key ^ ((key >> 31) & jnp.int32(0x7FFFFFFF))
    return pltpu.bitcast(bits, F32)


def _fold8(a):
    return jnp.sum(a.reshape(a.shape[0] // 8, 8, a.shape[1]), axis=0)


def _attn_kernel(q_ref, qi_ref, wixt_ref, kix_ref, kh_ref, vt_ref, ga_ref,
                 o_ref, isc_ref, hsc_ref, bias_ref, acc_ref, m_ref, *stage_refs, k_top):
    r_ref, p_ref = stage_refs[:R_SLOTS], stage_refs[R_SLOTS:]
    i = pl.program_id(1)
    t0 = i * Q_TILE
    nkb = (t0 + Q_TILE + KEY_BLOCK - 1) // KEY_BLOCK
    neg_inf = jnp.float32(-jnp.inf)

    t_idx = t0 + lax.broadcasted_iota(jnp.int32, (KEY_BLOCK, Q_TILE), 1)
    q_chunk = t_idx // CHUNK

    def admissible(kb):
        s_idx = kb * KEY_BLOCK + lax.broadcasted_iota(jnp.int32, (KEY_BLOCK, Q_TILE), 0)
        return (s_idx // CHUNK) <= q_chunk

    wix = wixt_ref[...] * (IDX_DIM ** -0.5)

    def idx_body(kb, carry):
        r0 = pl.multiple_of(kb * KEY_BLOCK, KEY_BLOCK)
        kx = kix_ref[pl.ds(r0, KEY_BLOCK), :]
        kx_even, kx_odd = kx[:, :LANES], kx[:, LANES:]
        acc = jnp.zeros((KEY_BLOCK, Q_TILE), F32)
        for pair in range(N_IDX_HEADS // 2):
            rhs = qi_ref[:, pair * LANES:(pair + 1) * LANES]
            for par, kxp in ((0, kx_even), (1, kx_odd)):
                h = 2 * pair + par
                d = lax.dot_general(kxp, rhs, (((1,), (1,)), ((), ())), preferred_element_type=F32)
                acc = acc + wix[h:h + 1, :] * jnp.maximum(d, 0.0)
        isc_ref[pl.ds(r0, KEY_BLOCK), :] = acc
        hsc_ref[pl.ds(r0, KEY_BLOCK), :] = acc.astype(BF16)
        return carry

    lax.fori_loop(0, nkb, idx_body, 0)
    diag_r0 = pl.multiple_of((nkb - 1) * KEY_BLOCK, KEY_BLOCK)
    diag = jnp.where(admissible(nkb - 1), isc_ref[pl.ds(diag_r0, KEY_BLOCK), :], neg_inf)
    isc_ref[pl.ds(diag_r0, KEY_BLOCK), :] = diag
    hsc_ref[pl.ds(diag_r0, KEY_BLOCK), :] = diag.astype(BF16)

    def count(pred):
        def body(kb, cnt):
            r0 = pl.multiple_of(kb * KEY_BLOCK, KEY_BLOCK)
            v = isc_ref[pl.ds(r0, KEY_BLOCK), :]
            return cnt + _fold8(jnp.where(pred(v), 1, 0).astype(jnp.int32))
        c8 = lax.fori_loop(0, nkb, body, jnp.zeros((8, Q_TILE), jnp.int32))
        return jnp.sum(c8, axis=0, keepdims=True)

    def count_half(cand):
        one, zero = jnp.ones((), BF16), jnp.zeros((), BF16)

        def body(kb, cnt):
            r0 = pl.multiple_of(kb * KEY_BLOCK, KEY_BLOCK)
            ind = jnp.where(hsc_ref[pl.ds(r0, KEY_BLOCK), :] >= cand, one, zero)
            parts = [ind[k * BF16_SUBLANES:(k + 1) * BF16_SUBLANES, :]
                     for k in range(KEY_BLOCK // BF16_SUBLANES)]
            while len(parts) > 1:
                parts = [a + b for a, b in zip(parts[0::2], parts[1::2])]
            return cnt + parts[0].astype(F32)

        c16 = lax.fori_loop(0, nkb, body, jnp.zeros((BF16_SUBLANES, Q_TILE), F32))
        return jnp.sum(c16, axis=0, keepdims=True)

    def half_to_bf16(k16):
        bits = k16 ^ ((k16 >> 31) & jnp.int32(0x7FFF))
        return pltpu.bitcast(jnp.left_shift(bits, 16), F32).astype(BF16)

    def coarse_body(step, k16):
        cand = k16 + jnp.left_shift(jnp.int32(1), HALF_BITS - 1 - step)
        return jnp.where(count_half(half_to_bf16(cand)) >= k_top, cand, k16)

    def fine_body(step, carry):
        key, key_cnt = carry
        cand = key + jnp.left_shift(jnp.int32(1), FINE_BITS - 1 - step)
        n_ge = count(lambda v: v >= _sortable_to_f32(cand))
        take = n_ge >= k_top
        return jnp.where(take, cand, key), jnp.where(take, n_ge, key_cnt)

    searching = t0 + Q_TILE > k_top
    k16 = lax.fori_loop(0, jnp.where(searching, HALF_BITS, 0), coarse_body,
                        jnp.full((1, Q_TILE), -2 ** (HALF_BITS - 1), jnp.int32))
    key_lo = jnp.left_shift(k16, 16) - jnp.int32(1 << 16)
    key, key_cnt = lax.fori_loop(0, jnp.where(searching, FINE_BITS, 0), fine_body,
                                 (key_lo, jnp.full((1, Q_TILE), k_top + 1, jnp.int32)))
    n_adm = (t_idx[0:1, :] // CHUNK + 1) * CHUNK
    take_all = n_adm <= k_top
    thr = jnp.where(take_all, neg_inf, _sortable_to_f32(key))
    has_ties = jnp.max(jnp.where(take_all, 0, key_cnt - k_top)) > 0

    @pl.when(jnp.logical_not(has_ties))
    def _():
        def sel_body(kb, carry):
            r0 = pl.multiple_of(kb * KEY_BLOCK, KEY_BLOCK)
            bias_ref[pl.ds(r0, KEY_BLOCK), :] = jnp.where(isc_ref[pl.ds(r0, KEY_BLOCK), :] >= thr,
                                                          0.0, neg_inf)
            return carry

        lax.fori_loop(0, nkb - 1, sel_body, 0)
        sel = (isc_ref[pl.ds(diag_r0, KEY_BLOCK), :] >= thr) & admissible(nkb - 1)
        bias_ref[pl.ds(diag_r0, KEY_BLOCK), :] = jnp.where(sel, 0.0, neg_inf)

    @pl.when(has_ties)
    def _():
        n_gt = count(lambda v: v > thr)
        need = (k_top - n_gt).astype(F32)
        tri = (lax.broadcasted_iota(jnp.int32, (KEY_BLOCK, KEY_BLOCK), 1)
               <= lax.broadcasted_iota(jnp.int32, (KEY_BLOCK, KEY_BLOCK), 0)).astype(BF16)

        def sel_body(kb, tie_carry):
            r0 = pl.multiple_of(kb * KEY_BLOCK, KEY_BLOCK)
            v = isc_ref[pl.ds(r0, KEY_BLOCK), :]
            tie = (v == thr).astype(BF16)
            rank = jnp.dot(tri, tie, preferred_element_type=F32) + tie_carry
            sel = ((v > thr) | ((v == thr) & (rank <= need))) & admissible(kb)
            bias_ref[pl.ds(r0, KEY_BLOCK), :] = jnp.where(sel, 0.0, neg_inf)
            return rank[KEY_BLOCK - 1:KEY_BLOCK, :]

        lax.fori_loop(0, nkb, sel_body, jnp.zeros((1, Q_TILE), F32))

    acc_ref[...] = jnp.zeros_like(acc_ref)
    m_ref[...] = jnp.full_like(m_ref, M_FLOOR)

    def scores_stage(kb, h):
        r0 = pl.multiple_of(kb * KEY_BLOCK, KEY_BLOCK)
        s = lax.dot_general(kh_ref[kb, h], q_ref[:, h * HEAD_DIM:(h + 1) * HEAD_DIM], _NT,
                            preferred_element_type=F32) + bias_ref[pl.ds(r0, KEY_BLOCK), :]
        r_ref[h % R_SLOTS][...] = s
        m_old = m_ref[h]
        m_blk = jnp.max(jnp.max(s.reshape(KEY_BLOCK // 8, 8, Q_TILE), axis=0), axis=0, keepdims=True)
        m_new = jnp.maximum(m_old, m_blk)
        m_ref[h] = m_new
        return m_new, jnp.exp2(m_old - m_new)

    first = scores_stage(0, 0) + scores_stage(0, 1)

    def attn_body(kb, carry):
        kb_next = jnp.minimum(kb + 1, nkb - 1)

        def pv_update(h, alpha):
            acc_ref[h] = alpha * acc_ref[h] + jnp.dot(vt_ref[kb, h], p_ref[h % P_SLOTS][...],
                                                      preferred_element_type=F32)

        stats = [carry[0:2], carry[2:4]]
        pending = None
        for h in range(N_HEADS):
            kb2, h2 = (kb, h + 2) if h + 2 < N_HEADS else (kb_next, h + 2 - N_HEADS)
            stats.append(scores_stage(kb2, h2))
            if pending is not None:
                pv_update(*pending)
            m_new, alpha = stats.pop(0)
            p_ref[h % P_SLOTS][...] = jnp.exp2(r_ref[h % R_SLOTS][...] - m_new).astype(BF16)
            pending = (h, alpha)
        pv_update(*pending)
        return stats[0] + stats[1]

    lax.fori_loop(0, nkb, attn_body, first)

    for h in range(N_HEADS):
        a = acc_ref[h]
        out_t = a[:HEAD_DIM, :] * (1.0 / a[HEAD_DIM:HEAD_DIM + 1, :])
        gate = ga_ref[:, h * HEAD_DIM:(h + 1) * HEAD_DIM].astype(F32)
        o_ref[:, h * HEAD_DIM:(h + 1) * HEAD_DIM] = (
            out_t.T * _silu(gate)).astype(o_ref.dtype)


def _sparse_attention(qqi, wixt, kix, kh, vt, gates, k_top):
    B, S, _ = qqi.shape
    DA = N_HEADS * HEAD_DIM
    DI = N_IDX_HEADS * IDX_DIM
    nq = S // Q_TILE
    return pl.pallas_call(
        functools.partial(_attn_kernel, k_top=k_top),
        grid=(B, nq),
        in_specs=[pl.BlockSpec((None, Q_TILE, DA), lambda b, i: (b, i, 0)),
                  pl.BlockSpec((None, Q_TILE, DI), lambda b, i: (b, i, DA // DI)),
                  pl.BlockSpec((None, N_IDX_HEADS, Q_TILE), lambda b, i: (b, 0, i)),
                  pl.BlockSpec((None, S, 2 * LANES), lambda b, i: (b, 0, 0)),
                  pl.BlockSpec((None,) + kh.shape[1:], lambda b, i: (b, 0, 0, 0, 0)),
                  pl.BlockSpec((None,) + vt.shape[1:], lambda b, i: (b, 0, 0, 0, 0),
                               pipeline_mode=pl.Buffered(1)),
                  pl.BlockSpec((Q_TILE, DA), lambda b, i: (b * nq + i, 0))],
        out_specs=pl.BlockSpec((Q_TILE, DA), lambda b, i: (b * nq + i, 0)),
        out_shape=jax.ShapeDtypeStruct((B * S, DA), BF16),
        scratch_shapes=[pltpu.VMEM((S, Q_TILE), F32),
                        pltpu.VMEM((S, Q_TILE), BF16),
                        pltpu.VMEM((S, Q_TILE), F32),
                        pltpu.VMEM((N_HEADS, V_ROWS, Q_TILE), F32),
                        pltpu.VMEM((N_HEADS, 1, Q_TILE), F32)]
                       + [pltpu.VMEM((KEY_BLOCK, Q_TILE), F32)] * R_SLOTS
                       + [pltpu.VMEM((KEY_BLOCK, Q_TILE), BF16)] * P_SLOTS,
        compiler_params=_cparams(("parallel", "arbitrary")),
        name="sparse_attention",
    )(qqi, qqi, wixt, kix, kh, vt, gates)


POOL_HALO = 16


def _pool_bands(ts, dtype):
    r = lax.broadcasted_iota(jnp.int32, (ts, ts), 0)
    c = lax.broadcasted_iota(jnp.int32, (ts, ts), 1)
    hr = lax.broadcasted_iota(jnp.int32, (POOL_HALO, POOL_HALO), 0)
    hc = lax.broadcasted_iota(jnp.int32, (POOL_HALO, POOL_HALO), 1)
    bands = [jnp.where((c <= r) & (c > r - win), 1.0 / win, 0.0) - jnp.where(c == r, 1.0, 0.0)
             for win in POOL_WINDOWS]
    hbands = [jnp.where(hc >= hr + (POOL_HALO + 1 - win), 1.0 / win, 0.0) for win in POOL_WINDOWS]
    return jnp.stack(bands).astype(dtype), jnp.stack(hbands).astype(dtype)


def _pool_kernel(pin_ref, halo_ref, gate_ref, wp_ref, ps_ref, band_ref, hband_ref, o_ref, mix_ref):
    i = pl.program_id(1)
    ts = pin_ref.shape[0]

    @pl.when(i == 0)
    def _():
        t = lax.broadcasted_iota(jnp.int32, (ts, POOL_GROUP), 0)
        for g, win in enumerate(POOL_WINDOWS):
            cols = slice(g * POOL_GROUP, (g + 1) * POOL_GROUP)
            cur = pin_ref[:, cols].astype(F32)
            ssum = jnp.concatenate([jnp.zeros((POOL_HALO, POOL_GROUP), F32), cur], axis=0)
            span = 1
            while span < win:
                ssum = ssum + pltpu.roll(ssum, span, 0)
                span *= 2
            count = jnp.minimum(t + 1, win).astype(F32)
            mix_ref[:, cols] = (ssum[POOL_HALO:, :] / count - cur).astype(mix_ref.dtype)

    @pl.when(i > 0)
    def _():
        for g in range(len(POOL_WINDOWS)):
            cols = slice(g * POOL_GROUP, (g + 1) * POOL_GROUP)
            m = jnp.dot(band_ref[g], pin_ref[:, cols], preferred_element_type=F32)
            top = m[:POOL_HALO, :] + jnp.dot(hband_ref[g], halo_ref[:, cols], preferred_element_type=F32)
            mix_ref[:POOL_HALO, cols] = top.astype(mix_ref.dtype)
            mix_ref[POOL_HALO:, cols] = m[POOL_HALO:, :].astype(mix_ref.dtype)

    for g in range(len(POOL_WINDOWS)):
        cols = slice(g * POOL_GROUP, (g + 1) * POOL_GROUP)
        y = jnp.dot(mix_ref[:, cols], wp_ref[g], preferred_element_type=F32)
        gate = gate_ref[:, cols].astype(F32)
        o_ref[:, cols] = (y * ps_ref[:, cols] * _silu(gate)).astype(o_ref.dtype)


def _pool_mixer(gates, w_pool, pool_scale, B, S, ts=256):
    T = gates.shape[0]
    D = gates.shape[1] // 3
    nS = S // ts
    hb = ts // POOL_HALO
    bands, hbands = _pool_bands(ts, gates.dtype)
    return pl.pallas_call(
        _pool_kernel,
        grid=(B, nS),
        in_specs=[pl.BlockSpec((ts, D), lambda b, i: (b * nS + i, 1)),
                  pl.BlockSpec((POOL_HALO, D), lambda b, i: (jnp.maximum((b * nS + i) * hb - 1, 0), 1)),
                  pl.BlockSpec((ts, D), lambda b, i: (b * nS + i, 2)),
                  pl.BlockSpec(w_pool.shape, lambda b, i: (0, 0, 0)),
                  pl.BlockSpec((1, D), lambda b, i: (0, 0)),
                  pl.BlockSpec(bands.shape, lambda b, i: (0, 0, 0)),
                  pl.BlockSpec(hbands.shape, lambda b, i: (0, 0, 0))],
        out_specs=pl.BlockSpec((ts, D), lambda b, i: (b * nS + i, 0)),
        out_shape=jax.ShapeDtypeStruct((T, D), BF16),
        scratch_shapes=[pltpu.VMEM((ts, D), BF16)],
        compiler_params=_cparams(("parallel", "arbitrary")),
        name="pool_mixer",
    )(gates, gates, gates, w_pool, pool_scale.reshape(1, D), bands, hbands)


OUT_COL_TILE = 512
OUT_PROJ_VMEM_LIMIT = 60 * 1024 * 1024


def _outproj_kernel(ya_ref, yb_ref, w_ref, x_ref, g_ref, o_ref):
    da = ya_ref.shape[1]
    tm, d = o_ref.shape
    n_col = d // OUT_COL_TILE
    ss = jnp.zeros((tm, LANES), F32)
    for jj in range(n_col):
        cols = slice(jj * OUT_COL_TILE, (jj + 1) * OUT_COL_TILE)
        y = (jnp.dot(ya_ref[...], w_ref[:da, cols], preferred_element_type=F32)
             + jnp.dot(yb_ref[...], w_ref[da:, cols], preferred_element_type=F32))
        o_ref[:, cols] = y
        for k in range(OUT_COL_TILE // LANES):
            yk = y[:, k * LANES:(k + 1) * LANES]
            ss = ss + yk * yk
    rs = lax.rsqrt(jnp.sum(ss, axis=-1, keepdims=True) * (1.0 / d) + NORM_EPS)
    for jj in range(n_col):
        cols = slice(jj * OUT_COL_TILE, (jj + 1) * OUT_COL_TILE)
        o_ref[:, cols] = x_ref[:, cols] + o_ref[:, cols] * rs * g_ref[:, cols]


def _out_proj(ya, yb, w_out_b, x2, post_g, tm=256):
    T, D = x2.shape
    DA = ya.shape[1]
    DB = yb.shape[1]
    return pl.pallas_call(
        _outproj_kernel,
        grid=(T // tm,),
        in_specs=[pl.BlockSpec((tm, DA), lambda i: (i, 0)),
                  pl.BlockSpec((tm, DB), lambda i: (i, 0)),
                  pl.BlockSpec((DA + DB, D), lambda i: (0, 0), pipeline_mode=pl.Buffered(1)),
                  pl.BlockSpec((tm, D), lambda i: (i, 0)),
                  pl.BlockSpec((1, D), lambda i: (0, 0))],
        out_specs=pl.BlockSpec((tm, D), lambda i: (i, 0)),
        out_shape=jax.ShapeDtypeStruct((T, D), F32),
        compiler_params=pltpu.CompilerParams(dimension_semantics=("parallel",),
                                             vmem_limit_bytes=OUT_PROJ_VMEM_LIMIT),
        name="out_proj",
    )(ya, yb, w_out_b, x2, post_g.reshape(1, D))


PROJ_TN = 512


def _layer(x2, B, S, pre_g, w_in, kv_g, w_uk, w_uv, ln_g, ln_b, w_pool, pool_scale, w_out, post_g, k_top):
    d_attn = N_HEADS * HEAD_DIM
    n_qi = N_IDX_HEADS * IDX_DIM
    o_c = d_attn
    o_qi = o_c + KV_LATENT
    o_k = o_qi + n_qi
    o_w = o_k + IDX_DIM
    o_g = o_w + N_IDX_HEADS
    K, n_in = w_in.shape
    wt = w_in.T
    w_uk_all = jnp.transpose(w_uk, (1, 0, 2)).reshape(KV_LATENT, d_attn).astype(BF16)
    w_uvt_all = jnp.swapaxes(w_uv, 1, 2).reshape(d_attn, KV_LATENT).astype(BF16)
    w_pool_b = w_pool.astype(BF16)

    h = _prenorm(x2, pre_g)
    tn = PROJ_TN
    assert o_c % tn == 0 and o_qi % tn == 0 and n_qi % tn == 0
    qqi_spec = pl.BlockSpec((tn, K), lambda i, j: (jnp.where(j < o_c // tn, j, j + (o_qi - o_c) // tn), 0))
    qqi = _matmul_nt(h, wt, qqi_spec, d_attn + n_qi, BF16, 1024, tn, "proj_q_qi",
                     n_scaled=o_c // tn, scale=(HEAD_DIM ** -0.5) * LOG2E)
    assert o_g % 16 == 0
    gates_spec = pl.BlockSpec((pl.Element(tn), pl.Element(K)),
                              lambda i, j: (pl.multiple_of(o_g + j * tn, 16), 0))
    gates, w_out_b = _matmul_nt(h, wt, gates_spec, n_in - o_g, BF16, 1024, tn, "proj_gates", side=w_out)
    kh, vt, kix, wixt = _latent_proj(h, wt, o_c, o_k, kv_g, ln_g, ln_b, w_uk_all, w_uvt_all, B, S)
    ya = _sparse_attention(qqi.reshape(B, S, d_attn + n_qi), wixt, kix, kh, vt, gates, k_top)
    yb = _pool_mixer(gates, w_pool_b, pool_scale, B, S)
    return _out_proj(ya, yb, w_out_b, x2, post_g)


def kernel(x, pre_norm, w_in, kv_norm, w_uk, w_uv, idx_k_norm_g, idx_k_norm_b, w_pool, pool_scale, w_out, post_norm):
    B, S, D = x.shape
    k_top = min(TOPK_MAX, S // 4)
    x2 = x.reshape(B * S, D)
    for l in range(pre_norm.shape[0]):
        x2 = _layer(x2, B, S, pre_norm[l], w_in[l], kv_norm[l], w_uk[l], w_uv[l], idx_k_norm_g[l],
                    idx_k_norm_b[l], w_pool[l], pool_scale[l], w_out[l], post_norm[l], k_top)
    return x2.reshape(B, S, D)
```

```python
import functools
import math

import jax
import jax.numpy as jnp
from jax import lax
from jax.experimental import pallas as pl
from jax.experimental.pallas import tpu as pltpu

F32 = jnp.float32
BF16 = jnp.bfloat16

NORM_EPS = 1e-6
CHUNK = 64
N_HEADS = 16
HEAD_DIM = 128
KV_LATENT = 512
N_IDX_HEADS = 16
IDX_DIM = 64
TOPK_MAX = 256
POOL_WINDOWS = (2, 4, 8, 16)
POOL_GROUP = 512

LANES = 128
Q_TILE = 256
KEY_BLOCK = 256
BF16_SUBLANES = 16
V_ROWS = HEAD_DIM + BF16_SUBLANES
R_SLOTS = 4
P_SLOTS = 2
assert KEY_BLOCK % Q_TILE == 0
HALF_BITS = 16
FINE_BITS = 18
M_FLOOR = -1e30
LOG2E = math.log2(math.e)
VMEM_LIMIT = 56 * 1024 * 1024


def _cparams(sem):
    return pltpu.CompilerParams(dimension_semantics=sem, vmem_limit_bytes=VMEM_LIMIT)


def _silu(x):
    h = 0.5 * x
    return h + h * jnp.tanh(h)


def _prenorm_kernel(x_ref, g_ref, o_ref):
    x = x_ref[...]
    ms = jnp.mean(x * x, axis=-1, keepdims=True)
    o_ref[...] = (x * lax.rsqrt(ms + NORM_EPS) * g_ref[...]).astype(o_ref.dtype)


def _prenorm(x2, g, tm=256):
    T, D = x2.shape
    return pl.pallas_call(
        _prenorm_kernel,
        grid=(T // tm,),
        in_specs=[pl.BlockSpec((tm, D), lambda i: (i, 0)),
                  pl.BlockSpec((1, D), lambda i: (0, 0))],
        out_specs=pl.BlockSpec((tm, D), lambda i: (i, 0)),
        out_shape=jax.ShapeDtypeStruct((T, D), BF16),
        compiler_params=_cparams(("parallel",)),
        name="prenorm",
    )(x2, g.reshape(1, D))


_NT = (((1,), (1,)), ((), ()))


def _matmul_nt_kernel(h_ref, wt_ref, *rest, n_scaled, scale):
    o_ref = rest[-1] if len(rest) == 1 else rest[1]
    z = lax.dot_general(h_ref[...], wt_ref[...].astype(BF16), _NT, preferred_element_type=F32)
    if n_scaled:
        z = z * jnp.where(pl.program_id(1) < n_scaled, jnp.float32(scale), jnp.float32(1.0))
    o_ref[...] = z.astype(o_ref.dtype)
    if len(rest) == 3:
        side_ref, _, side_out_ref = rest
        side_out_ref[...] = side_ref[...].astype(side_out_ref.dtype)


def _matmul_nt(h, wt, w_spec, n_out, out_dtype, tm, tn, name, n_scaled=0, scale=1.0, side=None):
    T, K = h.shape
    tm = min(tm, T)
    n_i, n_j = T // tm, n_out // tn
    in_specs = [pl.BlockSpec((tm, K), lambda i, j: (i, 0)), w_spec]
    out_specs = [pl.BlockSpec((tm, tn), lambda i, j: (i, j))]
    out_shape = [jax.ShapeDtypeStruct((T, n_out), out_dtype)]
    args = [h, wt]
    if side is not None:
        R, C = side.shape
        nb = 1
        while nb * 2 <= min(n_i * n_j, R // BF16_SUBLANES):
            nb *= 2
        assert R % nb == 0
        side_map = lambda i, j: (jnp.minimum(i * n_j + j, nb - 1), 0)
        in_specs.append(pl.BlockSpec((R // nb, C), side_map))
        out_specs.append(pl.BlockSpec((R // nb, C), side_map))
        out_shape.append(jax.ShapeDtypeStruct((R, C), BF16))
        args.append(side)
    res = pl.pallas_call(
        functools.partial(_matmul_nt_kernel, n_scaled=n_scaled, scale=scale),
        grid=(n_i, n_j),
        in_specs=in_specs,
        out_specs=out_specs,
        out_shape=out_shape,
        compiler_params=_cparams(("parallel", "arbitrary")),
        name=name,
    )(*args)
    return res if side is not None else res[0]


def _latent_kernel(h_ref, wc_ref, wkw_ref, kvg_ref, lng_ref, lnb_ref, wuk_ref, wuvt_ref,
                   kh_ref, vt_ref, kix_ref, wixt_ref):
    h = h_ref[...]
    c = lax.dot_general(h, wc_ref[...].astype(BF16), _NT, preferred_element_type=F32)
    cn = c * lax.rsqrt(jnp.mean(c * c, axis=-1, keepdims=True) + NORM_EPS) * kvg_ref[...]
    cb = cn.astype(BF16)
    n_kb = kh_ref.shape[0]
    for pair in range(N_HEADS // 2):
        k2 = jnp.dot(cb, wuk_ref[:, pair * 2 * HEAD_DIM:(pair + 1) * 2 * HEAD_DIM],
                     preferred_element_type=F32)
        for u in range(n_kb):
            for e in range(2):
                kh_ref[u, 2 * pair + e] = k2[u * KEY_BLOCK:(u + 1) * KEY_BLOCK,
                                             e * HEAD_DIM:(e + 1) * HEAD_DIM].astype(kh_ref.dtype)
    for u in range(n_kb):
        ct_u = cn[u * KEY_BLOCK:(u + 1) * KEY_BLOCK, :].T.astype(BF16)
        v_all = jnp.dot(wuvt_ref[...], ct_u, preferred_element_type=F32)
        for hh in range(N_HEADS):
            vt_ref[u, hh, :HEAD_DIM, :] = v_all[hh * HEAD_DIM:(hh + 1) * HEAD_DIM, :].astype(vt_ref.dtype)
            vt_ref[u, hh, HEAD_DIM:, :] = jnp.ones((V_ROWS - HEAD_DIM, KEY_BLOCK), vt_ref.dtype)
    kw = lax.dot_general(h, wkw_ref[...].astype(BF16), _NT, preferred_element_type=F32)
    lane = lax.broadcasted_iota(jnp.int32, kw.shape, 1)
    is_k = lane < IDX_DIM
    mu = jnp.sum(jnp.where(is_k, kw, 0.0), axis=-1, keepdims=True) * (1.0 / IDX_DIM)
    d = jnp.where(is_k, kw - mu, 0.0)
    var = jnp.sum(d * d, axis=-1, keepdims=True) * (1.0 / IDX_DIM)
    kn = d * lax.rsqrt(var + NORM_EPS) * lng_ref[...] + lnb_ref[...]
    kix_ref[...] = jnp.concatenate([kn, pltpu.roll(kn, IDX_DIM, 1)], axis=1).astype(kix_ref.dtype)
    wixt_ref[...] = kw.T[IDX_DIM:IDX_DIM + N_IDX_HEADS, :] * (N_IDX_HEADS ** -0.5)


def _latent_proj(h, wt, o_c, o_k, kv_norm, ln_g, ln_b, w_uk_all, w_uvt_all, B, S, tm=512):
    T, K = h.shape
    nS = S // tm
    pad = jnp.zeros((LANES - IDX_DIM,), F32)
    lng = jnp.concatenate([ln_g, pad]).reshape(1, LANES)
    lnb = jnp.concatenate([ln_b, pad]).reshape(1, LANES)
    assert o_c % KV_LATENT == 0 and o_k % LANES == 0
    return pl.pallas_call(
        _latent_kernel,
        grid=(B, nS),
        in_specs=[pl.BlockSpec((tm, K), lambda b, i: (b * nS + i, 0)),
                  pl.BlockSpec((KV_LATENT, K), lambda b, i: (o_c // KV_LATENT, 0)),
                  pl.BlockSpec((LANES, K), lambda b, i: (o_k // LANES, 0)),
                  pl.BlockSpec((1, KV_LATENT), lambda b, i: (0, 0)),
                  pl.BlockSpec((1, LANES), lambda b, i: (0, 0)),
                  pl.BlockSpec((1, LANES), lambda b, i: (0, 0)),
                  pl.BlockSpec(w_uk_all.shape, lambda b, i: (0, 0)),
                  pl.BlockSpec(w_uvt_all.shape, lambda b, i: (0, 0))],
        out_specs=[pl.BlockSpec((None, tm // KEY_BLOCK, N_HEADS, KEY_BLOCK, HEAD_DIM),
                                lambda b, i: (b, i, 0, 0, 0)),
                   pl.BlockSpec((None, tm // KEY_BLOCK, N_HEADS, V_ROWS, KEY_BLOCK),
                                lambda b, i: (b, i, 0, 0, 0)),
                   pl.BlockSpec((None, tm, 2 * LANES), lambda b, i: (b, i, 0)),
                   pl.BlockSpec((None, N_IDX_HEADS, tm), lambda b, i: (b, 0, i))],
        out_shape=[jax.ShapeDtypeStruct((B, S // KEY_BLOCK, N_HEADS, KEY_BLOCK, HEAD_DIM), BF16),
                   jax.ShapeDtypeStruct((B, S // KEY_BLOCK, N_HEADS, V_ROWS, KEY_BLOCK), BF16),
                   jax.ShapeDtypeStruct((B, S, 2 * LANES), BF16),
                   jax.ShapeDtypeStruct((B, N_IDX_HEADS, S), F32)],
        compiler_params=_cparams(("parallel", "arbitrary")),
        name="latent_proj",
    )(h, wt, wt, kv_norm.reshape(1, KV_LATENT), lng, lnb, w_uk_all, w_uvt_all)


def _sortable_to_f32(key):
    bits = key ^ ((key >> 31) & jnp.int32(0x7FFFFFFF))
    return pltpu.bitcast(bits, F32)


def _fold8(a):
    return jnp.sum(a.reshape(a.shape[0] // 8, 8, a.shape[1]), axis=0)


def _attn_kernel(q_ref, qi_ref, wixt_ref, kix_ref, kh_ref, vt_ref, ga_ref,
                 o_ref, isc_ref, hsc_ref, bias_ref, acc_ref, m_ref, *stage_refs, k_top):
    r_ref, p_ref = stage_refs[:R_SLOTS], stage_refs[R_SLOTS:]
    i = pl.program_id(1)
    t0 = i * Q_TILE
    nkb = (t0 + Q_TILE + KEY_BLOCK - 1) // KEY_BLOCK
    neg_inf = jnp.float32(-jnp.inf)

    t_idx = t0 + lax.broadcasted_iota(jnp.int32, (KEY_BLOCK, Q_TILE), 1)
    q_chunk = t_idx // CHUNK

    def admissible(kb):
        s_idx = kb * KEY_BLOCK + lax.broadcasted_iota(jnp.int32, (KEY_BLOCK, Q_TILE), 0)
        return (s_idx // CHUNK) <= q_chunk

    wix = wixt_ref[...] * (IDX_DIM ** -0.5)

    def idx_body(kb, carry):
        r0 = pl.multiple_of(kb * KEY_BLOCK, KEY_BLOCK)
        kx = kix_ref[pl.ds(r0, KEY_BLOCK), :]
        kx_even, kx_odd = kx[:, :LANES], kx[:, LANES:]
        acc = jnp.zeros((KEY_BLOCK, Q_TILE), F32)
        for pair in range(N_IDX_HEADS // 2):
            rhs = qi_ref[:, pair * LANES:(pair + 1) * LANES]
            for par, kxp in ((0, kx_even), (1, kx_odd)):
                h = 2 * pair + par
                d = lax.dot_general(kxp, rhs, (((1,), (1,)), ((), ())), preferred_element_type=F32)
                acc = acc + wix[h:h + 1, :] * jnp.maximum(d, 0.0)
        isc_ref[pl.ds(r0, KEY_BLOCK), :] = acc
        hsc_ref[pl.ds(r0, KEY_BLOCK), :] = acc.astype(BF16)
        return carry

    lax.fori_loop(0, nkb, idx_body, 0)
    diag_r0 = pl.multiple_of((nkb - 1) * KEY_BLOCK, KEY_BLOCK)
    diag = jnp.where(admissible(nkb - 1), isc_ref[pl.ds(diag_r0, KEY_BLOCK), :], neg_inf)
    isc_ref[pl.ds(diag_r0, KEY_BLOCK), :] = diag
    hsc_ref[pl.ds(diag_r0, KEY_BLOCK), :] = diag.astype(BF16)

    def count(pred):
        def body(kb, cnt):
            r0 = pl.multiple_of(kb * KEY_BLOCK, KEY_BLOCK)
            v = isc_ref[pl.ds(r0, KEY_BLOCK), :]
            return cnt + _fold8(jnp.where(pred(v), 1, 0).astype(jnp.int32))
        c8 = lax.fori_loop(0, nkb, body, jnp.zeros((8, Q_TILE), jnp.int32))
        return jnp.sum(c8, axis=0, keepdims=True)

    def count_half(cand):
        one, zero = jnp.ones((), BF16), jnp.zeros((), BF16)

        def body(kb, cnt):
            r0 = pl.multiple_of(kb * KEY_BLOCK, KEY_BLOCK)
            ind = jnp.where(hsc_ref[pl.ds(r0, KEY_BLOCK), :] >= cand, one, zero)
            parts = [ind[k * BF16_SUBLANES:(k + 1) * BF16_SUBLANES, :]
                     for k in range(KEY_BLOCK // BF16_SUBLANES)]
            while len(parts) > 1:
                parts = [a + b for a, b in zip(parts[0::2], parts[1::2])]
            return cnt + parts[0].astype(F32)

        c16 = lax.fori_loop(0, nkb, body, jnp.zeros((BF16_SUBLANES, Q_TILE), F32))
        return jnp.sum(c16, axis=0, keepdims=True)

    def half_to_bf16(k16):
        bits = k16 ^ ((k16 >> 31) & jnp.int32(0x7FFF))
        return pltpu.bitcast(jnp.left_shift(bits, 16), F32).astype(BF16)

    def coarse_body(step, k16):
        cand = k16 + jnp.left_shift(jnp.int32(1), HALF_BITS - 1 - step)
        return jnp.where(count_half(half_to_bf16(cand)) >= k_top, cand, k16)

    def fine_body(step, carry):
        key, key_cnt = carry
        cand = key + jnp.left_shift(jnp.int32(1), FINE_BITS - 1 - step)
        n_ge = count(lambda v: v >= _sortable_to_f32(cand))
        take = n_ge >= k_top
        return jnp.where(take, cand, key), jnp.where(take, n_ge, key_cnt)

    searching = t0 + Q_TILE > k_top
    k16 = lax.fori_loop(0, jnp.where(searching, HALF_BITS, 0), coarse_body,
                        jnp.full((1, Q_TILE), -2 ** (HALF_BITS - 1), jnp.int32))
    key_lo = jnp.left_shift(k16, 16) - jnp.int32(1 << 16)
    key, key_cnt = lax.fori_loop(0, jnp.where(searching, FINE_BITS, 0), fine_body,
                                 (key_lo, jnp.full((1, Q_TILE), k_top + 1, jnp.int32)))
    n_adm = (t_idx[0:1, :] // CHUNK + 1) * CHUNK
    take_all = n_adm <= k_top
    thr = jnp.where(take_all, neg_inf, _sortable_to_f32(key))
    has_ties = jnp.max(jnp.where(take_all, 0, key_cnt - k_top)) > 0

    @pl.when(jnp.logical_not(has_ties))
    def _():
        def sel_body(kb, carry):
            r0 = pl.multiple_of(kb * KEY_BLOCK, KEY_BLOCK)
            bias_ref[pl.ds(r0, KEY_BLOCK), :] = jnp.where(isc_ref[pl.ds(r0, KEY_BLOCK), :] >= thr,
                                                          0.0, neg_inf)
            return carry

        lax.fori_loop(0, nkb - 1, sel_body, 0)
        sel = (isc_ref[pl.ds(diag_r0, KEY_BLOCK), :] >= thr) & admissible(nkb - 1)
        bias_ref[pl.ds(diag_r0, KEY_BLOCK), :] = jnp.where(sel, 0.0, neg_inf)

    @pl.when(has_ties)
    def _():
        n_gt = count(lambda v: v > thr)
        need = (k_top - n_gt).astype(F32)
        tri = (lax.broadcasted_iota(jnp.int32, (KEY_BLOCK, KEY_BLOCK), 1)
               <= lax.broadcasted_iota(jnp.int32, (KEY_BLOCK, KEY_BLOCK), 0)).astype(BF16)

        def sel_body(kb, tie_carry):
            r0 = pl.multiple_of(kb * KEY_BLOCK, KEY_BLOCK)
            v = isc_ref[pl.ds(r0, KEY_BLOCK), :]
            tie = (v == thr).astype(BF16)
            rank = jnp.dot(tri, tie, preferred_element_type=F32) + tie_carry
            sel = ((v > thr) | ((v == thr) & (rank <= need))) & admissible(kb)
            bias_ref[pl.ds(r0, KEY_BLOCK), :] = jnp.where(sel, 0.0, neg_inf)
            return rank[KEY_BLOCK - 1:KEY_BLOCK, :]

        lax.fori_loop(0, nkb, sel_body, jnp.zeros((1, Q_TILE), F32))

    acc_ref[...] = jnp.zeros_like(acc_ref)
    m_ref[...] = jnp.full_like(m_ref, M_FLOOR)

    def scores_stage(kb, h):
        r0 = pl.multiple_of(kb * KEY_BLOCK, KEY_BLOCK)
        s = lax.dot_general(kh_ref[kb, h], q_ref[:, h * HEAD_DIM:(h + 1) * HEAD_DIM], _NT,
                            preferred_element_type=F32) + bias_ref[pl.ds(r0, KEY_BLOCK), :]
        r_ref[h % R_SLOTS][...] = s
        m_old = m_ref[h]
        m_blk = jnp.max(jnp.max(s.reshape(KEY_BLOCK // 8, 8, Q_TILE), axis=0), axis=0, keepdims=True)
        m_new = jnp.maximum(m_old, m_blk)
        m_ref[h] = m_new
        return m_new, jnp.exp2(m_old - m_new)

    first = scores_stage(0, 0) + scores_stage(0, 1)

    def attn_body(kb, carry):
        kb_next = jnp.minimum(kb + 1, nkb - 1)

        def pv_update(h, alpha):
            acc_ref[h] = alpha * acc_ref[h] + jnp.dot(vt_ref[kb, h], p_ref[h % P_SLOTS][...],
                                                      preferred_element_type=F32)

        stats = [carry[0:2], carry[2:4]]
        pending = None
        for h in range(N_HEADS):
            kb2, h2 = (kb, h + 2) if h + 2 < N_HEADS else (kb_next, h + 2 - N_HEADS)
            stats.append(scores_stage(kb2, h2))
            if pending is not None:
                pv_update(*pending)
            m_new, alpha = stats.pop(0)
            p_ref[h % P_SLOTS][...] = jnp.exp2(r_ref[h % R_SLOTS][...] - m_new).astype(BF16)
            pending = (h, alpha)
        pv_update(*pending)
        return stats[0] + stats[1]

    lax.fori_loop(0, nkb, attn_body, first)

    for h in range(N_HEADS):
        a = acc_ref[h]
        out_t = a[:HEAD_DIM, :] * (1.0 / a[HEAD_DIM:HEAD_DIM + 1, :])
        gate = ga_ref[:, h * HEAD_DIM:(h + 1) * HEAD_DIM].astype(F32)
        o_ref[:, h * HEAD_DIM:(h + 1) * HEAD_DIM] = (
            out_t.T * _silu(gate)).astype(o_ref.dtype)


def _sparse_attention(qqi, wixt, kix, kh, vt, gates, k_top):
    B, S, _ = qqi.shape
    DA = N_HEADS * HEAD_DIM
    DI = N_IDX_HEADS * IDX_DIM
    nq = S // Q_TILE
    return pl.pallas_call(
        functools.partial(_attn_kernel, k_top=k_top),
        grid=(B, nq),
        in_specs=[pl.BlockSpec((None, Q_TILE, DA), lambda b, i: (b, i, 0)),
                  pl.BlockSpec((None, Q_TILE, DI), lambda b, i: (b, i, DA // DI)),
                  pl.BlockSpec((None, N_IDX_HEADS, Q_TILE), lambda b, i: (b, 0, i)),
                  pl.BlockSpec((None, S, 2 * LANES), lambda b, i: (b, 0, 0)),
                  pl.BlockSpec((None,) + kh.shape[1:], lambda b, i: (b, 0, 0, 0, 0)),
                  pl.BlockSpec((None,) + vt.shape[1:], lambda b, i: (b, 0, 0, 0, 0)),
                  pl.BlockSpec((Q_TILE, DA), lambda b, i: (b * nq + i, 0))],
        out_specs=pl.BlockSpec((Q_TILE, DA), lambda b, i: (b * nq + i, 0)),
        out_shape=jax.ShapeDtypeStruct((B * S, DA), BF16),
        scratch_shapes=[pltpu.VMEM((S, Q_TILE), F32),
                        pltpu.VMEM((S, Q_TILE), BF16),
                        pltpu.VMEM((S, Q_TILE), F32),
                        pltpu.VMEM((N_HEADS, V_ROWS, Q_TILE), F32),
                        pltpu.VMEM((N_HEADS, 1, Q_TILE), F32)]
                       + [pltpu.VMEM((KEY_BLOCK, Q_TILE), F32)] * R_SLOTS
                       + [pltpu.VMEM((KEY_BLOCK, Q_TILE), BF16)] * P_SLOTS,
        compiler_params=_cparams(("parallel", "arbitrary")),
        name="sparse_attention",
    )(qqi, qqi, wixt, kix, kh, vt, gates)


POOL_HALO = 16


def _pool_bands(ts, dtype):
    r = lax.broadcasted_iota(jnp.int32, (ts, ts), 0)
    c = lax.broadcasted_iota(jnp.int32, (ts, ts), 1)
    hr = lax.broadcasted_iota(jnp.int32, (POOL_HALO, POOL_HALO), 0)
    hc = lax.broadcasted_iota(jnp.int32, (POOL_HALO, POOL_HALO), 1)
    bands = [jnp.where((c <= r) & (c > r - win), 1.0 / win, 0.0) - jnp.where(c == r, 1.0, 0.0)
             for win in POOL_WINDOWS]
    hbands = [jnp.where(hc >= hr + (POOL_HALO + 1 - win), 1.0 / win, 0.0) for win in POOL_WINDOWS]
    return jnp.stack(bands).astype(dtype), jnp.stack(hbands).astype(dtype)


def _pool_kernel(pin_ref, halo_ref, gate_ref, wp_ref, ps_ref, band_ref, hband_ref, o_ref, mix_ref):
    i = pl.program_id(1)
    ts = pin_ref.shape[0]

    @pl.when(i == 0)
    def _():
        t = lax.broadcasted_iota(jnp.int32, (ts, POOL_GROUP), 0)
        for g, win in enumerate(POOL_WINDOWS):
            cols = slice(g * POOL_GROUP, (g + 1) * POOL_GROUP)
            cur = pin_ref[:, cols].astype(F32)
            ssum = jnp.concatenate([jnp.zeros((POOL_HALO, POOL_GROUP), F32), cur], axis=0)
            span = 1
            while span < win:
                ssum = ssum + pltpu.roll(ssum, span, 0)
                span *= 2
            count = jnp.minimum(t + 1, win).astype(F32)
            mix_ref[:, cols] = (ssum[POOL_HALO:, :] / count - cur).astype(mix_ref.dtype)

    @pl.when(i > 0)
    def _():
        for g in range(len(POOL_WINDOWS)):
            cols = slice(g * POOL_GROUP, (g + 1) * POOL_GROUP)
            m = jnp.dot(band_ref[g], pin_ref[:, cols], preferred_element_type=F32)
            top = m[:POOL_HALO, :] + jnp.dot(hband_ref[g], halo_ref[:, cols], preferred_element_type=F32)
            mix_ref[:POOL_HALO, cols] = top.astype(mix_ref.dtype)
            mix_ref[POOL_HALO:, cols] = m[POOL_HALO:, :].astype(mix_ref.dtype)

    for g in range(len(POOL_WINDOWS)):
        cols = slice(g * POOL_GROUP, (g + 1) * POOL_GROUP)
        y = jnp.dot(mix_ref[:, cols], wp_ref[g], preferred_element_type=F32)
        gate = gate_ref[:, cols].astype(F32)
        o_ref[:, cols] = (y * ps_ref[:, cols] * _silu(gate)).astype(o_ref.dtype)


def _pool_mixer(gates, w_pool, pool_scale, B, S, ts=256):
    T = gates.shape[0]
    D = gates.shape[1] // 3
    nS = S // ts
    hb = ts // POOL_HALO
    bands, hbands = _pool_bands(ts, gates.dtype)
    return pl.pallas_call(
        _pool_kernel,
        grid=(B, nS),
        in_specs=[pl.BlockSpec((ts, D), lambda b, i: (b * nS + i, 1)),
                  pl.BlockSpec((POOL_HALO, D), lambda b, i: (jnp.maximum((b * nS + i) * hb - 1, 0), 1)),
                  pl.BlockSpec((ts, D), lambda b, i: (b * nS + i, 2)),
                  pl.BlockSpec(w_pool.shape, lambda b, i: (0, 0, 0)),
                  pl.BlockSpec((1, D), lambda b, i: (0, 0)),
                  pl.BlockSpec(bands.shape, lambda b, i: (0, 0, 0)),
                  pl.BlockSpec(hbands.shape, lambda b, i: (0, 0, 0))],
        out_specs=pl.BlockSpec((ts, D), lambda b, i: (b * nS + i, 0)),
        out_shape=jax.ShapeDtypeStruct((T, D), BF16),
        scratch_shapes=[pltpu.VMEM((ts, D), BF16)],
        compiler_params=_cparams(("parallel", "arbitrary")),
        name="pool_mixer",
    )(gates, gates, gates, w_pool, pool_scale.reshape(1, D), bands, hbands)


OUT_COL_TILE = 512
OUT_PROJ_VMEM_LIMIT = 60 * 1024 * 1024


def _outproj_kernel(ya_ref, yb_ref, w_ref, x_ref, g_ref, o_ref):
    da = ya_ref.shape[1]
    tm, d = o_ref.shape
    n_col = d // OUT_COL_TILE
    ss = jnp.zeros((tm, LANES), F32)
    for jj in range(n_col):
        cols = slice(jj * OUT_COL_TILE, (jj + 1) * OUT_COL_TILE)
        y = (jnp.dot(ya_ref[...], w_ref[:da, cols], preferred_element_type=F32)
             + jnp.dot(yb_ref[...], w_ref[da:, cols], preferred_element_type=F32))
        o_ref[:, cols] = y
        for k in range(OUT_COL_TILE // LANES):
            yk = y[:, k * LANES:(k + 1) * LANES]
            ss = ss + yk * yk
    rs = lax.rsqrt(jnp.sum(ss, axis=-1, keepdims=True) * (1.0 / d) + NORM_EPS)
    for jj in range(n_col):
        cols = slice(jj * OUT_COL_TILE, (jj + 1) * OUT_COL_TILE)
        o_ref[:, cols] = x_ref[:, cols] + o_ref[:, cols] * rs * g_ref[:, cols]


def _out_proj(ya, yb, w_out_b, x2, post_g, tm=256):
    T, D = x2.shape
    DA = ya.shape[1]
    DB = yb.shape[1]
    return pl.pallas_call(
        _outproj_kernel,
        grid=(T // tm,),
        in_specs=[pl.BlockSpec((tm, DA), lambda i: (i, 0)),
                  pl.BlockSpec((tm, DB), lambda i: (i, 0)),
                  pl.BlockSpec((DA + DB, D), lambda i: (0, 0), pipeline_mode=pl.Buffered(1)),
                  pl.BlockSpec((tm, D), lambda i: (i, 0)),
                  pl.BlockSpec((1, D), lambda i: (0, 0))],
        out_specs=pl.BlockSpec((tm, D), lambda i: (i, 0)),
        out_shape=jax.ShapeDtypeStruct((T, D), F32),
        compiler_params=pltpu.CompilerParams(dimension_semantics=("parallel",),
                                             vmem_limit_bytes=OUT_PROJ_VMEM_LIMIT),
        name="out_proj",
    )(ya, yb, w_out_b, x2, post_g.reshape(1, D))


PROJ_TN = 512


def _layer(x2, B, S, pre_g, w_in, kv_g, w_uk, w_uv, ln_g, ln_b, w_pool, pool_scale, w_out, post_g, k_top):
    d_attn = N_HEADS * HEAD_DIM
    n_qi = N_IDX_HEADS * IDX_DIM
    o_c = d_attn
    o_qi = o_c + KV_LATENT
    o_k = o_qi + n_qi
    o_w = o_k + IDX_DIM
    o_g = o_w + N_IDX_HEADS
    K, n_in = w_in.shape
    wt = w_in.T
    w_uk_all = jnp.transpose(w_uk, (1, 0, 2)).reshape(KV_LATENT, d_attn).astype(BF16)
    w_uvt_all = jnp.swapaxes(w_uv, 1, 2).reshape(d_attn, KV_LATENT).astype(BF16)
    w_pool_b = w_pool.astype(BF16)

    h = _prenorm(x2, pre_g)
    tn = PROJ_TN
    assert o_c % tn == 0 and o_qi % tn == 0 and n_qi % tn == 0
    qqi_spec = pl.BlockSpec((tn, K), lambda i, j: (jnp.where(j < o_c // tn, j, j + (o_qi - o_c) // tn), 0))
    qqi = _matmul_nt(h, wt, qqi_spec, d_attn + n_qi, BF16, 1024, tn, "proj_q_qi",
                     n_scaled=o_c // tn, scale=(HEAD_DIM ** -0.5) * LOG2E)
    assert o_g % 16 == 0
    gates_spec = pl.BlockSpec((pl.Element(tn), pl.Element(K)),
                              lambda i, j: (pl.multiple_of(o_g + j * tn, 16), 0))
    gates, w_out_b = _matmul_nt(h, wt, gates_spec, n_in - o_g, BF16, 1024, tn, "proj_gates", side=w_out)
    kh, vt, kix, wixt = _latent_proj(h, wt, o_c, o_k, kv_g, ln_g, ln_b, w_uk_all, w_uvt_all, B, S)
    ya = _sparse_attention(qqi.reshape(B, S, d_attn + n_qi), wixt, kix, kh, vt, gates, k_top)
    yb = _pool_mixer(gates, w_pool_b, pool_scale, B, S)
    return _out_proj(ya, yb, w_out_b, x2, post_g)


def kernel(x, pre_norm, w_in, kv_norm, w_uk, w_uv, idx_k_norm_g, idx_k_norm_b, w_pool, pool_scale, w_out, post_norm):
    B, S, D = x.shape
    k_top = min(TOPK_MAX, S // 4)
    x2 = x.reshape(B * S, D)
    for l in range(pre_norm.shape[0]):
        x2 = _layer(x2, B, S, pre_norm[l], w_in[l], kv_norm[l], w_uk[l], w_uv[l], idx_k_norm_g[l],
                    idx_k_norm_b[l], w_pool[l], pool_scale[l], w_out[l], post_norm[l], k_top)
    return x2.reshape(B, S, D)
```

```python
import functools
import math

import jax
import jax.numpy as jnp
from jax import lax
from jax.experimental import pallas as pl
from jax.experimental.pallas import tpu as pltpu

F32 = jnp.float32
BF16 = jnp.bfloat16

NORM_EPS = 1e-6
CHUNK = 64
N_HEADS = 16
HEAD_DIM = 128
KV_LATENT = 512
N_IDX_HEADS = 16
IDX_DIM = 64
TOPK_MAX = 256
POOL_WINDOWS = (2, 4, 8, 16)
POOL_GROUP = 512

LANES = 128
Q_TILE = 256
KEY_BLOCK = 256
BF16_SUBLANES = 16
V_ROWS = HEAD_DIM + BF16_SUBLANES
R_SLOTS = 4
P_SLOTS = 2
assert KEY_BLOCK % Q_TILE == 0
HALF_BITS = 16
FINE_BITS = 18
M_FLOOR = -1e30
LOG2E = math.log2(math.e)
VMEM_LIMIT = 56 * 1024 * 1024


def _cparams(sem):
    return pltpu.CompilerParams(dimension_semantics=sem, vmem_limit_bytes=VMEM_LIMIT)


def _silu(x):
    h = 0.5 * x
    return h + h * jnp.tanh(h)


def _rms_rows(x, g):
    ms = jnp.mean(x * x, axis=-1, keepdims=True)
    return (x * lax.rsqrt(ms + NORM_EPS) * g).astype(BF16)


def _prenorm_kernel(x_ref, g_ref, o_ref):
    o_ref[...] = _rms_rows(x_ref[...], g_ref[...])


def _prenorm(x2, g, rows, tm=256):
    D = x2.shape[1]
    return pl.pallas_call(
        _prenorm_kernel,
        grid=(rows // tm,),
        in_specs=[pl.BlockSpec((tm, D), lambda i: (i, 0)),
                  pl.BlockSpec((1, D), lambda i: (0, 0))],
        out_specs=pl.BlockSpec((tm, D), lambda i: (i, 0)),
        out_shape=jax.ShapeDtypeStruct((rows, D), BF16),
        compiler_params=_cparams(("parallel",)),
        name="prenorm",
    )(x2, g.reshape(1, D))


_NT = (((1,), (1,)), ((), ()))


def _matmul_nt_kernel(h_ref, wt_ref, *rest, n_scaled, scale):
    o_ref = rest[-1] if len(rest) == 1 else rest[1]
    z = lax.dot_general(h_ref[...], wt_ref[...].astype(BF16), _NT, preferred_element_type=F32)
    if n_scaled:
        z = z * jnp.where(pl.program_id(1) < n_scaled, jnp.float32(scale), jnp.float32(1.0))
    o_ref[...] = z.astype(o_ref.dtype)
    if len(rest) == 3:
        side_ref, _, side_out_ref = rest
        side_out_ref[...] = side_ref[...].astype(side_out_ref.dtype)


def _matmul_nt(h, wt, w_spec, n_out, out_dtype, tm, tn, name, n_scaled=0, scale=1.0, side=None):
    T, K = h.shape
    tm = min(tm, T)
    n_i, n_j = T // tm, n_out // tn
    in_specs = [pl.BlockSpec((tm, K), lambda i, j: (i, 0)), w_spec]
    out_specs = [pl.BlockSpec((tm, tn), lambda i, j: (i, j))]
    out_shape = [jax.ShapeDtypeStruct((T, n_out), out_dtype)]
    args = [h, wt]
    if side is not None:
        R, C = side.shape
        nb = 1
        while nb * 2 <= min(n_i * n_j, R // BF16_SUBLANES):
            nb *= 2
        assert R % nb == 0
        side_map = lambda i, j: (jnp.minimum(i * n_j + j, nb - 1), 0)
        in_specs.append(pl.BlockSpec((R // nb, C), side_map))
        out_specs.append(pl.BlockSpec((R // nb, C), side_map))
        out_shape.append(jax.ShapeDtypeStruct((R, C), BF16))
        args.append(side)
    res = pl.pallas_call(
        functools.partial(_matmul_nt_kernel, n_scaled=n_scaled, scale=scale),
        grid=(n_i, n_j),
        in_specs=in_specs,
        out_specs=out_specs,
        out_shape=out_shape,
        compiler_params=_cparams(("parallel", "arbitrary")),
        name=name,
    )(*args)
    return res if side is not None else res[0]


NORM_SEG = 256
NORM_PROJ_VMEM_LIMIT = 60 * 1024 * 1024


def _norm_proj_kernel(x_ref, g_ref, h0_ref, wt_ref, o_ref, hall_ref, hbuf_a, hbuf_b, *,
                      n_seg, n_scaled, scale):
    i, j = pl.program_id(0), pl.program_id(1)

    @pl.when((i == 0) & (j == 0))
    def _():
        hbuf_a[...] = h0_ref[...]

    def step(cur_ref, nxt_ref, with_norm):
        z = lax.dot_general(cur_ref[...], wt_ref[...].astype(BF16), _NT, preferred_element_type=F32)
        z = z * jnp.where(j < n_scaled, jnp.float32(scale), jnp.float32(1.0))
        o_ref[...] = z.astype(o_ref.dtype)
        if with_norm:
            hseg = _rms_rows(x_ref[...], g_ref[...])
            hall_ref[...] = hseg
            nxt_ref[pl.ds(pl.multiple_of(j * NORM_SEG, NORM_SEG), NORM_SEG), :] = hseg

    for parity, (cur_ref, nxt_ref) in enumerate(((hbuf_a, hbuf_b), (hbuf_b, hbuf_a))):
        for with_norm in (True, False):
            @pl.when((i % 2 == parity) & ((j < n_seg) == with_norm))
            def _(cur_ref=cur_ref, nxt_ref=nxt_ref, with_norm=with_norm):
                step(cur_ref, nxt_ref, with_norm)


def _norm_proj(x2, g, h0, wt, w_spec, n_out, tm, tn, n_scaled, scale):
    T, D = x2.shape
    n_i, n_j = T // tm, n_out // tn
    n_seg = tm // NORM_SEG
    assert n_seg <= n_j and h0.shape == (tm, D)
    seg_map = lambda i, j: (((i + 1) % n_i) * n_seg + jnp.minimum(j, n_seg - 1), 0)
    return pl.pallas_call(
        functools.partial(_norm_proj_kernel, n_seg=n_seg, n_scaled=n_scaled, scale=scale),
        grid=(n_i, n_j),
        in_specs=[pl.BlockSpec((NORM_SEG, D), seg_map),
                  pl.BlockSpec((1, D), lambda i, j: (0, 0)),
                  pl.BlockSpec((tm, D), lambda i, j: (0, 0), pipeline_mode=pl.Buffered(1)),
                  w_spec],
        out_specs=[pl.BlockSpec((tm, tn), lambda i, j: (i, j)),
                   pl.BlockSpec((NORM_SEG, D), seg_map)],
        out_shape=[jax.ShapeDtypeStruct((T, n_out), BF16),
                   jax.ShapeDtypeStruct((T, D), BF16)],
        scratch_shapes=[pltpu.VMEM((tm, D), BF16), pltpu.VMEM((tm, D), BF16)],
        compiler_params=pltpu.CompilerParams(dimension_semantics=("arbitrary", "arbitrary"),
                                             vmem_limit_bytes=NORM_PROJ_VMEM_LIMIT),
        name="norm_proj_q_qi",
    )(x2, g.reshape(1, D), h0, wt)


def _latent_kernel(h_ref, wc_ref, wkw_ref, kvg_ref, lng_ref, lnb_ref, wuk_ref, wuvt_ref,
                   kh_ref, vt_ref, kix_ref, wixt_ref):
    h = h_ref[...]
    c = lax.dot_general(h, wc_ref[...].astype(BF16), _NT, preferred_element_type=F32)
    cn = c * lax.rsqrt(jnp.mean(c * c, axis=-1, keepdims=True) + NORM_EPS) * kvg_ref[...]
    cb = cn.astype(BF16)
    n_kb = kh_ref.shape[0]
    for pair in range(N_HEADS // 2):
        k2 = jnp.dot(cb, wuk_ref[:, pair * 2 * HEAD_DIM:(pair + 1) * 2 * HEAD_DIM],
                     preferred_element_type=F32)
        for u in range(n_kb):
            for e in range(2):
                kh_ref[u, 2 * pair + e] = k2[u * KEY_BLOCK:(u + 1) * KEY_BLOCK,
                                             e * HEAD_DIM:(e + 1) * HEAD_DIM].astype(kh_ref.dtype)
    for u in range(n_kb):
        ct_u = cn[u * KEY_BLOCK:(u + 1) * KEY_BLOCK, :].T.astype(BF16)
        v_all = jnp.dot(wuvt_ref[...], ct_u, preferred_element_type=F32)
        for hh in range(N_HEADS):
            vt_ref[u, hh, :HEAD_DIM, :] = v_all[hh * HEAD_DIM:(hh + 1) * HEAD_DIM, :].astype(vt_ref.dtype)
            vt_ref[u, hh, HEAD_DIM:, :] = jnp.ones((V_ROWS - HEAD_DIM, KEY_BLOCK), vt_ref.dtype)
    kw = lax.dot_general(h, wkw_ref[...].astype(BF16), _NT, preferred_element_type=F32)
    lane = lax.broadcasted_iota(jnp.int32, kw.shape, 1)
    is_k = lane < IDX_DIM
    mu = jnp.sum(jnp.where(is_k, kw, 0.0), axis=-1, keepdims=True) * (1.0 / IDX_DIM)
    d = jnp.where(is_k, kw - mu, 0.0)
    var = jnp.sum(d * d, axis=-1, keepdims=True) * (1.0 / IDX_DIM)
    kn = d * lax.rsqrt(var + NORM_EPS) * lng_ref[...] + lnb_ref[...]
    kix_ref[...] = jnp.concatenate([kn, pltpu.roll(kn, IDX_DIM, 1)], axis=1).astype(kix_ref.dtype)
    wixt_ref[...] = kw.T[IDX_DIM:IDX_DIM + N_IDX_HEADS, :] * (N_IDX_HEADS ** -0.5)


def _latent_proj(h, wt, o_c, o_k, kv_norm, ln_g, ln_b, w_uk_all, w_uvt_all, B, S, tm=512):
    T, K = h.shape
    nS = S // tm
    pad = jnp.zeros((LANES - IDX_DIM,), F32)
    lng = jnp.concatenate([ln_g, pad]).reshape(1, LANES)
    lnb = jnp.concatenate([ln_b, pad]).reshape(1, LANES)
    assert o_c % KV_LATENT == 0 and o_k % LANES == 0
    return pl.pallas_call(
        _latent_kernel,
        grid=(B, nS),
        in_specs=[pl.BlockSpec((tm, K), lambda b, i: (b * nS + i, 0)),
                  pl.BlockSpec((KV_LATENT, K), lambda b, i: (o_c // KV_LATENT, 0)),
                  pl.BlockSpec((LANES, K), lambda b, i: (o_k // LANES, 0)),
                  pl.BlockSpec((1, KV_LATENT), lambda b, i: (0, 0)),
                  pl.BlockSpec((1, LANES), lambda b, i: (0, 0)),
                  pl.BlockSpec((1, LANES), lambda b, i: (0, 0)),
                  pl.BlockSpec(w_uk_all.shape, lambda b, i: (0, 0)),
                  pl.BlockSpec(w_uvt_all.shape, lambda b, i: (0, 0))],
        out_specs=[pl.BlockSpec((None, tm // KEY_BLOCK, N_HEADS, KEY_BLOCK, HEAD_DIM),
                                lambda b, i: (b, i, 0, 0, 0)),
                   pl.BlockSpec((None, tm // KEY_BLOCK, N_HEADS, V_ROWS, KEY_BLOCK),
                                lambda b, i: (b, i, 0, 0, 0)),
                   pl.BlockSpec((None, tm, 2 * LANES), lambda b, i: (b, i, 0)),
                   pl.BlockSpec((None, N_IDX_HEADS, tm), lambda b, i: (b, 0, i))],
        out_shape=[jax.ShapeDtypeStruct((B, S // KEY_BLOCK, N_HEADS, KEY_BLOCK, HEAD_DIM), BF16),
                   jax.ShapeDtypeStruct((B, S // KEY_BLOCK, N_HEADS, V_ROWS, KEY_BLOCK), BF16),
                   jax.ShapeDtypeStruct((B, S, 2 * LANES), BF16),
                   jax.ShapeDtypeStruct((B, N_IDX_HEADS, S), F32)],
        compiler_params=_cparams(("parallel", "arbitrary")),
        name="latent_proj",
    )(h, wt, wt, kv_norm.reshape(1, KV_LATENT), lng, lnb, w_uk_all, w_uvt_all)


def _sortable_to_f32(key):
    bits = key ^ ((key >> 31) & jnp.int32(0x7FFFFFFF))
    return pltpu.bitcast(bits, F32)


def _fold8(a):
    return jnp.sum(a.reshape(a.shape[0] // 8, 8, a.shape[1]), axis=0)


def _attn_kernel(q_ref, qi_ref, wixt_ref, kix_ref, kh_ref, vt_ref, ga_ref,
                 o_ref, isc_ref, hsc_ref, bias_ref, acc_ref, m_ref, *stage_refs, k_top):
    r_ref, p_ref = stage_refs[:R_SLOTS], stage_refs[R_SLOTS:]
    i = pl.program_id(1)
    t0 = i * Q_TILE
    nkb = (t0 + Q_TILE + KEY_BLOCK - 1) // KEY_BLOCK
    neg_inf = jnp.float32(-jnp.inf)

    t_idx = t0 + lax.broadcasted_iota(jnp.int32, (KEY_BLOCK, Q_TILE), 1)
    q_chunk = t_idx // CHUNK

    def admissible(kb):
        s_idx = kb * KEY_BLOCK + lax.broadcasted_iota(jnp.int32, (KEY_BLOCK, Q_TILE), 0)
        return (s_idx // CHUNK) <= q_chunk

    wix = wixt_ref[...] * (IDX_DIM ** -0.5)

    def idx_body(kb, carry):
        r0 = pl.multiple_of(kb * KEY_BLOCK, KEY_BLOCK)
        kx = kix_ref[pl.ds(r0, KEY_BLOCK), :]
        kx_even, kx_odd = kx[:, :LANES], kx[:, LANES:]
        acc = jnp.zeros((KEY_BLOCK, Q_TILE), F32)
        for pair in range(N_IDX_HEADS // 2):
            rhs = qi_ref[:, pair * LANES:(pair + 1) * LANES]
            for par, kxp in ((0, kx_even), (1, kx_odd)):
                h = 2 * pair + par
                d = lax.dot_general(kxp, rhs, (((1,), (1,)), ((), ())), preferred_element_type=F32)
                acc = acc + wix[h:h + 1, :] * jnp.maximum(d, 0.0)
        isc_ref[pl.ds(r0, KEY_BLOCK), :] = acc
        hsc_ref[pl.ds(r0, KEY_BLOCK), :] = acc.astype(BF16)
        return carry

    lax.fori_loop(0, nkb, idx_body, 0)
    diag_r0 = pl.multiple_of((nkb - 1) * KEY_BLOCK, KEY_BLOCK)
    diag = jnp.where(admissible(nkb - 1), isc_ref[pl.ds(diag_r0, KEY_BLOCK), :], neg_inf)
    isc_ref[pl.ds(diag_r0, KEY_BLOCK), :] = diag
    hsc_ref[pl.ds(diag_r0, KEY_BLOCK), :] = diag.astype(BF16)

    def count(pred):
        def body(kb, cnt):
            r0 = pl.multiple_of(kb * KEY_BLOCK, KEY_BLOCK)
            v = isc_ref[pl.ds(r0, KEY_BLOCK), :]
            return cnt + _fold8(jnp.where(pred(v), 1, 0).astype(jnp.int32))
        c8 = lax.fori_loop(0, nkb, body, jnp.zeros((8, Q_TILE), jnp.int32))
        return jnp.sum(c8, axis=0, keepdims=True)

    def count_half(cand):
        one, zero = jnp.ones((), BF16), jnp.zeros((), BF16)

        def body(kb, cnt):
            r0 = pl.multiple_of(kb * KEY_BLOCK, KEY_BLOCK)
            ind = jnp.where(hsc_ref[pl.ds(r0, KEY_BLOCK), :] >= cand, one, zero)
            parts = [ind[k * BF16_SUBLANES:(k + 1) * BF16_SUBLANES, :]
                     for k in range(KEY_BLOCK // BF16_SUBLANES)]
            while len(parts) > 1:
                parts = [a + b for a, b in zip(parts[0::2], parts[1::2])]
            return cnt + parts[0].astype(F32)

        c16 = lax.fori_loop(0, nkb, body, jnp.zeros((BF16_SUBLANES, Q_TILE), F32))
        return jnp.sum(c16, axis=0, keepdims=True)

    def half_to_bf16(k16):
        bits = k16 ^ ((k16 >> 31) & jnp.int32(0x7FFF))
        return pltpu.bitcast(jnp.left_shift(bits, 16), F32).astype(BF16)

    def coarse_body(step, k16):
        cand = k16 + jnp.left_shift(jnp.int32(1), HALF_BITS - 1 - step)
        return jnp.where(count_half(half_to_bf16(cand)) >= k_top, cand, k16)

    def fine_body(step, carry):
        key, key_cnt = carry
        cand = key + jnp.left_shift(jnp.int32(1), FINE_BITS - 1 - step)
        n_ge = count(lambda v: v >= _sortable_to_f32(cand))
        take = n_ge >= k_top
        return jnp.where(take, cand, key), jnp.where(take, n_ge, key_cnt)

    searching = t0 + Q_TILE > k_top
    k16 = lax.fori_loop(0, jnp.where(searching, HALF_BITS, 0), coarse_body,
                        jnp.full((1, Q_TILE), -2 ** (HALF_BITS - 1), jnp.int32))
    key_lo = jnp.left_shift(k16, 16) - jnp.int32(1 << 16)
    key, key_cnt = lax.fori_loop(0, jnp.where(searching, FINE_BITS, 0), fine_body,
                                 (key_lo, jnp.full((1, Q_TILE), k_top + 1, jnp.int32)))
    n_adm = (t_idx[0:1, :] // CHUNK + 1) * CHUNK
    take_all = n_adm <= k_top
    thr = jnp.where(take_all, neg_inf, _sortable_to_f32(key))
    has_ties = jnp.max(jnp.where(take_all, 0, key_cnt - k_top)) > 0

    @pl.when(jnp.logical_not(has_ties))
    def _():
        def sel_body(kb, carry):
            r0 = pl.multiple_of(kb * KEY_BLOCK, KEY_BLOCK)
            bias_ref[pl.ds(r0, KEY_BLOCK), :] = jnp.where(isc_ref[pl.ds(r0, KEY_BLOCK), :] >= thr,
                                                          0.0, neg_inf)
            return carry

        lax.fori_loop(0, nkb - 1, sel_body, 0)
        sel = (isc_ref[pl.ds(diag_r0, KEY_BLOCK), :] >= thr) & admissible(nkb - 1)
        bias_ref[pl.ds(diag_r0, KEY_BLOCK), :] = jnp.where(sel, 0.0, neg_inf)

    @pl.when(has_ties)
    def _():
        n_gt = count(lambda v: v > thr)
        need = (k_top - n_gt).astype(F32)
        tri = (lax.broadcasted_iota(jnp.int32, (KEY_BLOCK, KEY_BLOCK), 1)
               <= lax.broadcasted_iota(jnp.int32, (KEY_BLOCK, KEY_BLOCK), 0)).astype(BF16)

        def sel_body(kb, tie_carry):
            r0 = pl.multiple_of(kb * KEY_BLOCK, KEY_BLOCK)
            v = isc_ref[pl.ds(r0, KEY_BLOCK), :]
            tie = (v == thr).astype(BF16)
            rank = jnp.dot(tri, tie, preferred_element_type=F32) + tie_carry
            sel = ((v > thr) | ((v == thr) & (rank <= need))) & admissible(kb)
            bias_ref[pl.ds(r0, KEY_BLOCK), :] = jnp.where(sel, 0.0, neg_inf)
            return rank[KEY_BLOCK - 1:KEY_BLOCK, :]

        lax.fori_loop(0, nkb, sel_body, jnp.zeros((1, Q_TILE), F32))

    acc_ref[...] = jnp.zeros_like(acc_ref)
    m_ref[...] = jnp.full_like(m_ref, M_FLOOR)

    def scores_stage(kb, h):
        r0 = pl.multiple_of(kb * KEY_BLOCK, KEY_BLOCK)
        s = lax.dot_general(kh_ref[kb, h], q_ref[:, h * HEAD_DIM:(h + 1) * HEAD_DIM], _NT,
                            preferred_element_type=F32) + bias_ref[pl.ds(r0, KEY_BLOCK), :]
        r_ref[h % R_SLOTS][...] = s
        m_old = m_ref[h]
        m_blk = jnp.max(jnp.max(s.reshape(KEY_BLOCK // 8, 8, Q_TILE), axis=0), axis=0, keepdims=True)
        m_new = jnp.maximum(m_old, m_blk)
        m_ref[h] = m_new
        return m_new, jnp.exp2(m_old - m_new)

    first = scores_stage(0, 0) + scores_stage(0, 1)

    def attn_body(kb, carry):
        kb_next = jnp.minimum(kb + 1, nkb - 1)

        def pv_update(h, alpha):
            acc_ref[h] = alpha * acc_ref[h] + jnp.dot(vt_ref[kb, h], p_ref[h % P_SLOTS][...],
                                                      preferred_element_type=F32)

        stats = [carry[0:2], carry[2:4]]
        pending = None
        for h in range(N_HEADS):
            kb2, h2 = (kb, h + 2) if h + 2 < N_HEADS else (kb_next, h + 2 - N_HEADS)
            stats.append(scores_stage(kb2, h2))
            if pending is not None:
                pv_update(*pending)
            m_new, alpha = stats.pop(0)
            p_ref[h % P_SLOTS][...] = jnp.exp2(r_ref[h % R_SLOTS][...] - m_new).astype(BF16)
            pending = (h, alpha)
        pv_update(*pending)
        return stats[0] + stats[1]

    lax.fori_loop(0, nkb, attn_body, first)

    for h in range(N_HEADS):
        a = acc_ref[h]
        out_t = a[:HEAD_DIM, :] * (1.0 / a[HEAD_DIM:HEAD_DIM + 1, :])
        gate = ga_ref[:, h * HEAD_DIM:(h + 1) * HEAD_DIM].astype(F32)
        o_ref[:, h * HEAD_DIM:(h + 1) * HEAD_DIM] = (
            out_t.T * _silu(gate)).astype(o_ref.dtype)


def _sparse_attention(qqi, wixt, kix, kh, vt, gates, k_top):
    B, S, _ = qqi.shape
    DA = N_HEADS * HEAD_DIM
    DI = N_IDX_HEADS * IDX_DIM
    nq = S // Q_TILE
    return pl.pallas_call(
        functools.partial(_attn_kernel, k_top=k_top),
        grid=(B, nq),
        in_specs=[pl.BlockSpec((None, Q_TILE, DA), lambda b, i: (b, i, 0)),
                  pl.BlockSpec((None, Q_TILE, DI), lambda b, i: (b, i, DA // DI)),
                  pl.BlockSpec((None, N_IDX_HEADS, Q_TILE), lambda b, i: (b, 0, i)),
                  pl.BlockSpec((None, S, 2 * LANES), lambda b, i: (b, 0, 0)),
                  pl.BlockSpec((None,) + kh.shape[1:], lambda b, i: (b, 0, 0, 0, 0)),
                  pl.BlockSpec((None,) + vt.shape[1:], lambda b, i: (b, 0, 0, 0, 0)),
                  pl.BlockSpec((Q_TILE, DA), lambda b, i: (b * nq + i, 0))],
        out_specs=pl.BlockSpec((Q_TILE, DA), lambda b, i: (b * nq + i, 0)),
        out_shape=jax.ShapeDtypeStruct((B * S, DA), BF16),
        scratch_shapes=[pltpu.VMEM((S, Q_TILE), F32),
                        pltpu.VMEM((S, Q_TILE), BF16),
                        pltpu.VMEM((S, Q_TILE), F32),
                        pltpu.VMEM((N_HEADS, V_ROWS, Q_TILE), F32),
                        pltpu.VMEM((N_HEADS, 1, Q_TILE), F32)]
                       + [pltpu.VMEM((KEY_BLOCK, Q_TILE), F32)] * R_SLOTS
                       + [pltpu.VMEM((KEY_BLOCK, Q_TILE), BF16)] * P_SLOTS,
        compiler_params=_cparams(("parallel", "arbitrary")),
        name="sparse_attention",
    )(qqi, qqi, wixt, kix, kh, vt, gates)


POOL_HALO = 16


def _pool_bands(ts, dtype):
    r = lax.broadcasted_iota(jnp.int32, (ts, ts), 0)
    c = lax.broadcasted_iota(jnp.int32, (ts, ts), 1)
    hr = lax.broadcasted_iota(jnp.int32, (POOL_HALO, POOL_HALO), 0)
    hc = lax.broadcasted_iota(jnp.int32, (POOL_HALO, POOL_HALO), 1)
    bands = [jnp.where((c <= r) & (c > r - win), 1.0 / win, 0.0) - jnp.where(c == r, 1.0, 0.0)
             for win in POOL_WINDOWS]
    hbands = [jnp.where(hc >= hr + (POOL_HALO + 1 - win), 1.0 / win, 0.0) for win in POOL_WINDOWS]
    return jnp.stack(bands).astype(dtype), jnp.stack(hbands).astype(dtype)


def _pool_kernel(pin_ref, halo_ref, gate_ref, wp_ref, ps_ref, band_ref, hband_ref, o_ref, mix_ref):
    i = pl.program_id(1)
    ts = pin_ref.shape[0]

    @pl.when(i == 0)
    def _():
        t = lax.broadcasted_iota(jnp.int32, (ts, POOL_GROUP), 0)
        for g, win in enumerate(POOL_WINDOWS):
            cols = slice(g * POOL_GROUP, (g + 1) * POOL_GROUP)
            cur = pin_ref[:, cols].astype(F32)
            ssum = jnp.concatenate([jnp.zeros((POOL_HALO, POOL_GROUP), F32), cur], axis=0)
            span = 1
            while span < win:
                ssum = ssum + pltpu.roll(ssum, span, 0)
                span *= 2
            count = jnp.minimum(t + 1, win).astype(F32)
            mix_ref[:, cols] = (ssum[POOL_HALO:, :] / count - cur).astype(mix_ref.dtype)

    @pl.when(i > 0)
    def _():
        for g in range(len(POOL_WINDOWS)):
            cols = slice(g * POOL_GROUP, (g + 1) * POOL_GROUP)
            m = jnp.dot(band_ref[g], pin_ref[:, cols], preferred_element_type=F32)
            top = m[:POOL_HALO, :] + jnp.dot(hband_ref[g], halo_ref[:, cols], preferred_element_type=F32)
            mix_ref[:POOL_HALO, cols] = top.astype(mix_ref.dtype)
            mix_ref[POOL_HALO:, cols] = m[POOL_HALO:, :].astype(mix_ref.dtype)

    for g in range(len(POOL_WINDOWS)):
        cols = slice(g * POOL_GROUP, (g + 1) * POOL_GROUP)
        y = jnp.dot(mix_ref[:, cols], wp_ref[g], preferred_element_type=F32)
        gate = gate_ref[:, cols].astype(F32)
        o_ref[:, cols] = (y * ps_ref[:, cols] * _silu(gate)).astype(o_ref.dtype)


def _pool_mixer(gates, w_pool, pool_scale, B, S, ts=256):
    T = gates.shape[0]
    D = gates.shape[1] // 3
    nS = S // ts
    hb = ts // POOL_HALO
    bands, hbands = _pool_bands(ts, gates.dtype)
    return pl.pallas_call(
        _pool_kernel,
        grid=(B, nS),
        in_specs=[pl.BlockSpec((ts, D), lambda b, i: (b * nS + i, 1)),
                  pl.BlockSpec((POOL_HALO, D), lambda b, i: (jnp.maximum((b * nS + i) * hb - 1, 0), 1)),
                  pl.BlockSpec((ts, D), lambda b, i: (b * nS + i, 2)),
                  pl.BlockSpec(w_pool.shape, lambda b, i: (0, 0, 0)),
                  pl.BlockSpec((1, D), lambda b, i: (0, 0)),
                  pl.BlockSpec(bands.shape, lambda b, i: (0, 0, 0)),
                  pl.BlockSpec(hbands.shape, lambda b, i: (0, 0, 0))],
        out_specs=pl.BlockSpec((ts, D), lambda b, i: (b * nS + i, 0)),
        out_shape=jax.ShapeDtypeStruct((T, D), BF16),
        scratch_shapes=[pltpu.VMEM((ts, D), BF16)],
        compiler_params=_cparams(("parallel", "arbitrary")),
        name="pool_mixer",
    )(gates, gates, gates, w_pool, pool_scale.reshape(1, D), bands, hbands)


OUT_COL_TILE = 512
OUT_PROJ_VMEM_LIMIT = 60 * 1024 * 1024


def _outproj_kernel(ya_ref, yb_ref, w_ref, x_ref, g_ref, o_ref):
    da = ya_ref.shape[1]
    tm, d = o_ref.shape
    n_col = d // OUT_COL_TILE
    ss = jnp.zeros((tm, LANES), F32)
    for jj in range(n_col):
        cols = slice(jj * OUT_COL_TILE, (jj + 1) * OUT_COL_TILE)
        y = (jnp.dot(ya_ref[...], w_ref[:da, cols], preferred_element_type=F32)
             + jnp.dot(yb_ref[...], w_ref[da:, cols], preferred_element_type=F32))
        o_ref[:, cols] = y
        for k in range(OUT_COL_TILE // LANES):
            yk = y[:, k * LANES:(k + 1) * LANES]
            ss = ss + yk * yk
    rs = lax.rsqrt(jnp.sum(ss, axis=-1, keepdims=True) * (1.0 / d) + NORM_EPS)
    for jj in range(n_col):
        cols = slice(jj * OUT_COL_TILE, (jj + 1) * OUT_COL_TILE)
        o_ref[:, cols] = x_ref[:, cols] + o_ref[:, cols] * rs * g_ref[:, cols]


def _out_proj(ya, yb, w_out_b, x2, post_g, tm=256):
    T, D = x2.shape
    DA = ya.shape[1]
    DB = yb.shape[1]
    return pl.pallas_call(
        _outproj_kernel,
        grid=(T // tm,),
        in_specs=[pl.BlockSpec((tm, DA), lambda i: (i, 0)),
                  pl.BlockSpec((tm, DB), lambda i: (i, 0)),
                  pl.BlockSpec((DA + DB, D), lambda i: (0, 0), pipeline_mode=pl.Buffered(1)),
                  pl.BlockSpec((tm, D), lambda i: (i, 0)),
                  pl.BlockSpec((1, D), lambda i: (0, 0))],
        out_specs=pl.BlockSpec((tm, D), lambda i: (i, 0)),
        out_shape=jax.ShapeDtypeStruct((T, D), F32),
        compiler_params=pltpu.CompilerParams(dimension_semantics=("parallel",),
                                             vmem_limit_bytes=OUT_PROJ_VMEM_LIMIT),
        name="out_proj",
    )(ya, yb, w_out_b, x2, post_g.reshape(1, D))


PROJ_TN = 512
PROJ_TM = 1024


def _layer(x2, B, S, pre_g, w_in, kv_g, w_uk, w_uv, ln_g, ln_b, w_pool, pool_scale, w_out, post_g, k_top):
    d_attn = N_HEADS * HEAD_DIM
    n_qi = N_IDX_HEADS * IDX_DIM
    o_c = d_attn
    o_qi = o_c + KV_LATENT
    o_k = o_qi + n_qi
    o_w = o_k + IDX_DIM
    o_g = o_w + N_IDX_HEADS
    K, n_in = w_in.shape
    wt = w_in.T
    w_uk_all = jnp.transpose(w_uk, (1, 0, 2)).reshape(KV_LATENT, d_attn).astype(BF16)
    w_uvt_all = jnp.swapaxes(w_uv, 1, 2).reshape(d_attn, KV_LATENT).astype(BF16)
    w_pool_b = w_pool.astype(BF16)

    tn = PROJ_TN
    tm = min(PROJ_TM, x2.shape[0])
    assert o_c % tn == 0 and o_qi % tn == 0 and n_qi % tn == 0
    qqi_spec = pl.BlockSpec((tn, K), lambda i, j: (jnp.where(j < o_c // tn, j, j + (o_qi - o_c) // tn), 0))
    h0 = _prenorm(x2, pre_g, tm)
    qqi, h = _norm_proj(x2, pre_g, h0, wt, qqi_spec, d_attn + n_qi, tm, tn,
                        n_scaled=o_c // tn, scale=(HEAD_DIM ** -0.5) * LOG2E)
    assert o_g % 16 == 0
    gates_spec = pl.BlockSpec((pl.Element(tn), pl.Element(K)),
                              lambda i, j: (pl.multiple_of(o_g + j * tn, 16), 0))
    gates, w_out_b = _matmul_nt(h, wt, gates_spec, n_in - o_g, BF16, tm, tn, "proj_gates", side=w_out)
    kh, vt, kix, wixt = _latent_proj(h, wt, o_c, o_k, kv_g, ln_g, ln_b, w_uk_all, w_uvt_all, B, S)
    ya = _sparse_attention(qqi.reshape(B, S, d_attn + n_qi), wixt, kix, kh, vt, gates, k_top)
    yb = _pool_mixer(gates, w_pool_b, pool_scale, B, S)
    return _out_proj(ya, yb, w_out_b, x2, post_g)


def kernel(x, pre_norm, w_in, kv_norm, w_uk, w_uv, idx_k_norm_g, idx_k_norm_b, w_pool, pool_scale, w_out, post_norm):
    B, S, D = x.shape
    k_top = min(TOPK_MAX, S // 4)
    x2 = x.reshape(B * S, D)
    for l in range(pre_norm.shape[0]):
        x2 = _layer(x2, B, S, pre_norm[l], w_in[l], kv_norm[l], w_uk[l], w_uv[l], idx_k_norm_g[l],
                    idx_k_norm_b[l], w_pool[l], pool_scale[l], w_out[l], post_norm[l], k_top)
    return x2.reshape(B, S, D)
```

```python
import functools
import math

import jax
import jax.numpy as jnp
from jax import lax
from jax.experimental import pallas as pl
from jax.experimental.pallas import tpu as pltpu

F32 = jnp.float32
BF16 = jnp.bfloat16

NORM_EPS = 1e-6
CHUNK = 64
N_HEADS = 16
HEAD_DIM = 128
KV_LATENT = 512
N_IDX_HEADS = 16
IDX_DIM = 64
TOPK_MAX = 256
POOL_WINDOWS = (2, 4, 8, 16)
POOL_GROUP = 512

LANES = 128
Q_TILE = 256
KEY_BLOCK = 256
BF16_SUBLANES = 16
V_ROWS = HEAD_DIM + BF16_SUBLANES
S_AHEAD = 3
R_SLOTS = 4
PV_LAG = 2
P_SLOTS = 4
assert KEY_BLOCK % Q_TILE == 0
HALF_BITS = 16
FINE_BITS = 18
M_FLOOR = -1e30
LOG2E = math.log2(math.e)
VMEM_LIMIT = 56 * 1024 * 1024


def _cparams(sem):
    return pltpu.CompilerParams(dimension_semantics=sem, vmem_limit_bytes=VMEM_LIMIT)


def _silu(x):
    h = 0.5 * x
    return h + h * jnp.tanh(h)


def _rms_rows(x, g):
    ms = jnp.mean(x * x, axis=-1, keepdims=True)
    return (x * lax.rsqrt(ms + NORM_EPS) * g).astype(BF16)


def _prenorm_kernel(x_ref, g_ref, o_ref):
    o_ref[...] = _rms_rows(x_ref[...], g_ref[...])


def _prenorm(x2, g, rows, tm=256):
    D = x2.shape[1]
    return pl.pallas_call(
        _prenorm_kernel,
        grid=(rows // tm,),
        in_specs=[pl.BlockSpec((tm, D), lambda i: (i, 0)),
                  pl.BlockSpec((1, D), lambda i: (0, 0))],
        out_specs=pl.BlockSpec((tm, D), lambda i: (i, 0)),
        out_shape=jax.ShapeDtypeStruct((rows, D), BF16),
        compiler_params=_cparams(("parallel",)),
        name="prenorm",
    )(x2, g.reshape(1, D))


_NT = (((1,), (1,)), ((), ()))


def _matmul_nt_kernel(h_ref, wt_ref, *rest, n_scaled, scale):
    o_ref = rest[-1] if len(rest) == 1 else rest[1]
    z = lax.dot_general(h_ref[...], wt_ref[...].astype(BF16), _NT, preferred_element_type=F32)
    if n_scaled:
        z = z * jnp.where(pl.program_id(1) < n_scaled, jnp.float32(scale), jnp.float32(1.0))
    o_ref[...] = z.astype(o_ref.dtype)
    if len(rest) == 3:
        side_ref, _, side_out_ref = rest
        side_out_ref[...] = side_ref[...].astype(side_out_ref.dtype)


def _matmul_nt(h, wt, w_spec, n_out, out_dtype, tm, tn, name, n_scaled=0, scale=1.0, side=None):
    T, K = h.shape
    tm = min(tm, T)
    n_i, n_j = T // tm, n_out // tn
    in_specs = [pl.BlockSpec((tm, K), lambda i, j: (i, 0)), w_spec]
    out_specs = [pl.BlockSpec((tm, tn), lambda i, j: (i, j))]
    out_shape = [jax.ShapeDtypeStruct((T, n_out), out_dtype)]
    args = [h, wt]
    if side is not None:
        R, C = side.shape
        nb = 1
        while nb * 2 <= min(n_i * n_j, R // BF16_SUBLANES):
            nb *= 2
        assert R % nb == 0
        side_map = lambda i, j: (jnp.minimum(i * n_j + j, nb - 1), 0)
        in_specs.append(pl.BlockSpec((R // nb, C), side_map))
        out_specs.append(pl.BlockSpec((R // nb, C), side_map))
        out_shape.append(jax.ShapeDtypeStruct((R, C), BF16))
        args.append(side)
    res = pl.pallas_call(
        functools.partial(_matmul_nt_kernel, n_scaled=n_scaled, scale=scale),
        grid=(n_i, n_j),
        in_specs=in_specs,
        out_specs=out_specs,
        out_shape=out_shape,
        compiler_params=_cparams(("parallel", "arbitrary")),
        name=name,
    )(*args)
    return res if side is not None else res[0]


NORM_SEG = 256
NORM_PROJ_VMEM_LIMIT = 60 * 1024 * 1024


def _norm_proj_kernel(x_ref, g_ref, h0_ref, wt_ref, o_ref, hall_ref, hbuf_a, hbuf_b, *,
                      n_seg, n_scaled, scale):
    i, j = pl.program_id(0), pl.program_id(1)

    @pl.when((i == 0) & (j == 0))
    def _():
        hbuf_a[...] = h0_ref[...]

    def step(cur_ref, nxt_ref, with_norm):
        z = lax.dot_general(cur_ref[...], wt_ref[...].astype(BF16), _NT, preferred_element_type=F32)
        z = z * jnp.where(j < n_scaled, jnp.float32(scale), jnp.float32(1.0))
        o_ref[...] = z.astype(o_ref.dtype)
        if with_norm:
            hseg = _rms_rows(x_ref[...], g_ref[...])
            hall_ref[...] = hseg
            nxt_ref[pl.ds(pl.multiple_of(j * NORM_SEG, NORM_SEG), NORM_SEG), :] = hseg

    for parity, (cur_ref, nxt_ref) in enumerate(((hbuf_a, hbuf_b), (hbuf_b, hbuf_a))):
        for with_norm in (True, False):
            @pl.when((i % 2 == parity) & ((j < n_seg) == with_norm))
            def _(cur_ref=cur_ref, nxt_ref=nxt_ref, with_norm=with_norm):
                step(cur_ref, nxt_ref, with_norm)


def _norm_proj(x2, g, h0, wt, w_spec, n_out, tm, tn, n_scaled, scale):
    T, D = x2.shape
    n_i, n_j = T // tm, n_out // tn
    n_seg = tm // NORM_SEG
    assert n_seg <= n_j and h0.shape == (tm, D)
    seg_map = lambda i, j: (((i + 1) % n_i) * n_seg + jnp.minimum(j, n_seg - 1), 0)
    return pl.pallas_call(
        functools.partial(_norm_proj_kernel, n_seg=n_seg, n_scaled=n_scaled, scale=scale),
        grid=(n_i, n_j),
        in_specs=[pl.BlockSpec((NORM_SEG, D), seg_map),
                  pl.BlockSpec((1, D), lambda i, j: (0, 0)),
                  pl.BlockSpec((tm, D), lambda i, j: (0, 0), pipeline_mode=pl.Buffered(1)),
                  w_spec],
        out_specs=[pl.BlockSpec((tm, tn), lambda i, j: (i, j)),
                   pl.BlockSpec((NORM_SEG, D), seg_map)],
        out_shape=[jax.ShapeDtypeStruct((T, n_out), BF16),
                   jax.ShapeDtypeStruct((T, D), BF16)],
        scratch_shapes=[pltpu.VMEM((tm, D), BF16), pltpu.VMEM((tm, D), BF16)],
        compiler_params=pltpu.CompilerParams(dimension_semantics=("arbitrary", "arbitrary"),
                                             vmem_limit_bytes=NORM_PROJ_VMEM_LIMIT),
        name="norm_proj_q_qi",
    )(x2, g.reshape(1, D), h0, wt)


def _latent_kernel(h_ref, wc_ref, wkw_ref, kvg_ref, lng_ref, lnb_ref, wuk_ref, wuvt_ref,
                   kh_ref, vt_ref, kix_ref, wixt_ref):
    h = h_ref[...]
    c = lax.dot_general(h, wc_ref[...].astype(BF16), _NT, preferred_element_type=F32)
    cn = c * lax.rsqrt(jnp.mean(c * c, axis=-1, keepdims=True) + NORM_EPS) * kvg_ref[...]
    cb = cn.astype(BF16)
    n_kb = kh_ref.shape[0]
    for pair in range(N_HEADS // 2):
        k2 = jnp.dot(cb, wuk_ref[:, pair * 2 * HEAD_DIM:(pair + 1) * 2 * HEAD_DIM],
                     preferred_element_type=F32)
        for u in range(n_kb):
            for e in range(2):
                kh_ref[u, 2 * pair + e] = k2[u * KEY_BLOCK:(u + 1) * KEY_BLOCK,
                                             e * HEAD_DIM:(e + 1) * HEAD_DIM].astype(kh_ref.dtype)
    for u in range(n_kb):
        ct_u = cn[u * KEY_BLOCK:(u + 1) * KEY_BLOCK, :].T.astype(BF16)
        v_all = jnp.dot(wuvt_ref[...], ct_u, preferred_element_type=F32)
        for hh in range(N_HEADS):
            vt_ref[u, hh, :HEAD_DIM, :] = v_all[hh * HEAD_DIM:(hh + 1) * HEAD_DIM, :].astype(vt_ref.dtype)
            vt_ref[u, hh, HEAD_DIM:, :] = jnp.ones((V_ROWS - HEAD_DIM, KEY_BLOCK), vt_ref.dtype)
    kw = lax.dot_general(h, wkw_ref[...].astype(BF16), _NT, preferred_element_type=F32)
    lane = lax.broadcasted_iota(jnp.int32, kw.shape, 1)
    is_k = lane < IDX_DIM
    mu = jnp.sum(jnp.where(is_k, kw, 0.0), axis=-1, keepdims=True) * (1.0 / IDX_DIM)
    d = jnp.where(is_k, kw - mu, 0.0)
    var = jnp.sum(d * d, axis=-1, keepdims=True) * (1.0 / IDX_DIM)
    kn = d * lax.rsqrt(var + NORM_EPS) * lng_ref[...] + lnb_ref[...]
    kix_ref[...] = jnp.concatenate([kn, pltpu.roll(kn, IDX_DIM, 1)], axis=1).astype(kix_ref.dtype)
    wixt_ref[...] = kw.T[IDX_DIM:IDX_DIM + N_IDX_HEADS, :] * (N_IDX_HEADS ** -0.5)


def _latent_proj(h, wt, o_c, o_k, kv_norm, ln_g, ln_b, w_uk_all, w_uvt_all, B, S, tm=512):
    T, K = h.shape
    nS = S // tm
    pad = jnp.zeros((LANES - IDX_DIM,), F32)
    lng = jnp.concatenate([ln_g, pad]).reshape(1, LANES)
    lnb = jnp.concatenate([ln_b, pad]).reshape(1, LANES)
    assert o_c % KV_LATENT == 0 and o_k % LANES == 0
    return pl.pallas_call(
        _latent_kernel,
        grid=(B, nS),
        in_specs=[pl.BlockSpec((tm, K), lambda b, i: (b * nS + i, 0)),
                  pl.BlockSpec((KV_LATENT, K), lambda b, i: (o_c // KV_LATENT, 0)),
                  pl.BlockSpec((LANES, K), lambda b, i: (o_k // LANES, 0)),
                  pl.BlockSpec((1, KV_LATENT), lambda b, i: (0, 0)),
                  pl.BlockSpec((1, LANES), lambda b, i: (0, 0)),
                  pl.BlockSpec((1, LANES), lambda b, i: (0, 0)),
                  pl.BlockSpec(w_uk_all.shape, lambda b, i: (0, 0)),
                  pl.BlockSpec(w_uvt_all.shape, lambda b, i: (0, 0))],
        out_specs=[pl.BlockSpec((None, tm // KEY_BLOCK, N_HEADS, KEY_BLOCK, HEAD_DIM),
                                lambda b, i: (b, i, 0, 0, 0)),
                   pl.BlockSpec((None, tm // KEY_BLOCK, N_HEADS, V_ROWS, KEY_BLOCK),
                                lambda b, i: (b, i, 0, 0, 0)),
                   pl.BlockSpec((None, tm, 2 * LANES), lambda b, i: (b, i, 0)),
                   pl.BlockSpec((None, N_IDX_HEADS, tm), lambda b, i: (b, 0, i))],
        out_shape=[jax.ShapeDtypeStruct((B, S // KEY_BLOCK, N_HEADS, KEY_BLOCK, HEAD_DIM), BF16),
                   jax.ShapeDtypeStruct((B, S // KEY_BLOCK, N_HEADS, V_ROWS, KEY_BLOCK), BF16),
                   jax.ShapeDtypeStruct((B, S, 2 * LANES), BF16),
                   jax.ShapeDtypeStruct((B, N_IDX_HEADS, S), F32)],
        compiler_params=_cparams(("parallel", "arbitrary")),
        name="latent_proj",
    )(h, wt, wt, kv_norm.reshape(1, KV_LATENT), lng, lnb, w_uk_all, w_uvt_all)


def _sortable_to_f32(key):
    bits = key ^ ((key >> 31) & jnp.int32(0x7FFFFFFF))
    return pltpu.bitcast(bits, F32)


def _fold8(a):
    return jnp.sum(a.reshape(a.shape[0] // 8, 8, a.shape[1]), axis=0)


def _attn_kernel(q_ref, qi_ref, wixt_ref, kix_ref, kh_ref, vt_ref, ga_ref,
                 o_ref, isc_ref, hsc_ref, bias_ref, acc_ref, m_ref, *stage_refs, k_top):
    r_ref, p_ref = stage_refs[:R_SLOTS], stage_refs[R_SLOTS:]
    i = pl.program_id(1)
    t0 = i * Q_TILE
    nkb = (t0 + Q_TILE + KEY_BLOCK - 1) // KEY_BLOCK
    neg_inf = jnp.float32(-jnp.inf)

    t_idx = t0 + lax.broadcasted_iota(jnp.int32, (KEY_BLOCK, Q_TILE), 1)
    q_chunk = t_idx // CHUNK

    def admissible(kb):
        s_idx = kb * KEY_BLOCK + lax.broadcasted_iota(jnp.int32, (KEY_BLOCK, Q_TILE), 0)
        return (s_idx // CHUNK) <= q_chunk

    wix = wixt_ref[...] * (IDX_DIM ** -0.5)

    def idx_body(kb, carry):
        r0 = pl.multiple_of(kb * KEY_BLOCK, KEY_BLOCK)
        kx = kix_ref[pl.ds(r0, KEY_BLOCK), :]
        kx_even, kx_odd = kx[:, :LANES], kx[:, LANES:]
        acc = jnp.zeros((KEY_BLOCK, Q_TILE), F32)
        for pair in range(N_IDX_HEADS // 2):
            rhs = qi_ref[:, pair * LANES:(pair + 1) * LANES]
            for par, kxp in ((0, kx_even), (1, kx_odd)):
                h = 2 * pair + par
                d = lax.dot_general(kxp, rhs, (((1,), (1,)), ((), ())), preferred_element_type=F32)
                acc = acc + wix[h:h + 1, :] * jnp.maximum(d, 0.0)
        isc_ref[pl.ds(r0, KEY_BLOCK), :] = acc
        hsc_ref[pl.ds(r0, KEY_BLOCK), :] = acc.astype(BF16)
        return carry

    lax.fori_loop(0, nkb, idx_body, 0)
    diag_r0 = pl.multiple_of((nkb - 1) * KEY_BLOCK, KEY_BLOCK)
    diag = jnp.where(admissible(nkb - 1), isc_ref[pl.ds(diag_r0, KEY_BLOCK), :], neg_inf)
    isc_ref[pl.ds(diag_r0, KEY_BLOCK), :] = diag
    hsc_ref[pl.ds(diag_r0, KEY_BLOCK), :] = diag.astype(BF16)

    def count(pred):
        def body(kb, cnt):
            r0 = pl.multiple_of(kb * KEY_BLOCK, KEY_BLOCK)
            v = isc_ref[pl.ds(r0, KEY_BLOCK), :]
            return cnt + _fold8(jnp.where(pred(v), 1, 0).astype(jnp.int32))
        c8 = lax.fori_loop(0, nkb, body, jnp.zeros((8, Q_TILE), jnp.int32))
        return jnp.sum(c8, axis=0, keepdims=True)

    def count_half(cand):
        one, zero = jnp.ones((), BF16), jnp.zeros((), BF16)

        def body(kb, cnt):
            r0 = pl.multiple_of(kb * KEY_BLOCK, KEY_BLOCK)
            ind = jnp.where(hsc_ref[pl.ds(r0, KEY_BLOCK), :] >= cand, one, zero)
            parts = [ind[k * BF16_SUBLANES:(k + 1) * BF16_SUBLANES, :]
                     for k in range(KEY_BLOCK // BF16_SUBLANES)]
            while len(parts) > 1:
                parts = [a + b for a, b in zip(parts[0::2], parts[1::2])]
            return cnt + parts[0].astype(F32)

        c16 = lax.fori_loop(0, nkb, body, jnp.zeros((BF16_SUBLANES, Q_TILE), F32))
        return jnp.sum(c16, axis=0, keepdims=True)

    def half_to_bf16(k16):
        bits = k16 ^ ((k16 >> 31) & jnp.int32(0x7FFF))
        return pltpu.bitcast(jnp.left_shift(bits, 16), F32).astype(BF16)

    def coarse_body(step, k16):
        cand = k16 + jnp.left_shift(jnp.int32(1), HALF_BITS - 1 - step)
        return jnp.where(count_half(half_to_bf16(cand)) >= k_top, cand, k16)

    def fine_body(step, carry):
        key, key_cnt = carry
        cand = key + jnp.left_shift(jnp.int32(1), FINE_BITS - 1 - step)
        n_ge = count(lambda v: v >= _sortable_to_f32(cand))
        take = n_ge >= k_top
        return jnp.where(take, cand, key), jnp.where(take, n_ge, key_cnt)

    searching = t0 + Q_TILE > k_top
    k16 = lax.fori_loop(0, jnp.where(searching, HALF_BITS, 0), coarse_body,
                        jnp.full((1, Q_TILE), -2 ** (HALF_BITS - 1), jnp.int32))
    key_lo = jnp.left_shift(k16, 16) - jnp.int32(1 << 16)
    key, key_cnt = lax.fori_loop(0, jnp.where(searching, FINE_BITS, 0), fine_body,
                                 (key_lo, jnp.full((1, Q_TILE), k_top + 1, jnp.int32)))
    n_adm = (t_idx[0:1, :] // CHUNK + 1) * CHUNK
    take_all = n_adm <= k_top
    thr = jnp.where(take_all, neg_inf, _sortable_to_f32(key))
    has_ties = jnp.max(jnp.where(take_all, 0, key_cnt - k_top)) > 0

    @pl.when(jnp.logical_not(has_ties))
    def _():
        def sel_body(kb, carry):
            r0 = pl.multiple_of(kb * KEY_BLOCK, KEY_BLOCK)
            bias_ref[pl.ds(r0, KEY_BLOCK), :] = jnp.where(isc_ref[pl.ds(r0, KEY_BLOCK), :] >= thr,
                                                          0.0, neg_inf)
            return carry

        lax.fori_loop(0, nkb - 1, sel_body, 0)
        sel = (isc_ref[pl.ds(diag_r0, KEY_BLOCK), :] >= thr) & admissible(nkb - 1)
        bias_ref[pl.ds(diag_r0, KEY_BLOCK), :] = jnp.where(sel, 0.0, neg_inf)

    @pl.when(has_ties)
    def _():
        n_gt = count(lambda v: v > thr)
        need = (k_top - n_gt).astype(F32)
        tri = (lax.broadcasted_iota(jnp.int32, (KEY_BLOCK, KEY_BLOCK), 1)
               <= lax.broadcasted_iota(jnp.int32, (KEY_BLOCK, KEY_BLOCK), 0)).astype(BF16)

        def sel_body(kb, tie_carry):
            r0 = pl.multiple_of(kb * KEY_BLOCK, KEY_BLOCK)
            v = isc_ref[pl.ds(r0, KEY_BLOCK), :]
            tie = (v == thr).astype(BF16)
            rank = jnp.dot(tri, tie, preferred_element_type=F32) + tie_carry
            sel = ((v > thr) | ((v == thr) & (rank <= need))) & admissible(kb)
            bias_ref[pl.ds(r0, KEY_BLOCK), :] = jnp.where(sel, 0.0, neg_inf)
            return rank[KEY_BLOCK - 1:KEY_BLOCK, :]

        lax.fori_loop(0, nkb, sel_body, jnp.zeros((1, Q_TILE), F32))

    acc_ref[...] = jnp.zeros_like(acc_ref)
    m_ref[...] = jnp.full_like(m_ref, M_FLOOR)

    def scores_stage(kb, h):
        r0 = pl.multiple_of(kb * KEY_BLOCK, KEY_BLOCK)
        s = lax.dot_general(kh_ref[kb, h], q_ref[:, h * HEAD_DIM:(h + 1) * HEAD_DIM], _NT,
                            preferred_element_type=F32) + bias_ref[pl.ds(r0, KEY_BLOCK), :]
        r_ref[h % R_SLOTS][...] = s
        m_old = m_ref[h]
        m_blk = jnp.max(jnp.max(s.reshape(KEY_BLOCK // 8, 8, Q_TILE), axis=0), axis=0, keepdims=True)
        m_new = jnp.maximum(m_old, m_blk)
        m_ref[h] = m_new
        return m_new, jnp.exp2(m_old - m_new)

    first = ()
    for h in range(S_AHEAD):
        first += scores_stage(0, h)

    def attn_body(kb, carry):
        kb_next = jnp.minimum(kb + 1, nkb - 1)

        def pv_update(h, alpha):
            acc_ref[h] = alpha * acc_ref[h] + jnp.dot(vt_ref[kb, h], p_ref[h % P_SLOTS][...],
                                                      preferred_element_type=F32)

        stats = [carry[2 * a:2 * a + 2] for a in range(S_AHEAD)]
        pending = []
        for h in range(N_HEADS):
            kb2, h2 = (kb, h + S_AHEAD) if h + S_AHEAD < N_HEADS else (kb_next, h + S_AHEAD - N_HEADS)
            stats.append(scores_stage(kb2, h2))
            if len(pending) == PV_LAG:
                pv_update(*pending.pop(0))
            m_new, alpha = stats.pop(0)
            p_ref[h % P_SLOTS][...] = jnp.exp2(r_ref[h % R_SLOTS][...] - m_new).astype(BF16)
            pending.append((h, alpha))
        for item in pending:
            pv_update(*item)
        out = ()
        for st in stats:
            out += st
        return out

    lax.fori_loop(0, nkb, attn_body, first)

    for h in range(N_HEADS):
        a = acc_ref[h]
        out_t = a[:HEAD_DIM, :] * (1.0 / a[HEAD_DIM:HEAD_DIM + 1, :])
        gate = ga_ref[:, h * HEAD_DIM:(h + 1) * HEAD_DIM].astype(F32)
        o_ref[:, h * HEAD_DIM:(h + 1) * HEAD_DIM] = (
            out_t.T * _silu(gate)).astype(o_ref.dtype)


def _sparse_attention(qqi, wixt, kix, kh, vt, gates, k_top):
    B, S, _ = qqi.shape
    DA = N_HEADS * HEAD_DIM
    DI = N_IDX_HEADS * IDX_DIM
    nq = S // Q_TILE
    return pl.pallas_call(
        functools.partial(_attn_kernel, k_top=k_top),
        grid=(B, nq),
        in_specs=[pl.BlockSpec((None, Q_TILE, DA), lambda b, i: (b, i, 0)),
                  pl.BlockSpec((None, Q_TILE, DI), lambda b, i: (b, i, DA // DI)),
                  pl.BlockSpec((None, N_IDX_HEADS, Q_TILE), lambda b, i: (b, 0, i)),
                  pl.BlockSpec((None, S, 2 * LANES), lambda b, i: (b, 0, 0)),
                  pl.BlockSpec((None,) + kh.shape[1:], lambda b, i: (b, 0, 0, 0, 0)),
                  pl.BlockSpec((None,) + vt.shape[1:], lambda b, i: (b, 0, 0, 0, 0)),
                  pl.BlockSpec((Q_TILE, DA), lambda b, i: (b * nq + i, 0))],
        out_specs=pl.BlockSpec((Q_TILE, DA), lambda b, i: (b * nq + i, 0)),
        out_shape=jax.ShapeDtypeStruct((B * S, DA), BF16),
        scratch_shapes=[pltpu.VMEM((S, Q_TILE), F32),
                        pltpu.VMEM((S, Q_TILE), BF16),
                        pltpu.VMEM((S, Q_TILE), F32),
                        pltpu.VMEM((N_HEADS, V_ROWS, Q_TILE), F32),
                        pltpu.VMEM((N_HEADS, 1, Q_TILE), F32)]
                       + [pltpu.VMEM((KEY_BLOCK, Q_TILE), F32)] * R_SLOTS
                       + [pltpu.VMEM((KEY_BLOCK, Q_TILE), BF16)] * P_SLOTS,
        compiler_params=_cparams(("parallel", "arbitrary")),
        name="sparse_attention",
    )(qqi, qqi, wixt, kix, kh, vt, gates)


POOL_HALO = 16


def _pool_bands(ts, dtype):
    r = lax.broadcasted_iota(jnp.int32, (ts, ts), 0)
    c = lax.broadcasted_iota(jnp.int32, (ts, ts), 1)
    hr = lax.broadcasted_iota(jnp.int32, (POOL_HALO, POOL_HALO), 0)
    hc = lax.broadcasted_iota(jnp.int32, (POOL_HALO, POOL_HALO), 1)
    bands = [jnp.where((c <= r) & (c > r - win), 1.0 / win, 0.0) - jnp.where(c == r, 1.0, 0.0)
             for win in POOL_WINDOWS]
    hbands = [jnp.where(hc >= hr + (POOL_HALO + 1 - win), 1.0 / win, 0.0) for win in POOL_WINDOWS]
    return jnp.stack(bands).astype(dtype), jnp.stack(hbands).astype(dtype)


def _pool_kernel(pin_ref, halo_ref, gate_ref, wp_ref, ps_ref, band_ref, hband_ref, o_ref, mix_ref):
    i = pl.program_id(1)
    ts = pin_ref.shape[0]

    @pl.when(i == 0)
    def _():
        t = lax.broadcasted_iota(jnp.int32, (ts, POOL_GROUP), 0)
        for g, win in enumerate(POOL_WINDOWS):
            cols = slice(g * POOL_GROUP, (g + 1) * POOL_GROUP)
            cur = pin_ref[:, cols].astype(F32)
            ssum = jnp.concatenate([jnp.zeros((POOL_HALO, POOL_GROUP), F32), cur], axis=0)
            span = 1
            while span < win:
                ssum = ssum + pltpu.roll(ssum, span, 0)
                span *= 2
            count = jnp.minimum(t + 1, win).astype(F32)
            mix_ref[:, cols] = (ssum[POOL_HALO:, :] / count - cur).astype(mix_ref.dtype)

    @pl.when(i > 0)
    def _():
        for g in range(len(POOL_WINDOWS)):
            cols = slice(g * POOL_GROUP, (g + 1) * POOL_GROUP)
            m = jnp.dot(band_ref[g], pin_ref[:, cols], preferred_element_type=F32)
            top = m[:POOL_HALO, :] + jnp.dot(hband_ref[g], halo_ref[:, cols], preferred_element_type=F32)
            mix_ref[:POOL_HALO, cols] = top.astype(mix_ref.dtype)
            mix_ref[POOL_HALO:, cols] = m[POOL_HALO:, :].astype(mix_ref.dtype)

    for g in range(len(POOL_WINDOWS)):
        cols = slice(g * POOL_GROUP, (g + 1) * POOL_GROUP)
        y = jnp.dot(mix_ref[:, cols], wp_ref[g], preferred_element_type=F32)
        gate = gate_ref[:, cols].astype(F32)
        o_ref[:, cols] = (y * ps_ref[:, cols] * _silu(gate)).astype(o_ref.dtype)


def _pool_mixer(gates, w_pool, pool_scale, B, S, ts=256):
    T = gates.shape[0]
    D = gates.shape[1] // 3
    nS = S // ts
    hb = ts // POOL_HALO
    bands, hbands = _pool_bands(ts, gates.dtype)
    return pl.pallas_call(
        _pool_kernel,
        grid=(B, nS),
        in_specs=[pl.BlockSpec((ts, D), lambda b, i: (b * nS + i, 1)),
                  pl.BlockSpec((POOL_HALO, D), lambda b, i: (jnp.maximum((b * nS + i) * hb - 1, 0), 1)),
                  pl.BlockSpec((ts, D), lambda b, i: (b * nS + i, 2)),
                  pl.BlockSpec(w_pool.shape, lambda b, i: (0, 0, 0)),
                  pl.BlockSpec((1, D), lambda b, i: (0, 0)),
                  pl.BlockSpec(bands.shape, lambda b, i: (0, 0, 0)),
                  pl.BlockSpec(hbands.shape, lambda b, i: (0, 0, 0))],
        out_specs=pl.BlockSpec((ts, D), lambda b, i: (b * nS + i, 0)),
        out_shape=jax.ShapeDtypeStruct((T, D), BF16),
        scratch_shapes=[pltpu.VMEM((ts, D), BF16)],
        compiler_params=_cparams(("parallel", "arbitrary")),
        name="pool_mixer",
    )(gates, gates, gates, w_pool, pool_scale.reshape(1, D), bands, hbands)


OUT_COL_TILE = 512
OUT_PROJ_VMEM_LIMIT = 60 * 1024 * 1024


def _outproj_kernel(ya_ref, yb_ref, w_ref, x_ref, g_ref, o_ref):
    da = ya_ref.shape[1]
    tm, d = o_ref.shape
    n_col = d // OUT_COL_TILE
    ss = jnp.zeros((tm, LANES), F32)
    for jj in range(n_col):
        cols = slice(jj * OUT_COL_TILE, (jj + 1) * OUT_COL_TILE)
        y = (jnp.dot(ya_ref[...], w_ref[:da, cols], preferred_element_type=F32)
             + jnp.dot(yb_ref[...], w_ref[da:, cols], preferred_element_type=F32))
        o_ref[:, cols] = y
        for k in range(OUT_COL_TILE // LANES):
            yk = y[:, k * LANES:(k + 1) * LANES]
            ss = ss + yk * yk
    rs = lax.rsqrt(jnp.sum(ss, axis=-1, keepdims=True) * (1.0 / d) + NORM_EPS)
    for jj in range(n_col):
        cols = slice(jj * OUT_COL_TILE, (jj + 1) * OUT_COL_TILE)
        o_ref[:, cols] = x_ref[:, cols] + o_ref[:, cols] * rs * g_ref[:, cols]


def _out_proj(ya, yb, w_out_b, x2, post_g, tm=256):
    T, D = x2.shape
    DA = ya.shape[1]
    DB = yb.shape[1]
    return pl.pallas_call(
        _outproj_kernel,
        grid=(T // tm,),
        in_specs=[pl.BlockSpec((tm, DA), lambda i: (i, 0)),
                  pl.BlockSpec((tm, DB), lambda i: (i, 0)),
                  pl.BlockSpec((DA + DB, D), lambda i: (0, 0), pipeline_mode=pl.Buffered(1)),
                  pl.BlockSpec((tm, D), lambda i: (i, 0)),
                  pl.BlockSpec((1, D), lambda i: (0, 0))],
        out_specs=pl.BlockSpec((tm, D), lambda i: (i, 0)),
        out_shape=jax.ShapeDtypeStruct((T, D), F32),
        compiler_params=pltpu.CompilerParams(dimension_semantics=("parallel",),
                                             vmem_limit_bytes=OUT_PROJ_VMEM_LIMIT),
        name="out_proj",
    )(ya, yb, w_out_b, x2, post_g.reshape(1, D))


PROJ_TN = 512
PROJ_TM = 1024


def _layer(x2, B, S, pre_g, w_in, kv_g, w_uk, w_uv, ln_g, ln_b, w_pool, pool_scale, w_out, post_g, k_top):
    d_attn = N_HEADS * HEAD_DIM
    n_qi = N_IDX_HEADS * IDX_DIM
    o_c = d_attn
    o_qi = o_c + KV_LATENT
    o_k = o_qi + n_qi
    o_w = o_k + IDX_DIM
    o_g = o_w + N_IDX_HEADS
    K, n_in = w_in.shape
    wt = w_in.T
    w_uk_all = jnp.transpose(w_uk, (1, 0, 2)).reshape(KV_LATENT, d_attn).astype(BF16)
    w_uvt_all = jnp.swapaxes(w_uv, 1, 2).reshape(d_attn, KV_LATENT).astype(BF16)
    w_pool_b = w_pool.astype(BF16)

    tn = PROJ_TN
    tm = min(PROJ_TM, x2.shape[0])
    assert o_c % tn == 0 and o_qi % tn == 0 and n_qi % tn == 0
    qqi_spec = pl.BlockSpec((tn, K), lambda i, j: (jnp.where(j < o_c // tn, j, j + (o_qi - o_c) // tn), 0))
    h0 = _prenorm(x2, pre_g, tm)
    qqi, h = _norm_proj(x2, pre_g, h0, wt, qqi_spec, d_attn + n_qi, tm, tn,
                        n_scaled=o_c // tn, scale=(HEAD_DIM ** -0.5) * LOG2E)
    assert o_g % 16 == 0
    gates_spec = pl.BlockSpec((pl.Element(tn), pl.Element(K)),
                              lambda i, j: (pl.multiple_of(o_g + j * tn, 16), 0))
    gates, w_out_b = _matmul_nt(h, wt, gates_spec, n_in - o_g, BF16, tm, tn, "proj_gates", side=w_out)
    kh, vt, kix, wixt = _latent_proj(h, wt, o_c, o_k, kv_g, ln_g, ln_b, w_uk_all, w_uvt_all, B, S)
    ya = _sparse_attention(qqi.reshape(B, S, d_attn + n_qi), wixt, kix, kh, vt, gates, k_top)
    yb = _pool_mixer(gates, w_pool_b, pool_scale, B, S)
    return _out_proj(ya, yb, w_out_b, x2, post_g)


def kernel(x, pre_norm, w_in, kv_norm, w_uk, w_uv, idx_k_norm_g, idx_k_norm_b, w_pool, pool_scale, w_out, post_norm):
    B, S, D = x.shape
    k_top = min(TOPK_MAX, S // 4)
    x2 = x.reshape(B * S, D)
    for l in range(pre_norm.shape[0]):
        x2 = _layer(x2, B, S, pre_norm[l], w_in[l], kv_norm[l], w_uk[l], w_uv[l], idx_k_norm_g[l],
                    idx_k_norm_b[l], w_pool[l], pool_scale[l], w_out[l], post_norm[l], k_top)
    return x2.reshape(B, S, D)
```

```python
import functools
import math

import jax
import jax.numpy as jnp
from jax import lax
from jax.experimental import pallas as pl
from jax.experimental.pallas import tpu as pltpu

F32 = jnp.float32
BF16 = jnp.bfloat16

NORM_EPS = 1e-6
CHUNK = 64
N_HEADS = 16
HEAD_DIM = 128
KV_LATENT = 512
N_IDX_HEADS = 16
IDX_DIM = 64
TOPK_MAX = 256
POOL_WINDOWS = (2, 4, 8, 16)
POOL_GROUP = 512

LANES = 128
Q_TILE = 256
KEY_BLOCK = 256
BF16_SUBLANES = 16
V_ROWS = HEAD_DIM + BF16_SUBLANES
S_AHEAD = 3
R_SLOTS = 4
PV_LAG = 2
P_SLOTS = 4
assert KEY_BLOCK % Q_TILE == 0
HALF_BITS = 16
FINE_BITS = 18
M_FLOOR = -1e30
LOG2E = math.log2(math.e)
VMEM_LIMIT = 56 * 1024 * 1024


def _cparams(sem):
    return pltpu.CompilerParams(dimension_semantics=sem, vmem_limit_bytes=VMEM_LIMIT)


def _silu(x):
    h = 0.5 * x
    return h + h * jnp.tanh(h)


def _rms_rows(x, g):
    ms = jnp.mean(x * x, axis=-1, keepdims=True)
    return (x * lax.rsqrt(ms + NORM_EPS) * g).astype(BF16)


def _prenorm_kernel(x_ref, g_ref, o_ref):
    o_ref[...] = _rms_rows(x_ref[...], g_ref[...])


def _prenorm(x2, g, rows, tm=256):
    D = x2.shape[1]
    return pl.pallas_call(
        _prenorm_kernel,
        grid=(rows // tm,),
        in_specs=[pl.BlockSpec((tm, D), lambda i: (i, 0)),
                  pl.BlockSpec((1, D), lambda i: (0, 0))],
        out_specs=pl.BlockSpec((tm, D), lambda i: (i, 0)),
        out_shape=jax.ShapeDtypeStruct((rows, D), BF16),
        compiler_params=_cparams(("parallel",)),
        name="prenorm",
    )(x2, g.reshape(1, D))


_NT = (((1,), (1,)), ((), ()))


def _matmul_nt_kernel(h_ref, wt_ref, *rest, n_scaled, scale):
    o_ref = rest[-1] if len(rest) == 1 else rest[1]
    w = wt_ref[...]
    if w.dtype != BF16:
        w = w.astype(BF16)
    z = lax.dot_general(h_ref[...], w, _NT, preferred_element_type=F32)
    if n_scaled:
        z = z * jnp.where(pl.program_id(1) < n_scaled, jnp.float32(scale), jnp.float32(1.0))
    o_ref[...] = z.astype(o_ref.dtype)
    if len(rest) == 3:
        side_ref, _, side_out_ref = rest
        side_out_ref[...] = side_ref[...].astype(side_out_ref.dtype)


def _matmul_nt(h, wt, w_spec, n_out, out_dtype, tm, tn, name, n_scaled=0, scale=1.0, side=None):
    T, K = h.shape
    tm = min(tm, T)
    n_i, n_j = T // tm, n_out // tn
    in_specs = [pl.BlockSpec((tm, K), lambda i, j: (i, 0)), w_spec]
    out_specs = [pl.BlockSpec((tm, tn), lambda i, j: (i, j))]
    out_shape = [jax.ShapeDtypeStruct((T, n_out), out_dtype)]
    args = [h, wt]
    if side is not None:
        R, C = side.shape
        nb = 1
        while nb * 2 <= min(n_i * n_j, R // BF16_SUBLANES):
            nb *= 2
        assert R % nb == 0
        side_map = lambda i, j: (jnp.minimum(i * n_j + j, nb - 1), 0)
        in_specs.append(pl.BlockSpec((R // nb, C), side_map))
        out_specs.append(pl.BlockSpec((R // nb, C), side_map))
        out_shape.append(jax.ShapeDtypeStruct((R, C), BF16))
        args.append(side)
    res = pl.pallas_call(
        functools.partial(_matmul_nt_kernel, n_scaled=n_scaled, scale=scale),
        grid=(n_i, n_j),
        in_specs=in_specs,
        out_specs=out_specs,
        out_shape=out_shape,
        compiler_params=_cparams(("parallel", "arbitrary")),
        name=name,
    )(*args)
    return res if side is not None else res[0]


NORM_SEG = 256
NORM_PROJ_VMEM_LIMIT = 60 * 1024 * 1024


def _norm_proj_kernel(x_ref, g_ref, h0_ref, wt_ref, o_ref, hall_ref, hbuf_a, hbuf_b, *,
                      n_seg, n_scaled, scale):
    i, j = pl.program_id(0), pl.program_id(1)

    @pl.when((i == 0) & (j == 0))
    def _():
        hbuf_a[...] = h0_ref[...]

    def step(cur_ref, nxt_ref, with_norm):
        z = lax.dot_general(cur_ref[...], wt_ref[...].astype(BF16), _NT, preferred_element_type=F32)
        z = z * jnp.where(j < n_scaled, jnp.float32(scale), jnp.float32(1.0))
        o_ref[...] = z.astype(o_ref.dtype)
        if with_norm:
            hseg = _rms_rows(x_ref[...], g_ref[...])
            hall_ref[...] = hseg
            nxt_ref[pl.ds(pl.multiple_of(j * NORM_SEG, NORM_SEG), NORM_SEG), :] = hseg

    for parity, (cur_ref, nxt_ref) in enumerate(((hbuf_a, hbuf_b), (hbuf_b, hbuf_a))):
        for with_norm in (True, False):
            @pl.when((i % 2 == parity) & ((j < n_seg) == with_norm))
            def _(cur_ref=cur_ref, nxt_ref=nxt_ref, with_norm=with_norm):
                step(cur_ref, nxt_ref, with_norm)


def _norm_proj(x2, g, h0, wt, w_spec, n_out, tm, tn, n_scaled, scale):
    T, D = x2.shape
    n_i, n_j = T // tm, n_out // tn
    n_seg = tm // NORM_SEG
    assert n_seg <= n_j and h0.shape == (tm, D)
    seg_map = lambda i, j: (((i + 1) % n_i) * n_seg + jnp.minimum(j, n_seg - 1), 0)
    return pl.pallas_call(
        functools.partial(_norm_proj_kernel, n_seg=n_seg, n_scaled=n_scaled, scale=scale),
        grid=(n_i, n_j),
        in_specs=[pl.BlockSpec((NORM_SEG, D), seg_map),
                  pl.BlockSpec((1, D), lambda i, j: (0, 0)),
                  pl.BlockSpec((tm, D), lambda i, j: (0, 0), pipeline_mode=pl.Buffered(1)),
                  w_spec],
        out_specs=[pl.BlockSpec((tm, tn), lambda i, j: (i, j)),
                   pl.BlockSpec((NORM_SEG, D), seg_map)],
        out_shape=[jax.ShapeDtypeStruct((T, n_out), BF16),
                   jax.ShapeDtypeStruct((T, D), BF16)],
        scratch_shapes=[pltpu.VMEM((tm, D), BF16), pltpu.VMEM((tm, D), BF16)],
        compiler_params=pltpu.CompilerParams(dimension_semantics=("arbitrary", "arbitrary"),
                                             vmem_limit_bytes=NORM_PROJ_VMEM_LIMIT),
        name="norm_proj_q_qi",
    )(x2, g.reshape(1, D), h0, wt)


def _latent_kernel(h_ref, wc_ref, wkw_ref, kvg_ref, lng_ref, lnb_ref, wuk_ref, wuvt_ref, wside_ref,
                   kh_ref, vt_ref, kix_ref, wixt_ref, wside_out_ref):
    wside_out_ref[...] = wside_ref[...].astype(wside_out_ref.dtype)
    h = h_ref[...]
    c = lax.dot_general(h, wc_ref[...].astype(BF16), _NT, preferred_element_type=F32)
    cn = c * lax.rsqrt(jnp.mean(c * c, axis=-1, keepdims=True) + NORM_EPS) * kvg_ref[...]
    cb = cn.astype(BF16)
    n_kb = kh_ref.shape[0]
    for pair in range(N_HEADS // 2):
        k2 = jnp.dot(cb, wuk_ref[:, pair * 2 * HEAD_DIM:(pair + 1) * 2 * HEAD_DIM],
                     preferred_element_type=F32)
        for u in range(n_kb):
            for e in range(2):
                kh_ref[u, 2 * pair + e] = k2[u * KEY_BLOCK:(u + 1) * KEY_BLOCK,
                                             e * HEAD_DIM:(e + 1) * HEAD_DIM].astype(kh_ref.dtype)
    for u in range(n_kb):
        ct_u = cn[u * KEY_BLOCK:(u + 1) * KEY_BLOCK, :].T.astype(BF16)
        v_all = jnp.dot(wuvt_ref[...], ct_u, preferred_element_type=F32)
        for hh in range(N_HEADS):
            vt_ref[u, hh, :HEAD_DIM, :] = v_all[hh * HEAD_DIM:(hh + 1) * HEAD_DIM, :].astype(vt_ref.dtype)
            vt_ref[u, hh, HEAD_DIM:, :] = jnp.ones((V_ROWS - HEAD_DIM, KEY_BLOCK), vt_ref.dtype)
    kw = lax.dot_general(h, wkw_ref[...].astype(BF16), _NT, preferred_element_type=F32)
    lane = lax.broadcasted_iota(jnp.int32, kw.shape, 1)
    is_k = lane < IDX_DIM
    mu = jnp.sum(jnp.where(is_k, kw, 0.0), axis=-1, keepdims=True) * (1.0 / IDX_DIM)
    d = jnp.where(is_k, kw - mu, 0.0)
    var = jnp.sum(d * d, axis=-1, keepdims=True) * (1.0 / IDX_DIM)
    kn = d * lax.rsqrt(var + NORM_EPS) * lng_ref[...] + lnb_ref[...]
    kix_ref[...] = jnp.concatenate([kn, pltpu.roll(kn, IDX_DIM, 1)], axis=1).astype(kix_ref.dtype)
    wixt_ref[...] = kw.T[IDX_DIM:IDX_DIM + N_IDX_HEADS, :] * (N_IDX_HEADS ** -0.5)


def _latent_proj(h, wt, o_c, o_k, o_side, kv_norm, ln_g, ln_b, w_uk_all, w_uvt_all, B, S, tm=512):
    T, K = h.shape
    nS = S // tm
    pad = jnp.zeros((LANES - IDX_DIM,), F32)
    lng = jnp.concatenate([ln_g, pad]).reshape(1, LANES)
    lnb = jnp.concatenate([ln_b, pad]).reshape(1, LANES)
    assert o_c % KV_LATENT == 0 and o_k % LANES == 0
    n_side = wt.shape[0] - o_side
    side_rows = n_side // (B * nS)
    assert side_rows * B * nS == n_side and side_rows % BF16_SUBLANES == 0 and o_side % BF16_SUBLANES == 0
    return pl.pallas_call(
        _latent_kernel,
        grid=(B, nS),
        in_specs=[pl.BlockSpec((tm, K), lambda b, i: (b * nS + i, 0)),
                  pl.BlockSpec((KV_LATENT, K), lambda b, i: (o_c // KV_LATENT, 0)),
                  pl.BlockSpec((LANES, K), lambda b, i: (o_k // LANES, 0)),
                  pl.BlockSpec((1, KV_LATENT), lambda b, i: (0, 0)),
                  pl.BlockSpec((1, LANES), lambda b, i: (0, 0)),
                  pl.BlockSpec((1, LANES), lambda b, i: (0, 0)),
                  pl.BlockSpec(w_uk_all.shape, lambda b, i: (0, 0)),
                  pl.BlockSpec(w_uvt_all.shape, lambda b, i: (0, 0)),
                  pl.BlockSpec((pl.Element(side_rows), pl.Element(K)),
                               lambda b, i: (pl.multiple_of(o_side + (b * nS + i) * side_rows, 16), 0))],
        out_specs=[pl.BlockSpec((None, tm // KEY_BLOCK, N_HEADS, KEY_BLOCK, HEAD_DIM),
                                lambda b, i: (b, i, 0, 0, 0)),
                   pl.BlockSpec((None, tm // KEY_BLOCK, N_HEADS, V_ROWS, KEY_BLOCK),
                                lambda b, i: (b, i, 0, 0, 0)),
                   pl.BlockSpec((None, tm, 2 * LANES), lambda b, i: (b, i, 0)),
                   pl.BlockSpec((None, N_IDX_HEADS, tm), lambda b, i: (b, 0, i)),
                   pl.BlockSpec((side_rows, K), lambda b, i: (b * nS + i, 0))],
        out_shape=[jax.ShapeDtypeStruct((B, S // KEY_BLOCK, N_HEADS, KEY_BLOCK, HEAD_DIM), BF16),
                   jax.ShapeDtypeStruct((B, S // KEY_BLOCK, N_HEADS, V_ROWS, KEY_BLOCK), BF16),
                   jax.ShapeDtypeStruct((B, S, 2 * LANES), BF16),
                   jax.ShapeDtypeStruct((B, N_IDX_HEADS, S), F32),
                   jax.ShapeDtypeStruct((n_side, K), BF16)],
        compiler_params=_cparams(("parallel", "arbitrary")),
        name="latent_proj",
    )(h, wt, wt, kv_norm.reshape(1, KV_LATENT), lng, lnb, w_uk_all, w_uvt_all, wt)


def _sortable_to_f32(key):
    bits = key ^ ((key >> 31) & jnp.int32(0x7FFFFFFF))
    return pltpu.bitcast(bits, F32)


def _fold8(a):
    return jnp.sum(a.reshape(a.shape[0] // 8, 8, a.shape[1]), axis=0)


def _attn_kernel(q_ref, qi_ref, wixt_ref, kix_ref, kh_ref, vt_ref, ga_ref,
                 o_ref, isc_ref, hsc_ref, bias_ref, acc_ref, m_ref, *stage_refs, k_top):
    r_ref, p_ref = stage_refs[:R_SLOTS], stage_refs[R_SLOTS:]
    i = pl.program_id(1)
    t0 = i * Q_TILE
    nkb = (t0 + Q_TILE + KEY_BLOCK - 1) // KEY_BLOCK
    neg_inf = jnp.float32(-jnp.inf)

    t_idx = t0 + lax.broadcasted_iota(jnp.int32, (KEY_BLOCK, Q_TILE), 1)
    q_chunk = t_idx // CHUNK

    def admissible(kb):
        s_idx = kb * KEY_BLOCK + lax.broadcasted_iota(jnp.int32, (KEY_BLOCK, Q_TILE), 0)
        return (s_idx // CHUNK) <= q_chunk

    wix = wixt_ref[...] * (IDX_DIM ** -0.5)

    def idx_body(kb, carry):
        r0 = pl.multiple_of(kb * KEY_BLOCK, KEY_BLOCK)
        kx = kix_ref[pl.ds(r0, KEY_BLOCK), :]
        kx_even, kx_odd = kx[:, :LANES], kx[:, LANES:]
        acc = jnp.zeros((KEY_BLOCK, Q_TILE), F32)
        for pair in range(N_IDX_HEADS // 2):
            rhs = qi_ref[:, pair * LANES:(pair + 1) * LANES]
            for par, kxp in ((0, kx_even), (1, kx_odd)):
                h = 2 * pair + par
                d = lax.dot_general(kxp, rhs, (((1,), (1,)), ((), ())), preferred_element_type=F32)
                acc = acc + wix[h:h + 1, :] * jnp.maximum(d, 0.0)
        isc_ref[pl.ds(r0, KEY_BLOCK), :] = acc
        hsc_ref[pl.ds(r0, KEY_BLOCK), :] = acc.astype(BF16)
        return carry

    lax.fori_loop(0, nkb, idx_body, 0)
    diag_r0 = pl.multiple_of((nkb - 1) * KEY_BLOCK, KEY_BLOCK)
    diag = jnp.where(admissible(nkb - 1), isc_ref[pl.ds(diag_r0, KEY_BLOCK), :], neg_inf)
    isc_ref[pl.ds(diag_r0, KEY_BLOCK), :] = diag
    hsc_ref[pl.ds(diag_r0, KEY_BLOCK), :] = diag.astype(BF16)

    def count(pred):
        def body(kb, cnt):
            r0 = pl.multiple_of(kb * KEY_BLOCK, KEY_BLOCK)
            v = isc_ref[pl.ds(r0, KEY_BLOCK), :]
            return cnt + _fold8(jnp.where(pred(v), 1, 0).astype(jnp.int32))
        c8 = lax.fori_loop(0, nkb, body, jnp.zeros((8, Q_TILE), jnp.int32))
        return jnp.sum(c8, axis=0, keepdims=True)

    def count_half(cand):
        one, zero = jnp.ones((), BF16), jnp.zeros((), BF16)

        def body(kb, cnt):
            r0 = pl.multiple_of(kb * KEY_BLOCK, KEY_BLOCK)
            ind = jnp.where(hsc_ref[pl.ds(r0, KEY_BLOCK), :] >= cand, one, zero)
            parts = [ind[k * BF16_SUBLANES:(k + 1) * BF16_SUBLANES, :]
                     for k in range(KEY_BLOCK // BF16_SUBLANES)]
            while len(parts) > 1:
                parts = [a + b for a, b in zip(parts[0::2], parts[1::2])]
            return cnt + parts[0].astype(F32)

        c16 = lax.fori_loop(0, nkb, body, jnp.zeros((BF16_SUBLANES, Q_TILE), F32))
        return jnp.sum(c16, axis=0, keepdims=True)

    def half_to_bf16(k16):
        bits = k16 ^ ((k16 >> 31) & jnp.int32(0x7FFF))
        return pltpu.bitcast(jnp.left_shift(bits, 16), F32).astype(BF16)

    def coarse_body(step, k16):
        cand = k16 + jnp.left_shift(jnp.int32(1), HALF_BITS - 1 - step)
        return jnp.where(count_half(half_to_bf16(cand)) >= k_top, cand, k16)

    def fine_body(step, carry):
        key, key_cnt = carry
        cand = key + jnp.left_shift(jnp.int32(1), FINE_BITS - 1 - step)
        n_ge = count(lambda v: v >= _sortable_to_f32(cand))
        take = n_ge >= k_top
        return jnp.where(take, cand, key), jnp.where(take, n_ge, key_cnt)

    searching = t0 + Q_TILE > k_top
    k16 = lax.fori_loop(0, jnp.where(searching, HALF_BITS, 0), coarse_body,
                        jnp.full((1, Q_TILE), -2 ** (HALF_BITS - 1), jnp.int32))
    key_lo = jnp.left_shift(k16, 16) - jnp.int32(1 << 16)
    key, key_cnt = lax.fori_loop(0, jnp.where(searching, FINE_BITS, 0), fine_body,
                                 (key_lo, jnp.full((1, Q_TILE), k_top + 1, jnp.int32)))
    n_adm = (t_idx[0:1, :] // CHUNK + 1) * CHUNK
    take_all = n_adm <= k_top
    thr = jnp.where(take_all, neg_inf, _sortable_to_f32(key))
    has_ties = jnp.max(jnp.where(take_all, 0, key_cnt - k_top)) > 0

    @pl.when(jnp.logical_not(has_ties))
    def _():
        def sel_body(kb, carry):
            r0 = pl.multiple_of(kb * KEY_BLOCK, KEY_BLOCK)
            bias_ref[pl.ds(r0, KEY_BLOCK), :] = jnp.where(isc_ref[pl.ds(r0, KEY_BLOCK), :] >= thr,
                                                          0.0, neg_inf)
            return carry

        lax.fori_loop(0, nkb - 1, sel_body, 0)
        sel = (isc_ref[pl.ds(diag_r0, KEY_BLOCK), :] >= thr) & admissible(nkb - 1)
        bias_ref[pl.ds(diag_r0, KEY_BLOCK), :] = jnp.where(sel, 0.0, neg_inf)

    @pl.when(has_ties)
    def _():
        n_gt = count(lambda v: v > thr)
        need = (k_top - n_gt).astype(F32)
        tri = (lax.broadcasted_iota(jnp.int32, (KEY_BLOCK, KEY_BLOCK), 1)
               <= lax.broadcasted_iota(jnp.int32, (KEY_BLOCK, KEY_BLOCK), 0)).astype(BF16)

        def sel_body(kb, tie_carry):
            r0 = pl.multiple_of(kb * KEY_BLOCK, KEY_BLOCK)
            v = isc_ref[pl.ds(r0, KEY_BLOCK), :]
            tie = (v == thr).astype(BF16)
            rank = jnp.dot(tri, tie, preferred_element_type=F32) + tie_carry
            sel = ((v > thr) | ((v == thr) & (rank <= need))) & admissible(kb)
            bias_ref[pl.ds(r0, KEY_BLOCK), :] = jnp.where(sel, 0.0, neg_inf)
            return rank[KEY_BLOCK - 1:KEY_BLOCK, :]

        lax.fori_loop(0, nkb, sel_body, jnp.zeros((1, Q_TILE), F32))

    acc_ref[...] = jnp.zeros_like(acc_ref)
    m_ref[...] = jnp.full_like(m_ref, M_FLOOR)

    def scores_stage(kb, h):
        r0 = pl.multiple_of(kb * KEY_BLOCK, KEY_BLOCK)
        s = lax.dot_general(kh_ref[kb, h], q_ref[:, h * HEAD_DIM:(h + 1) * HEAD_DIM], _NT,
                            preferred_element_type=F32) + bias_ref[pl.ds(r0, KEY_BLOCK), :]
        r_ref[h % R_SLOTS][...] = s
        m_old = m_ref[h]
        m_blk = jnp.max(jnp.max(s.reshape(KEY_BLOCK // 8, 8, Q_TILE), axis=0), axis=0, keepdims=True)
        m_new = jnp.maximum(m_old, m_blk)
        m_ref[h] = m_new
        return m_new, jnp.exp2(m_old - m_new)

    first = ()
    for h in range(S_AHEAD):
        first += scores_stage(0, h)

    def attn_body(kb, carry):
        kb_next = jnp.minimum(kb + 1, nkb - 1)

        def pv_update(h, alpha):
            acc_ref[h] = alpha * acc_ref[h] + jnp.dot(vt_ref[kb, h], p_ref[h % P_SLOTS][...],
                                                      preferred_element_type=F32)

        stats = [carry[2 * a:2 * a + 2] for a in range(S_AHEAD)]
        pending = []
        for h in range(N_HEADS):
            kb2, h2 = (kb, h + S_AHEAD) if h + S_AHEAD < N_HEADS else (kb_next, h + S_AHEAD - N_HEADS)
            stats.append(scores_stage(kb2, h2))
            if len(pending) == PV_LAG:
                pv_update(*pending.pop(0))
            m_new, alpha = stats.pop(0)
            p_ref[h % P_SLOTS][...] = jnp.exp2(r_ref[h % R_SLOTS][...] - m_new).astype(BF16)
            pending.append((h, alpha))
        for item in pending:
            pv_update(*item)
        out = ()
        for st in stats:
            out += st
        return out

    lax.fori_loop(0, nkb, attn_body, first)

    for h in range(N_HEADS):
        a = acc_ref[h]
        out_t = a[:HEAD_DIM, :] * (1.0 / a[HEAD_DIM:HEAD_DIM + 1, :])
        gate = ga_ref[:, h * HEAD_DIM:(h + 1) * HEAD_DIM].astype(F32)
        o_ref[:, h * HEAD_DIM:(h + 1) * HEAD_DIM] = (
            out_t.T * _silu(gate)).astype(o_ref.dtype)


def _sparse_attention(qqi, wixt, kix, kh, vt, gates, k_top):
    B, S, _ = qqi.shape
    DA = N_HEADS * HEAD_DIM
    DI = N_IDX_HEADS * IDX_DIM
    nq = S // Q_TILE
    return pl.pallas_call(
        functools.partial(_attn_kernel, k_top=k_top),
        grid=(B, nq),
        in_specs=[pl.BlockSpec((None, Q_TILE, DA), lambda b, i: (b, i, 0)),
                  pl.BlockSpec((None, Q_TILE, DI), lambda b, i: (b, i, DA // DI)),
                  pl.BlockSpec((None, N_IDX_HEADS, Q_TILE), lambda b, i: (b, 0, i)),
                  pl.BlockSpec((None, S, 2 * LANES), lambda b, i: (b, 0, 0)),
                  pl.BlockSpec((None,) + kh.shape[1:], lambda b, i: (b, 0, 0, 0, 0)),
                  pl.BlockSpec((None,) + vt.shape[1:], lambda b, i: (b, 0, 0, 0, 0)),
                  pl.BlockSpec((Q_TILE, DA), lambda b, i: (b * nq + i, 0))],
        out_specs=pl.BlockSpec((Q_TILE, DA), lambda b, i: (b * nq + i, 0)),
        out_shape=jax.ShapeDtypeStruct((B * S, DA), BF16),
        scratch_shapes=[pltpu.VMEM((S, Q_TILE), F32),
                        pltpu.VMEM((S, Q_TILE), BF16),
                        pltpu.VMEM((S, Q_TILE), F32),
                        pltpu.VMEM((N_HEADS, V_ROWS, Q_TILE), F32),
                        pltpu.VMEM((N_HEADS, 1, Q_TILE), F32)]
                       + [pltpu.VMEM((KEY_BLOCK, Q_TILE), F32)] * R_SLOTS
                       + [pltpu.VMEM((KEY_BLOCK, Q_TILE), BF16)] * P_SLOTS,
        compiler_params=_cparams(("parallel", "arbitrary")),
        name="sparse_attention",
    )(qqi, qqi, wixt, kix, kh, vt, gates)


POOL_HALO = 16


def _pool_bands(ts, dtype):
    r = lax.broadcasted_iota(jnp.int32, (ts, ts), 0)
    c = lax.broadcasted_iota(jnp.int32, (ts, ts), 1)
    hr = lax.broadcasted_iota(jnp.int32, (POOL_HALO, POOL_HALO), 0)
    hc = lax.broadcasted_iota(jnp.int32, (POOL_HALO, POOL_HALO), 1)
    bands = [jnp.where((c <= r) & (c > r - win), 1.0 / win, 0.0) - jnp.where(c == r, 1.0, 0.0)
             for win in POOL_WINDOWS]
    hbands = [jnp.where(hc >= hr + (POOL_HALO + 1 - win), 1.0 / win, 0.0) for win in POOL_WINDOWS]
    return jnp.stack(bands).astype(dtype), jnp.stack(hbands).astype(dtype)


def _pool_kernel(pin_ref, halo_ref, gate_ref, wp_ref, ps_ref, band_ref, hband_ref, o_ref, mix_ref):
    i = pl.program_id(1)
    ts = pin_ref.shape[0]

    @pl.when(i == 0)
    def _():
        t = lax.broadcasted_iota(jnp.int32, (ts, POOL_GROUP), 0)
        for g, win in enumerate(POOL_WINDOWS):
            cols = slice(g * POOL_GROUP, (g + 1) * POOL_GROUP)
            cur = pin_ref[:, cols].astype(F32)
            ssum = jnp.concatenate([jnp.zeros((POOL_HALO, POOL_GROUP), F32), cur], axis=0)
            span = 1
            while span < win:
                ssum = ssum + pltpu.roll(ssum, span, 0)
                span *= 2
            count = jnp.minimum(t + 1, win).astype(F32)
            mix_ref[:, cols] = (ssum[POOL_HALO:, :] / count - cur).astype(mix_ref.dtype)

    @pl.when(i > 0)
    def _():
        for g in range(len(POOL_WINDOWS)):
            cols = slice(g * POOL_GROUP, (g + 1) * POOL_GROUP)
            m = jnp.dot(band_ref[g], pin_ref[:, cols], preferred_element_type=F32)
            top = m[:POOL_HALO, :] + jnp.dot(hband_ref[g], halo_ref[:, cols], preferred_element_type=F32)
            mix_ref[:POOL_HALO, cols] = top.astype(mix_ref.dtype)
            mix_ref[POOL_HALO:, cols] = m[POOL_HALO:, :].astype(mix_ref.dtype)

    for g in range(len(POOL_WINDOWS)):
        cols = slice(g * POOL_GROUP, (g + 1) * POOL_GROUP)
        y = jnp.dot(mix_ref[:, cols], wp_ref[g], preferred_element_type=F32)
        gate = gate_ref[:, cols].astype(F32)
        o_ref[:, cols] = (y * ps_ref[:, cols] * _silu(gate)).astype(o_ref.dtype)


def _pool_mixer(gates, w_pool, pool_scale, B, S, ts=256):
    T = gates.shape[0]
    D = gates.shape[1] // 3
    nS = S // ts
    hb = ts // POOL_HALO
    bands, hbands = _pool_bands(ts, gates.dtype)
    return pl.pallas_call(
        _pool_kernel,
        grid=(B, nS),
        in_specs=[pl.BlockSpec((ts, D), lambda b, i: (b * nS + i, 1)),
                  pl.BlockSpec((POOL_HALO, D), lambda b, i: (jnp.maximum((b * nS + i) * hb - 1, 0), 1)),
                  pl.BlockSpec((ts, D), lambda b, i: (b * nS + i, 2)),
                  pl.BlockSpec(w_pool.shape, lambda b, i: (0, 0, 0)),
                  pl.BlockSpec((1, D), lambda b, i: (0, 0)),
                  pl.BlockSpec(bands.shape, lambda b, i: (0, 0, 0)),
                  pl.BlockSpec(hbands.shape, lambda b, i: (0, 0, 0))],
        out_specs=pl.BlockSpec((ts, D), lambda b, i: (b * nS + i, 0)),
        out_shape=jax.ShapeDtypeStruct((T, D), BF16),
        scratch_shapes=[pltpu.VMEM((ts, D), BF16)],
        compiler_params=_cparams(("parallel", "arbitrary")),
        name="pool_mixer",
    )(gates, gates, gates, w_pool, pool_scale.reshape(1, D), bands, hbands)


OUT_COL_TILE = 512
OUT_PROJ_VMEM_LIMIT = 60 * 1024 * 1024


def _outproj_kernel(ya_ref, yb_ref, w_ref, x_ref, g_ref, o_ref):
    da = ya_ref.shape[1]
    tm, d = o_ref.shape
    n_col = d // OUT_COL_TILE
    ss = jnp.zeros((tm, LANES), F32)
    for jj in range(n_col):
        cols = slice(jj * OUT_COL_TILE, (jj + 1) * OUT_COL_TILE)
        y = (jnp.dot(ya_ref[...], w_ref[:da, cols], preferred_element_type=F32)
             + jnp.dot(yb_ref[...], w_ref[da:, cols], preferred_element_type=F32))
        o_ref[:, cols] = y
        for k in range(OUT_COL_TILE // LANES):
            yk = y[:, k * LANES:(k + 1) * LANES]
            ss = ss + yk * yk
    rs = lax.rsqrt(jnp.sum(ss, axis=-1, keepdims=True) * (1.0 / d) + NORM_EPS)
    for jj in range(n_col):
        cols = slice(jj * OUT_COL_TILE, (jj + 1) * OUT_COL_TILE)
        o_ref[:, cols] = x_ref[:, cols] + o_ref[:, cols] * rs * g_ref[:, cols]


def _out_proj(ya, yb, w_out_b, x2, post_g, tm=256):
    T, D = x2.shape
    DA = ya.shape[1]
    DB = yb.shape[1]
    return pl.pallas_call(
        _outproj_kernel,
        grid=(T // tm,),
        in_specs=[pl.BlockSpec((tm, DA), lambda i: (i, 0)),
                  pl.BlockSpec((tm, DB), lambda i: (i, 0)),
                  pl.BlockSpec((DA + DB, D), lambda i: (0, 0), pipeline_mode=pl.Buffered(1)),
                  pl.BlockSpec((tm, D), lambda i: (i, 0)),
                  pl.BlockSpec((1, D), lambda i: (0, 0))],
        out_specs=pl.BlockSpec((tm, D), lambda i: (i, 0)),
        out_shape=jax.ShapeDtypeStruct((T, D), F32),
        compiler_params=pltpu.CompilerParams(dimension_semantics=("parallel",),
                                             vmem_limit_bytes=OUT_PROJ_VMEM_LIMIT),
        name="out_proj",
    )(ya, yb, w_out_b, x2, post_g.reshape(1, D))


PROJ_TN = 512
PROJ_TM = 1024
GATES_TN = 1024


def _layer(x2, B, S, pre_g, w_in, kv_g, w_uk, w_uv, ln_g, ln_b, w_pool, pool_scale, w_out, post_g, k_top):
    d_attn = N_HEADS * HEAD_DIM
    n_qi = N_IDX_HEADS * IDX_DIM
    o_c = d_attn
    o_qi = o_c + KV_LATENT
    o_k = o_qi + n_qi
    o_w = o_k + IDX_DIM
    o_g = o_w + N_IDX_HEADS
    K, n_in = w_in.shape
    wt = w_in.T
    w_uk_all = jnp.transpose(w_uk, (1, 0, 2)).reshape(KV_LATENT, d_attn).astype(BF16)
    w_uvt_all = jnp.swapaxes(w_uv, 1, 2).reshape(d_attn, KV_LATENT).astype(BF16)
    w_pool_b = w_pool.astype(BF16)

    tn = PROJ_TN
    tm = min(PROJ_TM, x2.shape[0])
    assert o_c % tn == 0 and o_qi % tn == 0 and n_qi % tn == 0
    qqi_spec = pl.BlockSpec((tn, K), lambda i, j: (jnp.where(j < o_c // tn, j, j + (o_qi - o_c) // tn), 0))
    h0 = _prenorm(x2, pre_g, tm)
    qqi, h = _norm_proj(x2, pre_g, h0, wt, qqi_spec, d_attn + n_qi, tm, tn,
                        n_scaled=o_c // tn, scale=(HEAD_DIM ** -0.5) * LOG2E)
    kh, vt, kix, wixt, wg = _latent_proj(h, wt, o_c, o_k, o_g, kv_g, ln_g, ln_b, w_uk_all, w_uvt_all, B, S)
    gates_spec = pl.BlockSpec((GATES_TN, K), lambda i, j: (j, 0))
    gates, w_out_b = _matmul_nt(h, wg, gates_spec, n_in - o_g, BF16, tm, GATES_TN, "proj_gates", side=w_out)
    ya = _sparse_attention(qqi.reshape(B, S, d_attn + n_qi), wixt, kix, kh, vt, gates, k_top)
    yb = _pool_mixer(gates, w_pool_b, pool_scale, B, S)
    return _out_proj(ya, yb, w_out_b, x2, post_g)


def kernel(x, pre_norm, w_in, kv_norm, w_uk, w_uv, idx_k_norm_g, idx_k_norm_b, w_pool, pool_scale, w_out, post_norm):
    B, S, D = x.shape
    k_top = min(TOPK_MAX, S // 4)
    x2 = x.reshape(B * S, D)
    for l in range(pre_norm.shape[0]):
        x2 = _layer(x2, B, S, pre_norm[l], w_in[l], kv_norm[l], w_uk[l], w_uv[l], idx_k_norm_g[l],
                    idx_k_norm_b[l], w_pool[l], pool_scale[l], w_out[l], post_norm[l], k_top)
    return x2.reshape(B, S, D)
```

```python
import functools
import math

import jax
import jax.numpy as jnp
from jax import lax
from jax.experimental import pallas as pl
from jax.experimental.pallas import tpu as pltpu

F32 = jnp.float32
BF16 = jnp.bfloat16

NORM_EPS = 1e-6
CHUNK = 64
N_HEADS = 16
HEAD_DIM = 128
KV_LATENT = 512
N_IDX_HEADS = 16
IDX_DIM = 64
TOPK_MAX = 256
POOL_WINDOWS = (2, 4, 8, 16)
POOL_GROUP = 512

LANES = 128
Q_TILE = 256
KEY_BLOCK = 256
BF16_SUBLANES = 16
V_ROWS = HEAD_DIM + BF16_SUBLANES
S_AHEAD = 3
R_SLOTS = 4
PV_LAG = 2
P_SLOTS = 4
assert KEY_BLOCK % Q_TILE == 0
HALF_BITS = 16
FINE_BITS = 18
M_FLOOR = -1e30
LOG2E = math.log2(math.e)
VMEM_LIMIT = 56 * 1024 * 1024


def _cparams(sem):
    return pltpu.CompilerParams(dimension_semantics=sem, vmem_limit_bytes=VMEM_LIMIT)


def _silu(x):
    h = 0.5 * x
    return h + h * jnp.tanh(h)


def _rms_rows(x, g):
    ms = jnp.mean(x * x, axis=-1, keepdims=True)
    return (x * lax.rsqrt(ms + NORM_EPS) * g).astype(BF16)


def _prenorm_kernel(x_ref, g_ref, o_ref):
    o_ref[...] = _rms_rows(x_ref[...], g_ref[...])


def _prenorm(x2, g, rows, tm=256):
    D = x2.shape[1]
    return pl.pallas_call(
        _prenorm_kernel,
        grid=(rows // tm,),
        in_specs=[pl.BlockSpec((tm, D), lambda i: (i, 0)),
                  pl.BlockSpec((1, D), lambda i: (0, 0))],
        out_specs=pl.BlockSpec((tm, D), lambda i: (i, 0)),
        out_shape=jax.ShapeDtypeStruct((rows, D), BF16),
        compiler_params=_cparams(("parallel",)),
        name="prenorm",
    )(x2, g.reshape(1, D))


_NT = (((1,), (1,)), ((), ()))


def _matmul_nt_kernel(h_ref, wt_ref, *rest, n_scaled, scale):
    o_ref = rest[-1] if len(rest) == 1 else rest[1]
    w = wt_ref[...]
    if w.dtype != BF16:
        w = w.astype(BF16)
    z = lax.dot_general(h_ref[...], w, _NT, preferred_element_type=F32)
    if n_scaled:
        z = z * jnp.where(pl.program_id(1) < n_scaled, jnp.float32(scale), jnp.float32(1.0))
    o_ref[...] = z.astype(o_ref.dtype)
    if len(rest) == 3:
        side_ref, _, side_out_ref = rest
        side_out_ref[...] = side_ref[...].astype(side_out_ref.dtype)


def _matmul_nt(h, wt, w_spec, n_out, out_dtype, tm, tn, name, n_scaled=0, scale=1.0, side=None):
    T, K = h.shape
    tm = min(tm, T)
    n_i, n_j = T // tm, n_out // tn
    in_specs = [pl.BlockSpec((tm, K), lambda i, j: (i, 0)), w_spec]
    out_specs = [pl.BlockSpec((tm, tn), lambda i, j: (i, j))]
    out_shape = [jax.ShapeDtypeStruct((T, n_out), out_dtype)]
    args = [h, wt]
    if side is not None:
        R, C = side.shape
        nb = 1
        while nb * 2 <= min(n_i * n_j, R // BF16_SUBLANES):
            nb *= 2
        assert R % nb == 0
        side_map = lambda i, j: (jnp.minimum(i * n_j + j, nb - 1), 0)
        in_specs.append(pl.BlockSpec((R // nb, C), side_map))
        out_specs.append(pl.BlockSpec((R // nb, C), side_map))
        out_shape.append(jax.ShapeDtypeStruct((R, C), BF16))
        args.append(side)
    res = pl.pallas_call(
        functools.partial(_matmul_nt_kernel, n_scaled=n_scaled, scale=scale),
        grid=(n_i, n_j),
        in_specs=in_specs,
        out_specs=out_specs,
        out_shape=out_shape,
        compiler_params=_cparams(("parallel", "arbitrary")),
        name=name,
    )(*args)
    return res if side is not None else res[0]


NORM_SEG = 256
NORM_PROJ_VMEM_LIMIT = 60 * 1024 * 1024


def _norm_proj_kernel(x_ref, g_ref, h0_ref, wt_ref, o_ref, hall_ref, hbuf_a, hbuf_b, *,
                      n_seg, n_scaled, scale):
    i, j = pl.program_id(0), pl.program_id(1)

    @pl.when((i == 0) & (j == 0))
    def _():
        hbuf_a[...] = h0_ref[...]

    def step(cur_ref, nxt_ref, with_norm):
        z = lax.dot_general(cur_ref[...], wt_ref[...].astype(BF16), _NT, preferred_element_type=F32)
        z = z * jnp.where(j < n_scaled, jnp.float32(scale), jnp.float32(1.0))
        o_ref[...] = z.astype(o_ref.dtype)
        if with_norm:
            hseg = _rms_rows(x_ref[...], g_ref[...])
            hall_ref[...] = hseg
            nxt_ref[pl.ds(pl.multiple_of(j * NORM_SEG, NORM_SEG), NORM_SEG), :] = hseg

    for parity, (cur_ref, nxt_ref) in enumerate(((hbuf_a, hbuf_b), (hbuf_b, hbuf_a))):
        for with_norm in (True, False):
            @pl.when((i % 2 == parity) & ((j < n_seg) == with_norm))
            def _(cur_ref=cur_ref, nxt_ref=nxt_ref, with_norm=with_norm):
                step(cur_ref, nxt_ref, with_norm)


def _norm_proj(x2, g, h0, wt, w_spec, n_out, tm, tn, n_scaled, scale):
    T, D = x2.shape
    n_i, n_j = T // tm, n_out // tn
    n_seg = tm // NORM_SEG
    assert n_seg <= n_j and h0.shape == (tm, D)
    seg_map = lambda i, j: (((i + 1) % n_i) * n_seg + jnp.minimum(j, n_seg - 1), 0)
    return pl.pallas_call(
        functools.partial(_norm_proj_kernel, n_seg=n_seg, n_scaled=n_scaled, scale=scale),
        grid=(n_i, n_j),
        in_specs=[pl.BlockSpec((NORM_SEG, D), seg_map),
                  pl.BlockSpec((1, D), lambda i, j: (0, 0)),
                  pl.BlockSpec((tm, D), lambda i, j: (0, 0), pipeline_mode=pl.Buffered(1)),
                  w_spec],
        out_specs=[pl.BlockSpec((tm, tn), lambda i, j: (i, j)),
                   pl.BlockSpec((NORM_SEG, D), seg_map)],
        out_shape=[jax.ShapeDtypeStruct((T, n_out), BF16),
                   jax.ShapeDtypeStruct((T, D), BF16)],
        scratch_shapes=[pltpu.VMEM((tm, D), BF16), pltpu.VMEM((tm, D), BF16)],
        compiler_params=pltpu.CompilerParams(dimension_semantics=("arbitrary", "arbitrary"),
                                             vmem_limit_bytes=NORM_PROJ_VMEM_LIMIT),
        name="norm_proj_q_qi",
    )(x2, g.reshape(1, D), h0, wt)


def _latent_kernel(h_ref, wc_ref, wkw_ref, kvg_ref, lng_ref, lnb_ref, wuk_ref, wuvt_ref, wside_ref,
                   kh_ref, vt_ref, kix_ref, wixt_ref, wside_out_ref):
    wside_out_ref[...] = wside_ref[...].astype(wside_out_ref.dtype)
    h = h_ref[...]
    c = lax.dot_general(h, wc_ref[...].astype(BF16), _NT, preferred_element_type=F32)
    cn = c * lax.rsqrt(jnp.mean(c * c, axis=-1, keepdims=True) + NORM_EPS) * kvg_ref[...]
    cb = cn.astype(BF16)
    n_kb = kh_ref.shape[0]
    for pair in range(N_HEADS // 2):
        k2 = jnp.dot(cb, wuk_ref[:, pair * 2 * HEAD_DIM:(pair + 1) * 2 * HEAD_DIM],
                     preferred_element_type=F32)
        for u in range(n_kb):
            for e in range(2):
                kh_ref[u, 2 * pair + e] = k2[u * KEY_BLOCK:(u + 1) * KEY_BLOCK,
                                             e * HEAD_DIM:(e + 1) * HEAD_DIM].astype(kh_ref.dtype)
    for u in range(n_kb):
        ct_u = cn[u * KEY_BLOCK:(u + 1) * KEY_BLOCK, :].T.astype(BF16)
        v_all = jnp.dot(wuvt_ref[...], ct_u, preferred_element_type=F32)
        for hh in range(N_HEADS):
            vt_ref[u, hh, :HEAD_DIM, :] = v_all[hh * HEAD_DIM:(hh + 1) * HEAD_DIM, :].astype(vt_ref.dtype)
            vt_ref[u, hh, HEAD_DIM:, :] = jnp.ones((V_ROWS - HEAD_DIM, KEY_BLOCK), vt_ref.dtype)
    kw = lax.dot_general(h, wkw_ref[...].astype(BF16), _NT, preferred_element_type=F32)
    lane = lax.broadcasted_iota(jnp.int32, kw.shape, 1)
    is_k = lane < IDX_DIM
    mu = jnp.sum(jnp.where(is_k, kw, 0.0), axis=-1, keepdims=True) * (1.0 / IDX_DIM)
    d = jnp.where(is_k, kw - mu, 0.0)
    var = jnp.sum(d * d, axis=-1, keepdims=True) * (1.0 / IDX_DIM)
    kn = d * lax.rsqrt(var + NORM_EPS) * lng_ref[...] + lnb_ref[...]
    kix_ref[...] = jnp.concatenate([kn, pltpu.roll(kn, IDX_DIM, 1)], axis=1).astype(kix_ref.dtype)
    wixt_ref[...] = kw.T[IDX_DIM:IDX_DIM + N_IDX_HEADS, :] * (N_IDX_HEADS ** -0.5)


def _latent_proj(h, wt, o_c, o_k, o_side, kv_norm, ln_g, ln_b, w_uk_all, w_uvt_all, B, S, tm=512):
    T, K = h.shape
    nS = S // tm
    pad = jnp.zeros((LANES - IDX_DIM,), F32)
    lng = jnp.concatenate([ln_g, pad]).reshape(1, LANES)
    lnb = jnp.concatenate([ln_b, pad]).reshape(1, LANES)
    assert o_c % KV_LATENT == 0 and o_k % LANES == 0
    n_side = wt.shape[0] - o_side
    side_rows = n_side // (B * nS)
    assert side_rows * B * nS == n_side and side_rows % BF16_SUBLANES == 0 and o_side % BF16_SUBLANES == 0
    return pl.pallas_call(
        _latent_kernel,
        grid=(B, nS),
        in_specs=[pl.BlockSpec((tm, K), lambda b, i: (b * nS + i, 0)),
                  pl.BlockSpec((KV_LATENT, K), lambda b, i: (o_c // KV_LATENT, 0)),
                  pl.BlockSpec((LANES, K), lambda b, i: (o_k // LANES, 0)),
                  pl.BlockSpec((1, KV_LATENT), lambda b, i: (0, 0)),
                  pl.BlockSpec((1, LANES), lambda b, i: (0, 0)),
                  pl.BlockSpec((1, LANES), lambda b, i: (0, 0)),
                  pl.BlockSpec(w_uk_all.shape, lambda b, i: (0, 0)),
                  pl.BlockSpec(w_uvt_all.shape, lambda b, i: (0, 0)),
                  pl.BlockSpec((pl.Element(side_rows), pl.Element(K)),
                               lambda b, i: (pl.multiple_of(o_side + (b * nS + i) * side_rows, 16), 0))],
        out_specs=[pl.BlockSpec((None, tm // KEY_BLOCK, N_HEADS, KEY_BLOCK, HEAD_DIM),
                                lambda b, i: (b, i, 0, 0, 0)),
                   pl.BlockSpec((None, tm // KEY_BLOCK, N_HEADS, V_ROWS, KEY_BLOCK),
                                lambda b, i: (b, i, 0, 0, 0)),
                   pl.BlockSpec((None, tm, 2 * LANES), lambda b, i: (b, i, 0)),
                   pl.BlockSpec((None, N_IDX_HEADS, tm), lambda b, i: (b, 0, i)),
                   pl.BlockSpec((side_rows, K), lambda b, i: (b * nS + i, 0))],
        out_shape=[jax.ShapeDtypeStruct((B, S // KEY_BLOCK, N_HEADS, KEY_BLOCK, HEAD_DIM), BF16),
                   jax.ShapeDtypeStruct((B, S // KEY_BLOCK, N_HEADS, V_ROWS, KEY_BLOCK), BF16),
                   jax.ShapeDtypeStruct((B, S, 2 * LANES), BF16),
                   jax.ShapeDtypeStruct((B, N_IDX_HEADS, S), F32),
                   jax.ShapeDtypeStruct((n_side, K), BF16)],
        compiler_params=_cparams(("parallel", "arbitrary")),
        name="latent_proj",
    )(h, wt, wt, kv_norm.reshape(1, KV_LATENT), lng, lnb, w_uk_all, w_uvt_all, wt)


def _sortable_to_f32(key):
    bits = key ^ ((key >> 31) & jnp.int32(0x7FFFFFFF))
    return pltpu.bitcast(bits, F32)


def _fold8(a):
    return jnp.sum(a.reshape(a.shape[0] // 8, 8, a.shape[1]), axis=0)


def _attn_kernel(q_ref, qi_ref, wixt_ref, kix_ref, kh_ref, vt_ref, ga_ref,
                 o_ref, isc_ref, hsc_ref, bias_ref, acc_ref, m_ref, *stage_refs, k_top):
    r_ref, p_ref = stage_refs[:R_SLOTS], stage_refs[R_SLOTS:]
    i = pl.program_id(1)
    t0 = i * Q_TILE
    nkb = (t0 + Q_TILE + KEY_BLOCK - 1) // KEY_BLOCK
    neg_inf = jnp.float32(-jnp.inf)

    t_idx = t0 + lax.broadcasted_iota(jnp.int32, (KEY_BLOCK, Q_TILE), 1)
    q_chunk = t_idx // CHUNK

    def admissible(kb):
        s_idx = kb * KEY_BLOCK + lax.broadcasted_iota(jnp.int32, (KEY_BLOCK, Q_TILE), 0)
        return (s_idx // CHUNK) <= q_chunk

    wix = wixt_ref[...] * (IDX_DIM ** -0.5)

    def idx_body(kb, carry):
        r0 = pl.multiple_of(kb * KEY_BLOCK, KEY_BLOCK)
        kx = kix_ref[pl.ds(r0, KEY_BLOCK), :]
        kx_even, kx_odd = kx[:, :LANES], kx[:, LANES:]
        acc = jnp.zeros((KEY_BLOCK, Q_TILE), F32)
        for pair in range(N_IDX_HEADS // 2):
            rhs = qi_ref[:, pair * LANES:(pair + 1) * LANES]
            for par, kxp in ((0, kx_even), (1, kx_odd)):
                h = 2 * pair + par
                d = lax.dot_general(kxp, rhs, (((1,), (1,)), ((), ())), preferred_element_type=F32)
                acc = acc + wix[h:h + 1, :] * jnp.maximum(d, 0.0)
        isc_ref[pl.ds(r0, KEY_BLOCK), :] = acc
        hsc_ref[pl.ds(r0, KEY_BLOCK), :] = acc.astype(BF16)
        return carry

    lax.fori_loop(0, nkb, idx_body, 0)
    diag_r0 = pl.multiple_of((nkb - 1) * KEY_BLOCK, KEY_BLOCK)
    diag = jnp.where(admissible(nkb - 1), isc_ref[pl.ds(diag_r0, KEY_BLOCK), :], neg_inf)
    isc_ref[pl.ds(diag_r0, KEY_BLOCK), :] = diag
    hsc_ref[pl.ds(diag_r0, KEY_BLOCK), :] = diag.astype(BF16)

    def count(pred):
        def body(kb, cnt):
            r0 = pl.multiple_of(kb * KEY_BLOCK, KEY_BLOCK)
            v = isc_ref[pl.ds(r0, KEY_BLOCK), :]
            return cnt + _fold8(jnp.where(pred(v), 1, 0).astype(jnp.int32))
        c8 = lax.fori_loop(0, nkb, body, jnp.zeros((8, Q_TILE), jnp.int32))
        return jnp.sum(c8, axis=0, keepdims=True)

    def count_half(cand):
        one, zero = jnp.ones((), BF16), jnp.zeros((), BF16)

        def body(kb, cnt):
            r0 = pl.multiple_of(kb * KEY_BLOCK, KEY_BLOCK)
            ind = jnp.where(hsc_ref[pl.ds(r0, KEY_BLOCK), :] >= cand, one, zero)
            parts = [ind[k * BF16_SUBLANES:(k + 1) * BF16_SUBLANES, :]
                     for k in range(KEY_BLOCK // BF16_SUBLANES)]
            while len(parts) > 1:
                parts = [a + b for a, b in zip(parts[0::2], parts[1::2])]
            return cnt + parts[0].astype(F32)

        c16 = lax.fori_loop(0, nkb, body, jnp.zeros((BF16_SUBLANES, Q_TILE), F32))
        return jnp.sum(c16, axis=0, keepdims=True)

    def half_to_bf16(k16):
        bits = k16 ^ ((k16 >> 31) & jnp.int32(0x7FFF))
        return pltpu.bitcast(jnp.left_shift(bits, 16), F32).astype(BF16)

    def coarse_body(step, k16):
        cand = k16 + jnp.left_shift(jnp.int32(1), HALF_BITS - 1 - step)
        return jnp.where(count_half(half_to_bf16(cand)) >= k_top, cand, k16)

    def fine_body(step, carry):
        key, key_cnt = carry
        cand = key + jnp.left_shift(jnp.int32(1), FINE_BITS - 1 - step)
        n_ge = count(lambda v: v >= _sortable_to_f32(cand))
        take = n_ge >= k_top
        return jnp.where(take, cand, key), jnp.where(take, n_ge, key_cnt)

    searching = t0 + Q_TILE > k_top
    k16 = lax.fori_loop(0, jnp.where(searching, HALF_BITS, 0), coarse_body,
                        jnp.full((1, Q_TILE), -2 ** (HALF_BITS - 1), jnp.int32))
    key_lo = jnp.left_shift(k16, 16) - jnp.int32(1 << 16)
    key, key_cnt = lax.fori_loop(0, jnp.where(searching, FINE_BITS, 0), fine_body,
                                 (key_lo, jnp.full((1, Q_TILE), k_top + 1, jnp.int32)))
    n_adm = (t_idx[0:1, :] // CHUNK + 1) * CHUNK
    take_all = n_adm <= k_top
    thr = jnp.where(take_all, neg_inf, _sortable_to_f32(key))
    has_ties = jnp.max(jnp.where(take_all, 0, key_cnt - k_top)) > 0

    @pl.when(jnp.logical_not(has_ties))
    def _():
        def sel_body(kb, carry):
            r0 = pl.multiple_of(kb * KEY_BLOCK, KEY_BLOCK)
            bias_ref[pl.ds(r0, KEY_BLOCK), :] = jnp.where(isc_ref[pl.ds(r0, KEY_BLOCK), :] >= thr,
                                                          0.0, neg_inf)
            return carry

        lax.fori_loop(0, nkb - 1, sel_body, 0)
        sel = (isc_ref[pl.ds(diag_r0, KEY_BLOCK), :] >= thr) & admissible(nkb - 1)
        bias_ref[pl.ds(diag_r0, KEY_BLOCK), :] = jnp.where(sel, 0.0, neg_inf)

    @pl.when(has_ties)
    def _():
        n_gt = count(lambda v: v > thr)
        need = (k_top - n_gt).astype(F32)
        tri = (lax.broadcasted_iota(jnp.int32, (KEY_BLOCK, KEY_BLOCK), 1)
               <= lax.broadcasted_iota(jnp.int32, (KEY_BLOCK, KEY_BLOCK), 0)).astype(BF16)

        def sel_body(kb, tie_carry):
            r0 = pl.multiple_of(kb * KEY_BLOCK, KEY_BLOCK)
            v = isc_ref[pl.ds(r0, KEY_BLOCK), :]
            tie = (v == thr).astype(BF16)
            rank = jnp.dot(tri, tie, preferred_element_type=F32) + tie_carry
            sel = ((v > thr) | ((v == thr) & (rank <= need))) & admissible(kb)
            bias_ref[pl.ds(r0, KEY_BLOCK), :] = jnp.where(sel, 0.0, neg_inf)
            return rank[KEY_BLOCK - 1:KEY_BLOCK, :]

        lax.fori_loop(0, nkb, sel_body, jnp.zeros((1, Q_TILE), F32))

    acc_ref[...] = jnp.zeros_like(acc_ref)
    m_ref[...] = jnp.full_like(m_ref, M_FLOOR)

    def scores_stage(kb, h):
        r0 = pl.multiple_of(kb * KEY_BLOCK, KEY_BLOCK)
        s = lax.dot_general(kh_ref[kb, h], q_ref[:, h * HEAD_DIM:(h + 1) * HEAD_DIM], _NT,
                            preferred_element_type=F32) + bias_ref[pl.ds(r0, KEY_BLOCK), :]
        r_ref[h % R_SLOTS][...] = s
        m_old = m_ref[h]
        m_blk = jnp.max(jnp.max(s.reshape(KEY_BLOCK // 8, 8, Q_TILE), axis=0), axis=0, keepdims=True)
        m_new = jnp.maximum(m_old, m_blk)
        m_ref[h] = m_new
        return m_new, jnp.exp2(m_old - m_new)

    first = ()
    for h in range(S_AHEAD):
        first += scores_stage(0, h)

    def attn_body(kb, carry):
        kb_next = jnp.minimum(kb + 1, nkb - 1)

        def pv_update(h, alpha):
            acc_ref[h] = alpha * acc_ref[h] + jnp.dot(vt_ref[kb, h], p_ref[h % P_SLOTS][...],
                                                      preferred_element_type=F32)

        stats = [carry[2 * a:2 * a + 2] for a in range(S_AHEAD)]
        pending = []
        for h in range(N_HEADS):
            kb2, h2 = (kb, h + S_AHEAD) if h + S_AHEAD < N_HEADS else (kb_next, h + S_AHEAD - N_HEADS)
            stats.append(scores_stage(kb2, h2))
            if len(pending) == PV_LAG:
                pv_update(*pending.pop(0))
            m_new, alpha = stats.pop(0)
            p_ref[h % P_SLOTS][...] = jnp.exp2(r_ref[h % R_SLOTS][...] - m_new).astype(BF16)
            pending.append((h, alpha))
        for item in pending:
            pv_update(*item)
        out = ()
        for st in stats:
            out += st
        return out

    lax.fori_loop(0, nkb, attn_body, first)

    for h in range(N_HEADS):
        a = acc_ref[h]
        out_t = a[:HEAD_DIM, :] * (1.0 / a[HEAD_DIM:HEAD_DIM + 1, :])
        gate = ga_ref[:, h * HEAD_DIM:(h + 1) * HEAD_DIM].astype(F32)
        o_ref[:, h * HEAD_DIM:(h + 1) * HEAD_DIM] = (
            out_t.T * _silu(gate)).astype(o_ref.dtype)


def _sparse_attention(qqi, wixt, kix, kh, vt, gates, k_top):
    B, S, _ = qqi.shape
    DA = N_HEADS * HEAD_DIM
    DI = N_IDX_HEADS * IDX_DIM
    nq = S // Q_TILE
    return pl.pallas_call(
        functools.partial(_attn_kernel, k_top=k_top),
        grid=(B, nq),
        in_specs=[pl.BlockSpec((None, Q_TILE, DA), lambda b, i: (b, i, 0)),
                  pl.BlockSpec((None, Q_TILE, DI), lambda b, i: (b, i, DA // DI)),
                  pl.BlockSpec((None, N_IDX_HEADS, Q_TILE), lambda b, i: (b, 0, i)),
                  pl.BlockSpec((None, S, 2 * LANES), lambda b, i: (b, 0, 0)),
                  pl.BlockSpec((None,) + kh.shape[1:], lambda b, i: (b, 0, 0, 0, 0)),
                  pl.BlockSpec((None,) + vt.shape[1:], lambda b, i: (b, 0, 0, 0, 0)),
                  pl.BlockSpec((Q_TILE, DA), lambda b, i: (b * nq + i, 0))],
        out_specs=pl.BlockSpec((Q_TILE, DA), lambda b, i: (b * nq + i, 0)),
        out_shape=jax.ShapeDtypeStruct((B * S, DA), BF16),
        scratch_shapes=[pltpu.VMEM((S, Q_TILE), F32),
                        pltpu.VMEM((S, Q_TILE), BF16),
                        pltpu.VMEM((S, Q_TILE), F32),
                        pltpu.VMEM((N_HEADS, V_ROWS, Q_TILE), F32),
                        pltpu.VMEM((N_HEADS, 1, Q_TILE), F32)]
                       + [pltpu.VMEM((KEY_BLOCK, Q_TILE), F32)] * R_SLOTS
                       + [pltpu.VMEM((KEY_BLOCK, Q_TILE), BF16)] * P_SLOTS,
        compiler_params=_cparams(("parallel", "arbitrary")),
        name="sparse_attention",
    )(qqi, qqi, wixt, kix, kh, vt, gates)


POOL_HALO = 16


def _pool_bands(ts, dtype):
    r = lax.broadcasted_iota(jnp.int32, (ts, ts), 0)
    c = lax.broadcasted_iota(jnp.int32, (ts, ts), 1)
    hr = lax.broadcasted_iota(jnp.int32, (POOL_HALO, POOL_HALO), 0)
    hc = lax.broadcasted_iota(jnp.int32, (POOL_HALO, POOL_HALO), 1)
    bands = [jnp.where((c <= r) & (c > r - win), 1.0 / win, 0.0) - jnp.where(c == r, 1.0, 0.0)
             for win in POOL_WINDOWS]
    hbands = [jnp.where(hc >= hr + (POOL_HALO + 1 - win), 1.0 / win, 0.0) for win in POOL_WINDOWS]
    return jnp.stack(bands).astype(dtype), jnp.stack(hbands).astype(dtype)


def _pool_kernel(pin_ref, halo_ref, gate_ref, wp_ref, ps_ref, band_ref, hband_ref, o_ref, mix_ref):
    i = pl.program_id(1)
    ts = pin_ref.shape[0]

    @pl.when(i == 0)
    def _():
        t = lax.broadcasted_iota(jnp.int32, (ts, POOL_GROUP), 0)
        for g, win in enumerate(POOL_WINDOWS):
            cols = slice(g * POOL_GROUP, (g + 1) * POOL_GROUP)
            cur = pin_ref[:, cols].astype(F32)
            ssum = jnp.concatenate([jnp.zeros((POOL_HALO, POOL_GROUP), F32), cur], axis=0)
            span = 1
            while span < win:
                ssum = ssum + pltpu.roll(ssum, span, 0)
                span *= 2
            count = jnp.minimum(t + 1, win).astype(F32)
            mix_ref[:, cols] = (ssum[POOL_HALO:, :] / count - cur).astype(mix_ref.dtype)

    @pl.when(i > 0)
    def _():
        for g in range(len(POOL_WINDOWS)):
            cols = slice(g * POOL_GROUP, (g + 1) * POOL_GROUP)
            m = jnp.dot(band_ref[g], pin_ref[:, cols], preferred_element_type=F32)
            top = m[:POOL_HALO, :] + jnp.dot(hband_ref[g], halo_ref[:, cols], preferred_element_type=F32)
            mix_ref[:POOL_HALO, cols] = top.astype(mix_ref.dtype)
            mix_ref[POOL_HALO:, cols] = m[POOL_HALO:, :].astype(mix_ref.dtype)

    for g in range(len(POOL_WINDOWS)):
        cols = slice(g * POOL_GROUP, (g + 1) * POOL_GROUP)
        y = jnp.dot(mix_ref[:, cols], wp_ref[g], preferred_element_type=F32)
        gate = gate_ref[:, cols].astype(F32)
        o_ref[:, cols] = (y * ps_ref[:, cols] * _silu(gate)).astype(o_ref.dtype)


def _pool_mixer(gates, w_pool, pool_scale, B, S, ts=256):
    T = gates.shape[0]
    D = gates.shape[1] // 3
    nS = S // ts
    hb = ts // POOL_HALO
    bands, hbands = _pool_bands(ts, gates.dtype)
    return pl.pallas_call(
        _pool_kernel,
        grid=(B, nS),
        in_specs=[pl.BlockSpec((ts, D), lambda b, i: (b * nS + i, 1)),
                  pl.BlockSpec((POOL_HALO, D), lambda b, i: (jnp.maximum((b * nS + i) * hb - 1, 0), 1)),
                  pl.BlockSpec((ts, D), lambda b, i: (b * nS + i, 2)),
                  pl.BlockSpec(w_pool.shape, lambda b, i: (0, 0, 0)),
                  pl.BlockSpec((1, D), lambda b, i: (0, 0)),
                  pl.BlockSpec(bands.shape, lambda b, i: (0, 0, 0)),
                  pl.BlockSpec(hbands.shape, lambda b, i: (0, 0, 0))],
        out_specs=pl.BlockSpec((ts, D), lambda b, i: (b * nS + i, 0)),
        out_shape=jax.ShapeDtypeStruct((T, D), BF16),
        scratch_shapes=[pltpu.VMEM((ts, D), BF16)],
        compiler_params=_cparams(("parallel", "arbitrary")),
        name="pool_mixer",
    )(gates, gates, gates, w_pool, pool_scale.reshape(1, D), bands, hbands)


OUT_COL_TILE = 512
OUT_PROJ_VMEM_LIMIT = 60 * 1024 * 1024


def _outproj_kernel(ya_ref, yb_ref, w_hbm, x_ref, g_ref, o_ref, w_ref, sem):
    da = ya_ref.shape[1]
    tm, d = o_ref.shape
    n_col = d // OUT_COL_TILE

    def tile_copy(jj):
        cols = pl.ds(jj * OUT_COL_TILE, OUT_COL_TILE)
        return pltpu.make_async_copy(w_hbm.at[:, cols], w_ref.at[:, cols], sem.at[jj])

    def rows(first_step):
        if first_step:
            for jj in range(n_col):
                tile_copy(jj).start()
        ss = jnp.zeros((tm, LANES), F32)
        for jj in range(n_col):
            if first_step:
                tile_copy(jj).wait()
            cols = slice(jj * OUT_COL_TILE, (jj + 1) * OUT_COL_TILE)
            y = (jnp.dot(ya_ref[...], w_ref[:da, cols], preferred_element_type=F32)
                 + jnp.dot(yb_ref[...], w_ref[da:, cols], preferred_element_type=F32))
            o_ref[:, cols] = y
            for k in range(OUT_COL_TILE // LANES):
                yk = y[:, k * LANES:(k + 1) * LANES]
                ss = ss + yk * yk
        rs = lax.rsqrt(jnp.sum(ss, axis=-1, keepdims=True) * (1.0 / d) + NORM_EPS)
        for jj in range(n_col):
            cols = slice(jj * OUT_COL_TILE, (jj + 1) * OUT_COL_TILE)
            o_ref[:, cols] = x_ref[:, cols] + o_ref[:, cols] * rs * g_ref[:, cols]

    pl.when(pl.program_id(0) == 0)(functools.partial(rows, True))
    pl.when(pl.program_id(0) > 0)(functools.partial(rows, False))


def _out_proj(ya, yb, w_out_b, x2, post_g, tm=256):
    T, D = x2.shape
    DA = ya.shape[1]
    DB = yb.shape[1]
    return pl.pallas_call(
        _outproj_kernel,
        grid=(T // tm,),
        in_specs=[pl.BlockSpec((tm, DA), lambda i: (i, 0)),
                  pl.BlockSpec((tm, DB), lambda i: (i, 0)),
                  pl.BlockSpec(memory_space=pl.ANY),
                  pl.BlockSpec((tm, D), lambda i: (i, 0)),
                  pl.BlockSpec((1, D), lambda i: (0, 0))],
        out_specs=pl.BlockSpec((tm, D), lambda i: (i, 0)),
        out_shape=jax.ShapeDtypeStruct((T, D), F32),
        scratch_shapes=[pltpu.VMEM((DA + DB, D), BF16),
                        pltpu.SemaphoreType.DMA((D // OUT_COL_TILE,))],
        compiler_params=pltpu.CompilerParams(dimension_semantics=("arbitrary",),
                                             vmem_limit_bytes=OUT_PROJ_VMEM_LIMIT),
        name="out_proj",
    )(ya, yb, w_out_b, x2, post_g.reshape(1, D))


PROJ_TN = 512
PROJ_TM = 1024
GATES_TN = 1024


def _layer(x2, B, S, pre_g, w_in, kv_g, w_uk, w_uv, ln_g, ln_b, w_pool, pool_scale, w_out, post_g, k_top):
    d_attn = N_HEADS * HEAD_DIM
    n_qi = N_IDX_HEADS * IDX_DIM
    o_c = d_attn
    o_qi = o_c + KV_LATENT
    o_k = o_qi + n_qi
    o_w = o_k + IDX_DIM
    o_g = o_w + N_IDX_HEADS
    K, n_in = w_in.shape
    wt = w_in.T
    w_uk_all = jnp.transpose(w_uk, (1, 0, 2)).reshape(KV_LATENT, d_attn).astype(BF16)
    w_uvt_all = jnp.swapaxes(w_uv, 1, 2).reshape(d_attn, KV_LATENT).astype(BF16)
    w_pool_b = w_pool.astype(BF16)

    tn = PROJ_TN
    tm = min(PROJ_TM, x2.shape[0])
    assert o_c % tn == 0 and o_qi % tn == 0 and n_qi % tn == 0
    qqi_spec = pl.BlockSpec((tn, K), lambda i, j: (jnp.where(j < o_c // tn, j, j + (o_qi - o_c) // tn), 0))
    h0 = _prenorm(x2, pre_g, tm)
    qqi, h = _norm_proj(x2, pre_g, h0, wt, qqi_spec, d_attn + n_qi, tm, tn,
                        n_scaled=o_c // tn, scale=(HEAD_DIM ** -0.5) * LOG2E)
    kh, vt, kix, wixt, wg = _latent_proj(h, wt, o_c, o_k, o_g, kv_g, ln_g, ln_b, w_uk_all, w_uvt_all, B, S)
    gates_spec = pl.BlockSpec((GATES_TN, K), lambda i, j: (j, 0))
    gates, w_out_b = _matmul_nt(h, wg, gates_spec, n_in - o_g, BF16, tm, GATES_TN, "proj_gates", side=w_out)
    ya = _sparse_attention(qqi.reshape(B, S, d_attn + n_qi), wixt, kix, kh, vt, gates, k_top)
    yb = _pool_mixer(gates, w_pool_b, pool_scale, B, S, ts=512)
    return _out_proj(ya, yb, w_out_b, x2, post_g)


def kernel(x, pre_norm, w_in, kv_norm, w_uk, w_uv, idx_k_norm_g, idx_k_norm_b, w_pool, pool_scale, w_out, post_norm):
    B, S, D = x.shape
    k_top = min(TOPK_MAX, S // 4)
    x2 = x.reshape(B * S, D)
    for l in range(pre_norm.shape[0]):
        x2 = _layer(x2, B, S, pre_norm[l], w_in[l], kv_norm[l], w_uk[l], w_uv[l], idx_k_norm_g[l],
                    idx_k_norm_b[l], w_pool[l], pool_scale[l], w_out[l], post_norm[l], k_top)
    return x2.reshape(B, S, D)
```
